```python
import jax, jax.numpy as jnp
from jax import lax
import numpy as np

D_MODEL = 1024
BATCH = 4
SEQ = 4096
DEPTH = 1
DEC_BATCH = 128
DEC_SEQ = 4
PAST_LEN = 2048
PAGE_SIZE = 128

A_HEADS = 4
A_DQK = 128
A_DV = 256
A_CHUNK = 64
B_HEADS = 16
B_KV = 4
B_REP = B_HEADS // B_KV
B_HD = 64
CMP_BLOCK = 64
SEL_BLOCK = 64
N_SEL = 16
WINDOW = 512
SEL_QBLOCK = 64
WIN_QBLOCK = 128
P_HEADS = 8
P_NKEYS = 128
P_EXPERTS = P_NKEYS * P_NKEYS
P_DKEY = 256
P_DHALF = P_DKEY // 2
P_TOPK = 16
P_TBLOCK = 256
DN_ALPHA = (2.0 * DEPTH) ** 0.25
DN_BETA = (8.0 * DEPTH) ** -0.25
LN_EPS = 1e-5
NEG = -1e30

SPLIT_SIZES = (A_HEADS * A_DQK, A_HEADS * A_DQK, A_HEADS * A_DV, A_HEADS, A_HEADS, A_HEADS * A_DV,
               B_HEADS * B_HD, 2 * B_KV * B_HD, 2 * B_KV * B_HD, 2 * B_KV * B_HD, 3 * B_HEADS)
D_IN = sum(SPLIT_SIZES)

kernel_name = "hybrid_mlstm_nsa_peer_step"


def layer_norm(x, g, b):
    xf = x.astype(jnp.float32)
    mu = xf.mean(-1, keepdims=True)
    var = jnp.square(xf - mu).mean(-1, keepdims=True)
    return ((xf - mu) * lax.rsqrt(var + LN_EPS) * g + b).astype(x.dtype)


def alibi_slopes():
    s = 2.0 ** (-8.0 * np.arange(1, B_HEADS + 1) / B_HEADS)
    return jnp.asarray(s, jnp.float32).reshape(B_KV, B_REP)


def in_projection(x, w_in, b_in):
    proj = jnp.einsum('btd,de->bte', x, w_in) + b_in
    return jnp.split(proj, np.cumsum(SPLIT_SIZES)[:-1].tolist(), axis=-1)


def mlstm_scan(q, k, v, i_pre, log_f, C0, n0, m0, chunk):
    b_, h_, t_, _ = q.shape
    nc = t_ // chunk

    def to_chunks(a):
        return jnp.moveaxis(a.reshape(a.shape[:2] + (nc, chunk) + a.shape[3:]), 2, 0)

    causal = jnp.tril(jnp.ones((chunk, chunk), dtype=bool))

    def step(carry, inp):
        C, n, m = carry
        qc, kc, vc, ic, fc = inp
        b = jnp.cumsum(fc, axis=-1)
        d_log = jnp.where(causal, b[..., :, None] - b[..., None, :] + ic[..., None, :], -jnp.inf)
        inter = b + m[..., None]
        m_t = jnp.maximum(inter, d_log.max(-1))
        s = jnp.einsum('bhtd,bhsd->bhts', qc, kc) * jnp.exp(d_log - m_t[..., None])
        w_inter = jnp.exp(inter - m_t)
        num = w_inter[..., None] * jnp.einsum('bhtd,bhdv->bhtv', qc, C) + jnp.einsum('bhts,bhsv->bhtv', s, vc)
        den = w_inter * jnp.einsum('bhtd,bhd->bht', qc, n) + s.sum(-1)
        h = num / jnp.maximum(jnp.abs(den), jnp.exp(-m_t))[..., None]
        b_end = b[..., -1]
        g = b_end[..., None] - b + ic
        m_new = jnp.maximum(b_end + m, g.max(-1))
        a = jnp.exp(b_end + m - m_new)
        w = jnp.exp(g - m_new[..., None])
        C_new = a[..., None, None] * C + jnp.einsum('bhs,bhsd,bhsv->bhdv', w, kc, vc)
        n_new = a[..., None] * n + jnp.einsum('bhs,bhsd->bhd', w, kc)
        return (C_new, n_new, m_new), h

    (C, n, m), h = lax.scan(step, (C0, n0, m0), tuple(to_chunks(a) for a in (q, k, v, i_pre, log_f)))
    h = jnp.moveaxis(h, 0, 2).reshape(b_, h_, t_, v.shape[-1])
    return h, C, n, m


def mlstm_branch(a_q, a_k, a_v, a_i, a_f, a_o, C0, n0, m0, chunk, norm_a_g):
    B, T, _ = a_q.shape
    f32 = jnp.float32

    def heads(a, d):
        return a.reshape(B, T, A_HEADS, d).transpose(0, 2, 1, 3).astype(f32)

    q = heads(a_q, A_DQK)
    k = heads(a_k, A_DQK) * (A_DQK ** -0.5)
    v = heads(a_v, A_DV)
    i_pre = a_i.astype(f32).transpose(0, 2, 1)
    log_f = jax.nn.log_sigmoid(a_f.astype(f32)).transpose(0, 2, 1)
    h, C, n, m = mlstm_scan(q, k, v, i_pre, log_f, C0.astype(f32), n0.astype(f32), m0.astype(f32), chunk)
    mu = h.mean(-1, keepdims=True)
    var = jnp.square(h - mu).mean(-1, keepdims=True)
    h = ((h - mu) * lax.rsqrt(var + LN_EPS)).transpose(0, 2, 1, 3).reshape(B, T, A_HEADS * A_DV) * norm_a_g
    y = (h * jax.nn.sigmoid(a_o.astype(f32))).astype(a_q.dtype)
    return y, C, n, m


def dense_attn(q, k, v, q_pos, k_pos, slopes, window):
    s = jnp.einsum('...qgrd,...kgd->...grqk', q, k).astype(jnp.float32) * (B_HD ** -0.5)
    dist = q_pos[..., None, None, :, None] - k_pos[..., None, None, None, :]
    s = s - slopes[:, :, None, None] * dist.astype(jnp.float32)
    mask = (dist >= 0) & (k_pos >= 0)[..., None, None, None, :]
    if window is not None:
        mask = mask & (dist < window)
    p = jnp.where(mask, jax.nn.softmax(jnp.where(mask, s, NEG), axis=-1), 0.0)
    o = jnp.einsum('...grqk,...kgd->...qgrd', p.astype(v.dtype), v)
    return o, p


def pad_blocks(a):
    pad = (-a.shape[1]) % SEL_BLOCK
    return jnp.pad(a, ((0, 0), (0, pad)) + ((0, 0),) * (a.ndim - 2))


def compress(kv, pe, w_cmp):
    B, Tp, G, d = kv.shape
    blk = kv.reshape(B, Tp // CMP_BLOCK, CMP_BLOCK, G, d) + pe[:, None, :]
    return jnp.einsum('bnlgd,lde->bnge', blk, w_cmp)


def select_attn(q, k, v, q_pos, blk_idx, slopes):
    B, Tq, G, R, d = q.shape
    qb = SEL_QBLOCK if Tq % SEL_QBLOCK == 0 else Tq
    nq = Tq // qb
    n_sel = blk_idx.shape[-1]
    kt = k.transpose(0, 2, 1, 3)
    vt = v.transpose(0, 2, 1, 3)
    b_ix = jnp.arange(B)[:, None, None]
    g_ix = jnp.arange(G)[None, :, None]
    offs = jnp.arange(SEL_BLOCK)

    def one_block(args):
        qi, ii, pi = args
        tok = (ii[..., None] * SEL_BLOCK + offs).reshape(B, G, qb * n_sel * SEL_BLOCK)
        kg = kt[b_ix, g_ix, tok].reshape(B, G, qb, n_sel * SEL_BLOCK, d)
        vg = vt[b_ix, g_ix, tok].reshape(B, G, qb, n_sel * SEL_BLOCK, d)
        tok = tok.reshape(B, G, qb, n_sel * SEL_BLOCK)
        s = jnp.einsum('bqgrd,bgqkd->bgrqk', qi, kg).astype(jnp.float32) * (B_HD ** -0.5)
        dist = (pi[None, None, :, None] - tok)[:, :, None]
        s = s - slopes[None, :, :, None, None] * dist.astype(jnp.float32)
        mask = dist >= 0
        p = jnp.where(mask, jax.nn.softmax(jnp.where(mask, s, NEG), axis=-1), 0.0)
        return jnp.einsum('bgrqk,bgqkd->bqgrd', p.astype(v.dtype), vg)

    q_blocks = jnp.moveaxis(q.reshape(B, nq, qb, G, R, d), 1, 0)
    i_blocks = jnp.moveaxis(blk_idx.reshape(B, G, nq, qb, n_sel), 2, 0)
    p_blocks = q_pos.reshape(nq, qb)
    o = lax.map(one_block, (q_blocks, i_blocks, p_blocks))
    return jnp.moveaxis(o, 0, 1).reshape(B, Tq, G, R, d)


def nsa_cmp_slc(q, k_cmp, v_cmp, k_slc, v_slc, q_pos, nsa_pe, nsa_w_cmp, slopes):
    k_cmp, v_cmp, k_slc, v_slc = pad_blocks(k_cmp), pad_blocks(v_cmp), pad_blocks(k_slc), pad_blocks(v_slc)
    nb = k_cmp.shape[1] // CMP_BLOCK
    kc = compress(k_cmp, nsa_pe[0], nsa_w_cmp[0])
    vc = compress(v_cmp, nsa_pe[1], nsa_w_cmp[1])
    blk_end = (jnp.arange(nb) + 1) * CMP_BLOCK - 1
    o_cmp, p_cmp = dense_attn(q, kc, vc, q_pos, blk_end, slopes, None)
    imp = p_cmp.sum(2)
    j = jnp.arange(nb)[None, :]
    cur = (q_pos // SEL_BLOCK)[:, None]
    imp = jnp.where((j == cur) | (j == 0), float(B_REP + 1), imp)
    imp = jnp.where(j > cur, -1.0, imp)
    _, blk_idx = lax.top_k(imp, min(N_SEL, nb))
    o_slc = select_attn(q, k_slc, v_slc, q_pos, blk_idx, slopes)
    return o_cmp, o_slc


def window_prompt(q, k, v, slopes):
    B, T, G, R, d = q.shape
    nq = T // WIN_QBLOCK
    span = WINDOW + WIN_QBLOCK
    kp = jnp.pad(k, ((0, 0), (WINDOW, 0), (0, 0), (0, 0)))
    vp = jnp.pad(v, ((0, 0), (WINDOW, 0), (0, 0), (0, 0)))
    idx = jnp.arange(nq)[:, None] * WIN_QBLOCK + jnp.arange(span)[None, :]
    q_pos = jnp.arange(T).reshape(nq, WIN_QBLOCK)
    o, _ = dense_attn(q.reshape(B, nq, WIN_QBLOCK, G, R, d), kp[:, idx], vp[:, idx], q_pos, idx - WINDOW,
                      slopes, WINDOW)
    return o.reshape(B, T, G, R, d)


def token_mixers(x, q_pos, C0, n0, m0, chunk, past_cmp, past_slc, win_buf,
                 w_in, b_in, norm_a_g, nsa_pe, nsa_w_cmp):
    B, T, _ = x.shape
    a_q, a_k, a_v, a_i, a_f, a_o, b_q, b_cmp, b_slc, b_win, b_gate = in_projection(x, w_in, b_in)
    y_a, C, n, m = mlstm_branch(a_q, a_k, a_v, a_i, a_f, a_o, C0, n0, m0, chunk, norm_a_g)
    slopes = alibi_slopes()
    q = b_q.reshape(B, T, B_KV, B_REP, B_HD)
    kv_cmp = b_cmp.reshape(B, T, 2, B_KV, B_HD)
    kv_slc = b_slc.reshape(B, T, 2, B_KV, B_HD)
    kv_win = b_win.reshape(B, T, 2, B_KV, B_HD)
    if past_cmp is None:
        all_cmp, all_slc = kv_cmp, kv_slc
        o_win = window_prompt(q, kv_win[:, :, 0], kv_win[:, :, 1], slopes)
        new_win = kv_win[:, -min(WINDOW, T):]
    else:
        all_cmp = jnp.concatenate([past_cmp, kv_cmp], axis=1)
        all_slc = jnp.concatenate([past_slc, kv_slc], axis=1)
        buf = jnp.concatenate([win_buf, kv_win], axis=1)
        wb = win_buf.shape[1]
        k_pos = past_cmp.shape[1] - wb + jnp.arange(buf.shape[1])
        o_win, _ = dense_attn(q, buf[:, :, 0], buf[:, :, 1], q_pos, k_pos, slopes, WINDOW)
        new_win = buf[:, -wb:]
    o_cmp, o_slc = nsa_cmp_slc(q, all_cmp[:, :, 0], all_cmp[:, :, 1], all_slc[:, :, 0], all_slc[:, :, 1],
                               q_pos, nsa_pe, nsa_w_cmp, slopes)
    g = jax.nn.sigmoid(b_gate.astype(jnp.float32).reshape(B, T, 3, B_KV, B_REP))[..., None]
    y_b = (g[:, :, 0] * o_cmp + g[:, :, 1] * o_slc + g[:, :, 2] * o_win).reshape(B, T, B_HEADS * B_HD)
    return y_a, y_b.astype(x.dtype), kv_cmp, kv_slc, new_win, C, n, m


def peer_ffn(x, peer_wq, peer_keys, peer_u, peer_v):
    B, T, D = x.shape
    xf = x.reshape(B * T, D)
    n_tok = B * T
    xf = jnp.pad(xf, ((0, (-n_tok) % P_TBLOCK), (0, 0)))

    def one(xb):
        q = (xb @ peer_wq).reshape(P_TBLOCK, P_HEADS, 2, P_DHALF)
        s = jnp.einsum('tpcd,pckd->tpck', q, peer_keys).astype(jnp.float32)
        s1, i1 = lax.top_k(s[:, :, 0], P_TOPK)
        s2, i2 = lax.top_k(s[:, :, 1], P_TOPK)
        cand = (s1[..., :, None] + s2[..., None, :]).reshape(P_TBLOCK, P_HEADS, P_TOPK * P_TOPK)
        cidx = (i1[..., :, None] * P_NKEYS + i2[..., None, :]).reshape(P_TBLOCK, P_HEADS, P_TOPK * P_TOPK)
        top_s, j = lax.top_k(cand, P_TOPK)
        eidx = jnp.take_along_axis(cidx, j, axis=-1)
        gate = jax.nn.softmax(top_s, axis=-1)
        u = peer_u[eidx]
        v = peer_v[eidx]
        act = jax.nn.gelu(jnp.einsum('td,tpkd->tpk', xb, u).astype(jnp.float32), approximate=False)
        return jnp.einsum('tpk,tpkd->td', (gate * act).astype(v.dtype), v)

    out = lax.map(one, xf.reshape(-1, P_TBLOCK, D)).reshape(-1, D)[:n_tok]
    return out.reshape(B, T, D)


def layer_tail(x, y_a, y_b, w_br_a, w_br_b, w_merge, w_out, ln1_g, ln1_b,
               peer_wq, peer_keys, peer_u, peer_v, ln2_g, ln2_b):
    B, T, D = x.shape
    gates = jax.nn.sigmoid(jnp.einsum('btd,de->bte', x, w_merge)).reshape(B, T, 2, D)
    merged = gates[:, :, 0] * (y_a @ w_br_a) + gates[:, :, 1] * (y_b @ w_br_b)
    h = layer_norm(DN_ALPHA * x + merged @ w_out, ln1_g, ln1_b)
    return layer_norm(DN_ALPHA * h + peer_ffn(h, peer_wq, peer_keys, peer_u, peer_v), ln2_g, ln2_b)


def setup_inputs(seed: int = 0) -> dict:
    key = jax.random.key(seed)
    ks = jax.random.split(key, 28)
    nrm = jax.random.normal
    n_pages = PAST_LEN // PAGE_SIZE
    n_used = DEC_BATCH * n_pages
    n_pool = n_used + n_used // 4
    win_buf = min(WINDOW, PAST_LEN)
    f_start = int(np.cumsum(SPLIT_SIZES)[3])
    b_in = 0.02 * nrm(ks[10], (D_IN,))
    b_in = b_in.at[f_start:f_start + A_HEADS].add(jnp.linspace(3.0, 6.0, A_HEADS))
    return {
        'x_prompt': nrm(ks[0], (BATCH, SEQ, D_MODEL)),
        'x_sample': nrm(ks[1], (DEC_BATCH, DEC_SEQ, D_MODEL)),
        'cache_cmp_kv': nrm(ks[2], (n_pool, PAGE_SIZE, 2, B_KV, B_HD)),
        'cache_slc_kv': nrm(ks[3], (n_pool, PAGE_SIZE, 2, B_KV, B_HD)),
        'cache_win_kv': nrm(ks[4], (DEC_BATCH, win_buf, 2, B_KV, B_HD)),
        'state_C': 0.1 * nrm(ks[5], (DEC_BATCH, A_HEADS, A_DQK, A_DV)),
        'state_n': 0.1 * nrm(ks[6], (DEC_BATCH, A_HEADS, A_DQK)),
        'state_m': 0.5 * nrm(ks[7], (DEC_BATCH, A_HEADS)),
        'page_table': jax.random.permutation(ks[8], n_pool)[:n_used].reshape(DEC_BATCH, n_pages).astype(jnp.int32),
        'w_in': nrm(ks[9], (D_MODEL, D_IN)) * D_MODEL ** -0.5,
        'b_in': b_in,
        'norm_a_g': 1.0 + 0.02 * nrm(ks[11], (A_HEADS * A_DV,)),
        'nsa_pe': 0.02 * nrm(ks[12], (2, CMP_BLOCK, B_HD)),
        'nsa_w_cmp': nrm(ks[13], (2, CMP_BLOCK, B_HD, B_HD)) * (CMP_BLOCK * B_HD) ** -0.5,
        'w_br_a': nrm(ks[14], (A_HEADS * A_DV, D_MODEL)) * (A_HEADS * A_DV) ** -0.5 * DN_BETA,
        'w_br_b': nrm(ks[15], (B_HEADS * B_HD, D_MODEL)) * (B_HEADS * B_HD) ** -0.5 * DN_BETA,
        'w_merge': nrm(ks[16], (D_MODEL, 2 * D_MODEL)) * D_MODEL ** -0.5,
        'w_out': nrm(ks[17], (D_MODEL, D_MODEL)) * D_MODEL ** -0.5 * DN_BETA,
        'ln1_g': 1.0 + 0.02 * nrm(ks[18], (D_MODEL,)),
        'ln1_b': 0.02 * nrm(ks[19], (D_MODEL,)),
        'peer_wq': nrm(ks[20], (D_MODEL, P_HEADS * P_DKEY)) * D_MODEL ** -0.5,
        'peer_keys': nrm(ks[21], (P_HEADS, 2, P_NKEYS, P_DHALF)) * P_DHALF ** -0.5,
        'peer_u': nrm(ks[22], (P_EXPERTS, D_MODEL)) * D_MODEL ** -0.5,
        'peer_v': nrm(ks[23], (P_EXPERTS, D_MODEL)) * P_HEADS ** -0.5 * DN_BETA,
        'ln2_g': 1.0 + 0.02 * nrm(ks[24], (D_MODEL,)),
        'ln2_b': 0.02 * nrm(ks[25], (D_MODEL,)),
    }


def reference(x_prompt, x_sample, cache_cmp_kv, cache_slc_kv, cache_win_kv, state_C, state_n, state_m,
              page_table, w_in, b_in, norm_a_g, nsa_pe, nsa_w_cmp, w_br_a, w_br_b, w_merge, w_out,
              ln1_g, ln1_b, peer_wq, peer_keys, peer_u, peer_v, ln2_g, ln2_b):
    Bp, Tp, _ = x_prompt.shape
    dt = x_prompt.dtype
    C0 = jnp.zeros((Bp, A_HEADS, A_DQK, A_DV), jnp.float32)
    n0 = jnp.zeros((Bp, A_HEADS, A_DQK), jnp.float32)
    m0 = jnp.zeros((Bp, A_HEADS), jnp.float32)
    ya, yb, p_cmp, p_slc, p_win, p_C, p_n, p_m = token_mixers(
        x_prompt, jnp.arange(Tp), C0, n0, m0, A_CHUNK, None, None, None,
        w_in, b_in, norm_a_g, nsa_pe, nsa_w_cmp)
    y_prompt = layer_tail(x_prompt, ya, yb, w_br_a, w_br_b, w_merge, w_out, ln1_g, ln1_b,
                          peer_wq, peer_keys, peer_u, peer_v, ln2_g, ln2_b)
    Bs, Ts, _ = x_sample.shape
    past_len = page_table.shape[1] * PAGE_SIZE
    past_cmp = cache_cmp_kv[page_table].reshape(Bs, past_len, 2, B_KV, B_HD)
    past_slc = cache_slc_kv[page_table].reshape(Bs, past_len, 2, B_KV, B_HD)
    ya_s, yb_s, s_cmp, s_slc, s_win, s_C, s_n, s_m = token_mixers(
        x_sample, past_len + jnp.arange(Ts), state_C, state_n, state_m, Ts, past_cmp, past_slc, cache_win_kv,
        w_in, b_in, norm_a_g, nsa_pe, nsa_w_cmp)
    y_sample = layer_tail(x_sample, ya_s, yb_s, w_br_a, w_br_b, w_merge, w_out, ln1_g, ln1_b,
                          peer_wq, peer_keys, peer_u, peer_v, ln2_g, ln2_b)
    sd = state_C.dtype
    return (y_prompt, y_sample, p_cmp, p_slc, p_win, p_C.astype(dt), p_n.astype(dt), p_m.astype(dt),
            s_cmp, s_slc, s_win, s_C.astype(sd), s_n.astype(sd), s_m.astype(sd))
```

```python
import functools

import jax
import jax.numpy as jnp
import numpy as np
from jax import lax
from jax.experimental import pallas as pl
from jax.experimental.pallas import tpu as pltpu

D_MODEL = 1024
A_HEADS, A_DQK, A_DV = 4, 128, 256
B_HEADS, B_KV, B_HD = 16, 4, 64
B_REP = B_HEADS // B_KV
CMP_BLOCK = 64
N_SEL = 16
WINDOW = 512
PAGE_SIZE = 128
P_HEADS, P_NKEYS, P_DHALF, P_TOPK = 8, 128, 128, 16
P_EXPERTS = P_NKEYS * P_NKEYS
DN_ALPHA = 2.0 ** 0.25
LN_EPS = 1e-5
NEG = -1e30

LANES = 128
KV_COLS = 2 * B_KV * B_HD
VMEM_LIMIT = 56 * 1024 * 1024

C_AQ, C_AK, C_AV, C_AO, C_BQ, C_CMP, C_SLC, C_WIN, C_SMALL, C_END = (
    0, 512, 1024, 2048, 3072, 4096, 4608, 5120, 5632, 5760)
G_I, G_F, G_GATE = 0, A_HEADS, 2 * A_HEADS

bf16 = jnp.bfloat16
f32 = jnp.float32


def _cparams(*sem):
    return pltpu.CompilerParams(dimension_semantics=sem, vmem_limit_bytes=VMEM_LIMIT)


def _full(shape):
    nd = len(shape)
    return pl.BlockSpec(shape, lambda *_: (0,) * nd)


def _proj_kernel(x_ref, w_ref, b_ref, o_ref):
    o_ref[...] = jnp.dot(x_ref[...], w_ref[...], preferred_element_type=f32) + b_ref[...]


def _proj(xb, w, b, tm):
    n, k = xb.shape
    e = w.shape[1]
    return pl.pallas_call(
        _proj_kernel,
        grid=(n // tm,),
        in_specs=[pl.BlockSpec((tm, k), lambda i: (i, 0)), _full((k, e)), _full((1, e))],
        out_specs=pl.BlockSpec((tm, e), lambda i: (i, 0)),
        out_shape=jax.ShapeDtypeStruct((n, e), f32),
        compiler_params=_cparams("parallel"),
        name="proj",
    )(xb, w, b)


def _proj_t_kernel(wt_ref, x_ref, b_ref, o_ref):
    o_ref[...] = lax.dot_general(wt_ref[...], x_ref[...], (((1,), (1,)), ((), ())),
                                 preferred_element_type=f32) + b_ref[...]


def _proj_t(wt, xb, bcol, tn):
    e, k = wt.shape
    n = xb.shape[0]
    return pl.pallas_call(
        _proj_t_kernel,
        grid=(n // tn,),
        in_specs=[_full((e, k)), pl.BlockSpec((tn, k), lambda i: (i, 0)), _full((e, 1))],
        out_specs=pl.BlockSpec((e, tn), lambda i: (0, i)),
        out_shape=jax.ShapeDtypeStruct((e, n), f32),
        compiler_params=_cparams("parallel"),
        name="proj_t",
    )(wt, xb, bcol)


def _mlstm_kernel(q_ref, k_ref, v_ref, ao_ref, g_ref, gt_ref, ng_ref, c0_ref, n0_ref, m0_ref,
                  y_ref, c_out, n_out, m_out, c_s, n_s, m_s, *, L, valid):
    h = pl.program_id(1)
    c = pl.program_id(2)

    @pl.when(c == 0)
    def _():
        c_s[...] = c0_ref[0, 0]
        n_s[...] = n0_ref[0, 0]
        m_s[...] = m0_ref[0, 0]

    q = q_ref[...]
    k = k_ref[...] * (A_DQK ** -0.5)
    v = v_ref[...]
    g = g_ref[...]
    gt = gt_ref[0]
    lane = lax.broadcasted_iota(jnp.int32, g.shape, 1)
    sub = lax.broadcasted_iota(jnp.int32, gt.shape, 0)
    i_col = jnp.sum(jnp.where(lane == G_I + h, g, 0.0), axis=1, keepdims=True)
    f_col = jnp.sum(jnp.where(lane == G_F + h, g, 0.0), axis=1, keepdims=True)
    i_row = jnp.sum(jnp.where(sub == G_I + h, gt, 0.0), axis=0, keepdims=True)
    f_row = jnp.sum(jnp.where(sub == G_F + h, gt, 0.0), axis=0, keepdims=True)
    lf_col = jax.nn.log_sigmoid(f_col)
    lf_row = jax.nn.log_sigmoid(f_row)
    t_col = lax.broadcasted_iota(jnp.int32, (L, 1), 0)
    s_row = lax.broadcasted_iota(jnp.int32, (1, L), 1)
    if valid < L:
        lf_col = jnp.where(t_col < valid, lf_col, 0.0)
        lf_row = jnp.where(s_row < valid, lf_row, 0.0)
        i_col = jnp.where(t_col < valid, i_col, NEG)
        i_row = jnp.where(s_row < valid, i_row, NEG)
    tt = lax.broadcasted_iota(jnp.int32, (L, L), 0)
    ss = lax.broadcasted_iota(jnp.int32, (L, L), 1)
    causal = ss <= tt
    b_col = jnp.sum(jnp.where(causal, lf_row, 0.0), axis=1, keepdims=True)
    b_row = jnp.sum(jnp.where(tt <= ss, lf_col, 0.0), axis=0, keepdims=True)
    m_prev = m_s[...]
    cmat = c_s[...]
    n_row = n_s[...]

    d_log = jnp.where(causal, b_col - b_row + i_row, NEG)
    inter = b_col + m_prev
    m_t = jnp.maximum(inter, jnp.max(d_log, axis=1, keepdims=True))
    qb = q.astype(bf16)
    qk = lax.dot_general(qb, k.astype(bf16), (((1,), (1,)), ((), ())), preferred_element_type=f32)
    smat = qk * jnp.exp(d_log - m_t)
    w_inter = jnp.exp(inter - m_t)
    vb = v.astype(bf16)
    num = (w_inter * jnp.dot(qb, cmat.astype(bf16), preferred_element_type=f32)
           + jnp.dot(smat.astype(bf16), vb, preferred_element_type=f32))
    den = w_inter * jnp.sum(q * n_row, axis=1, keepdims=True) + jnp.sum(smat, axis=1, keepdims=True)
    hid = num / jnp.maximum(jnp.abs(den), jnp.exp(-m_t))
    mu = jnp.mean(hid, axis=1, keepdims=True)
    var = jnp.mean(jnp.square(hid - mu), axis=1, keepdims=True)
    hid = (hid - mu) * lax.rsqrt(var + LN_EPS) * ng_ref[...]
    y_ref[...] = hid * jax.nn.sigmoid(ao_ref[...])

    b_end = b_col[L - 1:L, :]
    g_row = b_end - b_row + i_row
    m_new = jnp.maximum(b_end + m_prev, jnp.max(g_row, axis=1, keepdims=True))
    a = jnp.exp(b_end + m_prev - m_new)
    w_col = jnp.exp(b_end - b_col + i_col - m_new)
    kw = k * w_col
    c_new = a * cmat + lax.dot_general(kw.astype(bf16), vb, (((0,), (0,)), ((), ())),
                                       preferred_element_type=f32)
    n_new = a * n_row + jnp.sum(kw, axis=0, keepdims=True)
    c_s[...] = c_new
    n_s[...] = n_new
    m_s[...] = m_new

    @pl.when(c == pl.num_programs(2) - 1)
    def _():
        c_out[0, 0] = c_new
        n_out[0, 0] = n_new
        m_out[0, 0] = m_new


def _mlstm(proj, gt, norm_g, c0, n0, m0, *, row0, nb, t, L, valid):
    nc = t // L
    rb0 = row0 // L
    gt = gt[:8, row0:row0 + nb * t].reshape(8, nb * nc, L).transpose(1, 0, 2)
    rows = lambda b, h, c: rb0 + b * nc + c
    st = lambda b, h, c: (b, h, 0, 0)
    y, c_f, n_f, m_f = pl.pallas_call(
        functools.partial(_mlstm_kernel, L=L, valid=valid),
        grid=(nb, A_HEADS, nc),
        in_specs=[
            pl.BlockSpec((L, A_DQK), lambda b, h, c: (rows(b, h, c), C_AQ // A_DQK + h)),
            pl.BlockSpec((L, A_DQK), lambda b, h, c: (rows(b, h, c), C_AK // A_DQK + h)),
            pl.BlockSpec((L, A_DV), lambda b, h, c: (rows(b, h, c), C_AV // A_DV + h)),
            pl.BlockSpec((L, A_DV), lambda b, h, c: (rows(b, h, c), C_AO // A_DV + h)),
            pl.BlockSpec((L, LANES), lambda b, h, c: (rows(b, h, c), C_SMALL // LANES)),
            pl.BlockSpec((1, 8, L), lambda b, h, c: (b * nc + c, 0, 0)),
            pl.BlockSpec((1, A_DV), lambda b, h, c: (0, h)),
            pl.BlockSpec((1, 1, A_DQK, A_DV), st),
            pl.BlockSpec((1, 1, 1, A_DQK), st),
            pl.BlockSpec((1, 1, 1, 1), st),
        ],
        out_specs=[
            pl.BlockSpec((L, A_DV), lambda b, h, c: (b * nc + c, h)),
            pl.BlockSpec((1, 1, A_DQK, A_DV), st),
            pl.BlockSpec((1, 1, 1, A_DQK), st),
            pl.BlockSpec((1, 1, 1, 1), st),
        ],
        out_shape=[
            jax.ShapeDtypeStruct((nb * t, A_HEADS * A_DV), f32),
            jax.ShapeDtypeStruct((nb, A_HEADS, A_DQK, A_DV), f32),
            jax.ShapeDtypeStruct((nb, A_HEADS, 1, A_DQK), f32),
            jax.ShapeDtypeStruct((nb, A_HEADS, 1, 1), f32),
        ],
        scratch_shapes=[pltpu.VMEM((A_DQK, A_DV), f32), pltpu.VMEM((1, A_DQK), f32), pltpu.VMEM((1, 1), f32)],
        compiler_params=_cparams("parallel", "parallel", "arbitrary"),
        name="mlstm",
    )(proj, proj, proj, proj, proj, gt, norm_g, c0, n0, m0)
    return y, c_f, n_f[:, :, 0], m_f[:, :, 0, 0]


def _compress_kernel(x0_ref, x1_ref, x2_ref, x3_ref, pe_ref, w_ref, o_ref, xf_ref, *, nblk):
    for p, x_ref in enumerate((x0_ref, x1_ref, x2_ref, x3_ref)):
        for l in range(CMP_BLOCK):
            xf_ref[p, :, l * LANES:(l + 1) * LANES] = x_ref[pl.ds(l, nblk, stride=CMP_BLOCK), :]
    for p in range(KV_COLS // LANES):
        c = p // 2
        xf = (xf_ref[p] + pe_ref[c]).astype(bf16)
        o_ref[0, :, p * LANES:(p + 1) * LANES] = jnp.dot(xf, w_ref[c], preferred_element_type=f32)


def _compress(x2, pe2, w2, *, nb, tk, row0, colblk):
    nblk = tk // CMP_BLOCK
    kflat = CMP_BLOCK * LANES
    rb0 = row0 // tk
    ngrp = KV_COLS // LANES
    return pl.pallas_call(
        functools.partial(_compress_kernel, nblk=nblk),
        grid=(nb,),
        in_specs=[pl.BlockSpec((tk, LANES), functools.partial(lambda b, p: (rb0 + b, colblk * ngrp + p), p=p))
                  for p in range(ngrp)] + [_full((2, 1, kflat)), _full((2, kflat, LANES))],
        out_specs=pl.BlockSpec((1, nblk, KV_COLS), lambda b: (b, 0, 0)),
        out_shape=jax.ShapeDtypeStruct((nb, nblk, KV_COLS), f32),
        scratch_shapes=[pltpu.VMEM((KV_COLS // LANES, nblk, kflat), f32)],
        compiler_params=_cparams("parallel"),
        name="compress",
    )(x2, x2, x2, x2, pe2, w2)


def _gate_col(g, lane_idx):
    lane = lax.broadcasted_iota(jnp.int32, g.shape, 1)
    return jax.nn.sigmoid(jnp.sum(jnp.where(lane == lane_idx, g, 0.0), axis=1, keepdims=True))


def _cmp_topk_kernel(slope_ref, q_ref, kc_ref, vc_ref, g_ref, o_ref, sel_ref, *, tq, nblk, qpos0):
    gi = pl.program_id(1)
    i = pl.program_id(2)
    qpos = qpos0 + i * tq + lax.broadcasted_iota(jnp.int32, (tq, 1), 0)
    j = lax.broadcasted_iota(jnp.int32, (1, nblk), 1)
    dist = qpos - ((j + 1) * CMP_BLOCK - 1)
    valid = dist >= 0
    distf = dist.astype(f32)
    kc = kc_ref[0, 0].astype(bf16)
    vc = vc_ref[0, 0].astype(bf16)
    g = g_ref[...]
    imp = jnp.zeros((tq, nblk), f32)
    for r in range(B_REP):
        qr = q_ref[0, 0, 0, r * tq:(r + 1) * tq, :]
        s = lax.dot_general(qr, kc, (((1,), (1,)), ((), ())), preferred_element_type=f32) * (B_HD ** -0.5)
        s = s - slope_ref[gi * B_REP + r] * distf
        s = jnp.where(valid, s, NEG)
        e = jnp.exp(s - jnp.max(s, axis=1, keepdims=True))
        p = jnp.where(valid, e / jnp.sum(e, axis=1, keepdims=True), 0.0)
        imp = imp + p
        o = jnp.dot(p.astype(bf16), vc, preferred_element_type=f32)
        o_ref[0, 0, 0, r * tq:(r + 1) * tq, :] = o * _gate_col(g, G_GATE + gi * B_REP + r)
    cur = qpos // CMP_BLOCK
    imp = jnp.where((j == cur) | (j == 0), float(B_REP + 1), imp)
    imp = jnp.where(j > cur, -1.0, imp)
    jf = j.astype(f32)
    sel = jnp.zeros((tq, nblk), f32)
    for _ in range(N_SEL):
        mx = jnp.max(imp, axis=1, keepdims=True)
        idx = jnp.min(jnp.where(imp == mx, jf, float(nblk)), axis=1, keepdims=True)
        hit = jf == idx
        sel = jnp.where(hit, 1.0, sel)
        imp = jnp.where(hit, NEG, imp)
    sel_ref[0, 0] = sel


def _cmp_topk(slopes, qt, kc, vc, gsmall, *, tq, qpos0, row0):
    nb, _, nqt, _, _ = qt.shape
    nblk = kc.shape[2]
    rb0 = row0 // tq
    return pl.pallas_call(
        functools.partial(_cmp_topk_kernel, tq=tq, nblk=nblk, qpos0=qpos0),
        grid=(nb, B_KV, nqt),
        in_specs=[
            pl.BlockSpec(memory_space=pltpu.SMEM),
            pl.BlockSpec((1, 1, 1, B_REP * tq, B_HD), lambda b, g, i: (b, g, i, 0, 0)),
            pl.BlockSpec((1, 1, nblk, B_HD), lambda b, g, i: (b, g, 0, 0)),
            pl.BlockSpec((1, 1, nblk, B_HD), lambda b, g, i: (b, g, 0, 0)),
            pl.BlockSpec((tq, LANES), lambda b, g, i: (rb0 + b * nqt + i, 0)),
        ],
        out_specs=[
            pl.BlockSpec((1, 1, 1, B_REP * tq, B_HD), lambda b, g, i: (b, g, i, 0, 0)),
            pl.BlockSpec((1, 1, tq, nblk), lambda b, g, i: (b, g, i, 0)),
        ],
        out_shape=[
            jax.ShapeDtypeStruct(qt.shape, f32),
            jax.ShapeDtypeStruct((nb, B_KV, nqt * tq, nblk), f32),
        ],
        compiler_params=_cparams("parallel", "parallel", "parallel"),
        name="cmp_topk",
    )(slopes, qt, kc, vc, gsmall)


def _attn_kernel(slope_ref, q_ref, k_ref, v_ref, g_ref, *rest, tq, tk, nkt, nblk, qpos0, kpos0, window,
                 gate_lane):
    if window is None:
        sel_ref, o_ref, m_s, l_s, acc_s = rest
    else:
        o_ref, m_s, l_s, acc_s = rest
    gi = pl.program_id(1)
    i = pl.program_id(2)
    qlo = qpos0 + i * tq
    qpos = qlo + lax.broadcasted_iota(jnp.int32, (tq, 1), 0)
    m_s[...] = jnp.full(m_s.shape, NEG, f32)
    l_s[...] = jnp.zeros(l_s.shape, f32)
    acc_s[...] = jnp.zeros(acc_s.shape, f32)
    kt_hi = jnp.minimum((qlo + tq - 1 - kpos0) // tk + 1, nkt)
    if window is None:
        kt_lo = 0
        selb = sel_ref[0, 0].astype(bf16)
    else:
        kt_lo = jnp.maximum(qlo - (window - 1) - kpos0, 0) // tk

    def body(kt, carry):
        k0 = pl.multiple_of(kt * tk, tk)
        kb = k_ref[0, 0, pl.ds(k0, tk), :]
        vb = v_ref[0, 0, pl.ds(k0, tk), :]
        kidx = k0 + lax.broadcasted_iota(jnp.int32, (1, tk), 1)
        dist = qpos - (kpos0 + kidx)
        mask = dist >= 0
        if window is None:
            blk = lax.broadcasted_iota(jnp.int32, (nblk, tk), 0)
            kblk = (k0 + lax.broadcasted_iota(jnp.int32, (nblk, tk), 1)) // CMP_BLOCK
            expand = jnp.where(blk == kblk, 1.0, 0.0).astype(bf16)
            mask = mask & (jnp.dot(selb, expand, preferred_element_type=f32) > 0.5)
        else:
            mask = mask & (dist < window)
        distf = dist.astype(f32)
        for r in range(B_REP):
            rows = slice(r * tq, (r + 1) * tq)
            s = lax.dot_general(q_ref[0, 0, 0, rows, :], kb, (((1,), (1,)), ((), ())),
                                preferred_element_type=f32) * (B_HD ** -0.5)
            s = jnp.where(mask, s - slope_ref[gi * B_REP + r] * distf, NEG)
            m_old = m_s[rows, :]
            m_new = jnp.maximum(m_old, jnp.max(s, axis=1, keepdims=True))
            alpha = jnp.exp(m_old - m_new)
            p = jnp.exp(s - m_new)
            l_s[rows, :] = alpha * l_s[rows, :] + jnp.sum(p, axis=1, keepdims=True)
            acc_s[rows, :] = alpha * acc_s[rows, :] + jnp.dot(p.astype(bf16), vb, preferred_element_type=f32)
            m_s[rows, :] = m_new
        return carry

    lax.fori_loop(kt_lo, kt_hi, body, 0)
    g = g_ref[...]
    for r in range(B_REP):
        rows = slice(r * tq, (r + 1) * tq)
        o_ref[0, 0, 0, rows, :] = acc_s[rows, :] / l_s[rows, :] * _gate_col(g, gate_lane + gi * B_REP + r)


def _attn(slopes, qt, kh, vh, gsmall, sel, *, tq, tk, qpos0, kpos0, window, gate_lane, row0):
    nb, _, nqt, _, _ = qt.shape
    tkk = kh.shape[2]
    nkt = tkk // tk
    nblk = None if sel is None else sel.shape[3]
    rb0 = row0 // tq
    in_specs = [
        pl.BlockSpec(memory_space=pltpu.SMEM),
        pl.BlockSpec((1, 1, 1, B_REP * tq, B_HD), lambda b, g, i: (b, g, i, 0, 0)),
        pl.BlockSpec((1, 1, tkk, B_HD), lambda b, g, i: (b, g, 0, 0)),
        pl.BlockSpec((1, 1, tkk, B_HD), lambda b, g, i: (b, g, 0, 0)),
        pl.BlockSpec((tq, LANES), lambda b, g, i: (rb0 + b * nqt + i, 0)),
    ]
    args = [slopes, qt, kh, vh, gsmall]
    if sel is not None:
        in_specs.append(pl.BlockSpec((1, 1, tq, nblk), lambda b, g, i: (b, g, i, 0)))
        args.append(sel)
    return pl.pallas_call(
        functools.partial(_attn_kernel, tq=tq, tk=tk, nkt=nkt, nblk=nblk, qpos0=qpos0, kpos0=kpos0,
                          window=window, gate_lane=gate_lane),
        grid=(nb, B_KV, nqt),
        in_specs=in_specs,
        out_specs=pl.BlockSpec((1, 1, 1, B_REP * tq, B_HD), lambda b, g, i: (b, g, i, 0, 0)),
        out_shape=jax.ShapeDtypeStruct(qt.shape, f32),
        scratch_shapes=[pltpu.VMEM((B_REP * tq, 1), f32), pltpu.VMEM((B_REP * tq, 1), f32),
                        pltpu.VMEM((B_REP * tq, B_HD), f32)],
        compiler_params=_cparams("parallel", "parallel", "parallel"),
        name="attn_sel" if window is None else "attn_win",
    )(*args)


PAGES_PER_STEP = 4


def _gather_kernel(pt_ref, *refs):
    del pt_ref
    pages, tail_ref, o_ref = refs[:PAGES_PER_STEP], refs[PAGES_PER_STEP], refs[PAGES_PER_STEP + 1]
    j = pl.program_id(1)
    last = pl.num_programs(1) - 1

    @pl.when(j < last)
    def _():
        for u in range(PAGES_PER_STEP):
            o_ref[0, u] = pages[u][0]

    @pl.when(j == last)
    def _():
        o_ref[0, 0] = tail_ref[0]
        for u in range(1, PAGES_PER_STEP):
            o_ref[0, u] = jnp.zeros(o_ref.shape[2:], f32)


def _gather_pages(page_table, cache, tail):
    nb, n_pages = page_table.shape
    steps = n_pages // PAGES_PER_STEP
    page = (1, PAGE_SIZE, KV_COLS)

    def page_map(u):
        return lambda b, j, pt: (pt[b, jnp.minimum(j, steps - 1) * PAGES_PER_STEP + u], 0, 0)

    return pl.pallas_call(
        _gather_kernel,
        grid_spec=pltpu.PrefetchScalarGridSpec(
            num_scalar_prefetch=1,
            grid=(nb, steps + 1),
            in_specs=[pl.BlockSpec(page, page_map(u)) for u in range(PAGES_PER_STEP)]
            + [pl.BlockSpec(page, lambda b, j, pt: (b, 0, 0))],
            out_specs=pl.BlockSpec((1, PAGES_PER_STEP, PAGE_SIZE, KV_COLS), lambda b, j, pt: (b, j, 0, 0)),
        ),
        out_shape=jax.ShapeDtypeStruct((nb, n_pages + PAGES_PER_STEP, PAGE_SIZE, KV_COLS), f32),
        compiler_params=_cparams("parallel", "arbitrary"),
        name="gather_pages",
    )(page_table, *([cache] * PAGES_PER_STEP), tail)


def _layer_norm(z, g, b):
    mu = jnp.mean(z, axis=1, keepdims=True)
    var = jnp.mean(jnp.square(z - mu), axis=1, keepdims=True)
    return (z - mu) * lax.rsqrt(var + LN_EPS) * g + b


def _tail_kernel(x_ref, ya_ref, oc_ref, os_ref, ow_ref, wm_ref, wa_ref, wb_ref, wo_ref, g_ref, b_ref,
                 h_ref, hb_ref):
    x = x_ref[...]
    gates = jax.nn.sigmoid(jnp.dot(x.astype(bf16), wm_ref[...], preferred_element_type=f32))
    yb = oc_ref[...] + os_ref[...] + ow_ref[...]
    ma = jnp.dot(ya_ref[...].astype(bf16), wa_ref[...], preferred_element_type=f32)
    mb = jnp.dot(yb.astype(bf16), wb_ref[...], preferred_element_type=f32)
    merged = gates[:, :D_MODEL] * ma + gates[:, D_MODEL:] * mb
    z = DN_ALPHA * x + jnp.dot(merged.astype(bf16), wo_ref[...], preferred_element_type=f32)
    h = _layer_norm(z, g_ref[...], b_ref[...])
    h_ref[...] = h
    hb_ref[...] = h.astype(bf16)


def _tail(x, ya, oc, os_, ow, wm, wa, wb, wo, g, b, tm):
    n = x.shape[0]
    row = pl.BlockSpec((tm, D_MODEL), lambda i: (i, 0))
    return pl.pallas_call(
        _tail_kernel,
        grid=(n // tm,),
        in_specs=[row] * 5 + [_full(wm.shape), _full(wa.shape), _full(wb.shape), _full(wo.shape),
                              _full(g.shape), _full(b.shape)],
        out_specs=[row, row],
        out_shape=[jax.ShapeDtypeStruct((n, D_MODEL), f32), jax.ShapeDtypeStruct((n, D_MODEL), bf16)],
        compiler_params=_cparams("parallel"),
        name="tail",
    )(x, ya, oc, os_, ow, wm, wa, wb, wo, g, b)


def _top16(x, lanes_out):
    rows, width = x.shape
    jf = lax.broadcasted_iota(jnp.int32, (1, width), 1).astype(f32)
    jo = lax.broadcasted_iota(jnp.int32, (1, lanes_out), 1)
    rank = jnp.full((rows, width), float(P_TOPK), f32)
    vals = jnp.zeros((rows, lanes_out), f32)
    for k in range(P_TOPK):
        mx = jnp.max(x, axis=1, keepdims=True)
        idx = jnp.min(jnp.where(x == mx, jf, float(width)), axis=1, keepdims=True)
        hit = jf == idx
        rank = jnp.where(hit, float(k), rank)
        vals = jnp.where(jo == k, mx, vals)
        x = jnp.where(hit, NEG, x)
    return rank, vals


def _route_kernel(h_ref, wq_ref, keys_ref, tab_ref):
    qp = jnp.dot(h_ref[...], wq_ref[...], preferred_element_type=f32)
    tb = qp.shape[0]
    lane = lax.broadcasted_iota(jnp.int32, (1, LANES), 1)
    grid_lane = lax.broadcasted_iota(jnp.int32, (1, P_TOPK * P_TOPK), 1)
    for p in range(P_HEADS):
        sc, rk, vl = [], [], []
        for c in range(2):
            qs = qp[:, (2 * p + c) * P_DHALF:(2 * p + c + 1) * P_DHALF].astype(bf16)
            s = lax.dot_general(qs, keys_ref[p, c], (((1,), (1,)), ((), ())), preferred_element_type=f32)
            r, v = _top16(s, LANES)
            sc.append(s)
            rk.append(r)
            vl.append(v)
        v0k = [jnp.sum(jnp.where(lane == k, vl[0], 0.0), axis=1, keepdims=True) for k in range(P_TOPK)]
        v1k = [jnp.sum(jnp.where(lane == k, vl[1], 0.0), axis=1, keepdims=True) for k in range(P_TOPK)]
        cand = jnp.zeros((tb, P_TOPK * P_TOPK), f32)
        for k in range(P_TOPK):
            cand = cand + jnp.where(grid_lane // P_TOPK == k, v0k[k], 0.0) + jnp.where(grid_lane % P_TOPK == k, v1k[k], 0.0)
        crank, cvals = _top16(cand, LANES)
        taken = crank < float(P_TOPK)
        top0 = jnp.sum(jnp.where(lane == 0, cvals, 0.0), axis=1, keepdims=True)
        z = jnp.sum(jnp.where(lane < P_TOPK, jnp.exp(cvals - top0), 0.0), axis=1, keepdims=True)
        n_a = jnp.zeros((tb, LANES), f32)
        for k1 in range(P_TOPK):
            cnt = jnp.sum(jnp.where(taken & (grid_lane // P_TOPK == k1), 1.0, 0.0), axis=1, keepdims=True)
            n_a = jnp.where(rk[0] == float(k1), cnt, n_a)
        e0 = jnp.exp(sc[0] - v0k[0])
        e1n = jnp.exp(sc[1] - v1k[0]) / z
        tab_ref[p, 0] = n_a.T
        tab_ref[p, 1] = e0.T
        tab_ref[p, 2] = rk[1].T
        tab_ref[p, 3] = e1n.T


def _route(hb, wq, keys, tb):
    n = hb.shape[0]
    return pl.pallas_call(
        _route_kernel,
        grid=(n // tb,),
        in_specs=[pl.BlockSpec((tb, D_MODEL), lambda i: (i, 0)), _full(wq.shape), _full(keys.shape)],
        out_specs=pl.BlockSpec((P_HEADS, 4, P_NKEYS, tb), lambda i: (0, 0, 0, i)),
        out_shape=jax.ShapeDtypeStruct((P_HEADS, 4, P_NKEYS, n), f32),
        compiler_params=_cparams("parallel"),
        name="peer_route",
    )(hb, wq, keys)


def _experts_kernel(hb_ref, h_ref, tab_ref, u_ref, vt_ref, g_ref, b_ref, y_ref, acc_s, pt_s, *, te):
    j = pl.program_id(1)

    @pl.when(j == 0)
    def _():
        acc_s[...] = jnp.zeros(acc_s.shape, f32)

    ht = lax.dot_general(u_ref[...], hb_ref[...], (((1,), (1,)), ((), ())), preferred_element_type=f32)
    for aa in range(te // P_NKEYS):
        a = j * (te // P_NKEYS) + aa
        w = jnp.zeros((P_NKEYS, ht.shape[1]), f32)
        for p in range(P_HEADS):
            n_row = tab_ref[p, 0, pl.ds(a, 1), :]
            e0_row = tab_ref[p, 1, pl.ds(a, 1), :]
            w = w + jnp.where(tab_ref[p, 2] < n_row, e0_row * tab_ref[p, 3], 0.0)
        hs = ht[aa * P_NKEYS:(aa + 1) * P_NKEYS, :]
        act = 0.5 * hs * (1.0 + lax.erf(hs * (0.5 ** 0.5)))
        pt_s[aa * P_NKEYS:(aa + 1) * P_NKEYS, :] = (w * act).astype(bf16)
    acc_s[...] += jnp.dot(vt_ref[...], pt_s[...], preferred_element_type=f32)

    @pl.when(j == pl.num_programs(1) - 1)
    def _():
        z = DN_ALPHA * h_ref[...] + acc_s[...].T
        y_ref[...] = _layer_norm(z, g_ref[...], b_ref[...])


def _experts(hb, h, tab, u, vt, g, b, tb, te):
    n = hb.shape[0]
    row = pl.BlockSpec((tb, D_MODEL), lambda i, j: (i, 0))
    return pl.pallas_call(
        functools.partial(_experts_kernel, te=te),
        grid=(n // tb, P_EXPERTS // te),
        in_specs=[row, row,
                  pl.BlockSpec((P_HEADS, 4, P_NKEYS, tb), lambda i, j: (0, 0, 0, i)),
                  pl.BlockSpec((te, D_MODEL), lambda i, j: (j, 0)),
                  pl.BlockSpec((D_MODEL, te), lambda i, j: (0, j)),
                  pl.BlockSpec((1, D_MODEL), lambda i, j: (0, 0)),
                  pl.BlockSpec((1, D_MODEL), lambda i, j: (0, 0))],
        out_specs=row,
        out_shape=jax.ShapeDtypeStruct((n, D_MODEL), f32),
        scratch_shapes=[pltpu.VMEM((D_MODEL, tb), f32), pltpu.VMEM((te, tb), bf16)],
        compiler_params=_cparams("parallel", "arbitrary"),
        name="peer_experts",
    )(hb, h, tab, u, vt, g, b)


def _to_q_tiles(q2, nb, t, tq):
    q = q2.reshape(nb, t // tq, tq, B_KV, B_REP, B_HD).transpose(0, 3, 1, 4, 2, 5)
    return q.reshape(nb, B_KV, t // tq, B_REP * tq, B_HD).astype(bf16)


def _from_q_tiles(o, nb, t, tq):
    o = o.reshape(nb, B_KV, t // tq, B_REP, tq, B_HD).transpose(0, 2, 4, 1, 3, 5)
    return o.reshape(nb, t, B_HEADS * B_HD)


def _kv_heads(kv3):
    nb, tk, _ = kv3.shape
    kv = kv3.reshape(nb, tk, 2, B_KV, B_HD).transpose(2, 0, 3, 1, 4)
    return kv[0], kv[1]


def _nsa(slopes, proj, gsmall, pe2, w2, *, nb, t, tq, row0, qpos0, cmp_src, slc3, win3, win_kpos0, tk_sel, tk_win):
    qt = _to_q_tiles(proj[row0:row0 + nb * t, C_BQ:C_CMP], nb, t, tq)
    x2, cmp_tk, cmp_row0, cmp_colblk = cmp_src
    kvc = _compress(x2, pe2, w2, nb=nb, tk=cmp_tk, row0=cmp_row0, colblk=cmp_colblk)
    kc, vc = _kv_heads(kvc)
    o_cmp, sel = _cmp_topk(slopes, qt, kc, vc, gsmall, tq=tq, qpos0=qpos0, row0=row0)
    ks, vs = _kv_heads(slc3)
    o_slc = _attn(slopes, qt, ks.astype(bf16), vs.astype(bf16), gsmall, sel, tq=tq, tk=tk_sel, qpos0=qpos0,
                  kpos0=0, window=None, gate_lane=G_GATE + B_HEADS, row0=row0)
    kw, vw = _kv_heads(win3)
    o_win = _attn(slopes, qt, kw.astype(bf16), vw.astype(bf16), gsmall, None, tq=tq, tk=tk_win, qpos0=qpos0,
                  kpos0=win_kpos0, window=WINDOW, gate_lane=G_GATE + 2 * B_HEADS, row0=row0)
    return tuple(_from_q_tiles(o, nb, t, tq).reshape(nb * t, B_HEADS * B_HD) for o in (o_cmp, o_slc, o_win))


def kernel(x_prompt, x_sample, cache_cmp_kv, cache_slc_kv, cache_win_kv, state_C, state_n, state_m, page_table,
           w_in, b_in, norm_a_g, nsa_pe, nsa_w_cmp, w_br_a, w_br_b, w_merge, w_out, ln1_g, ln1_b,
           peer_wq, peer_keys, peer_u, peer_v, ln2_g, ln2_b):
    bp, tp, _ = x_prompt.shape
    bs, ts, _ = x_sample.shape
    tsp = 8
    n_p, n_s = bp * tp, bs * tsp
    past = page_table.shape[1] * PAGE_SIZE

    perm = np.concatenate([np.arange(0, 2048), np.arange(2056, 5640), np.arange(2048, 2056), np.arange(5640, 5688)])
    w_perm = jnp.pad(w_in[:, perm], ((0, 0), (0, C_END - perm.size)))
    b_perm = jnp.pad(b_in[perm], (0, C_END - perm.size))
    w_perm_b = w_perm.astype(bf16)
    slopes = jnp.asarray(2.0 ** (-8.0 * np.arange(1, B_HEADS + 1) / B_HEADS), f32)
    wc = nsa_w_cmp.reshape(2, CMP_BLOCK, 1, B_HD, 1, B_HD)
    eye2 = jnp.eye(2, dtype=f32).reshape(1, 1, 2, 1, 2, 1)
    w2 = (wc * eye2).reshape(2, CMP_BLOCK * LANES, LANES).astype(bf16)
    pe2 = jnp.tile(nsa_pe, (1, 1, 2)).reshape(2, 1, CMP_BLOCK * LANES)

    xs_pad = jnp.pad(x_sample, ((0, 0), (0, tsp - ts), (0, 0)))
    x_all = jnp.concatenate([x_prompt.reshape(n_p, D_MODEL), xs_pad.reshape(n_s, D_MODEL)], axis=0)
    xb = x_all.astype(bf16)
    proj = _proj(xb, w_perm_b, b_perm.reshape(1, C_END), 256)
    gt = _proj_t(w_perm_b[:, C_SMALL:].T, xb, b_perm[C_SMALL:].reshape(LANES, 1), 512)

    zc = jnp.zeros((bp, A_HEADS, A_DQK, A_DV), f32)
    zn = jnp.zeros((bp, A_HEADS, 1, A_DQK), f32)
    zm = jnp.zeros((bp, A_HEADS, 1, 1), f32)
    ng = norm_a_g.reshape(1, A_HEADS * A_DV)
    ya_p, p_c, p_n, p_m = _mlstm(proj, gt, ng, zc, zn, zm, row0=0, nb=bp, t=tp, L=256, valid=256)
    ya_s, s_c, s_n, s_m = _mlstm(proj, gt, ng, state_C, state_n.reshape(bs, A_HEADS, 1, A_DQK),
                                 state_m.reshape(bs, A_HEADS, 1, 1), row0=n_p, nb=bs, t=tsp, L=tsp, valid=ts)

    gsmall = proj[:, C_SMALL:C_END]
    kv_p = proj[:n_p, C_CMP:C_SMALL].reshape(bp, tp, 3, KV_COLS)
    p_cmp, p_slc, p_winrows = kv_p[:, :, 0], kv_p[:, :, 1], kv_p[:, :, 2]
    ob_p = _nsa(slopes, proj, gsmall, pe2, w2, nb=bp, t=tp, tq=256, row0=0, qpos0=0,
                cmp_src=(proj, tp, 0, C_CMP // KV_COLS), slc3=p_slc, win3=p_winrows, win_kpos0=0,
                tk_sel=512, tk_win=256)

    kv_s = proj[n_p:, C_CMP:C_SMALL].reshape(bs, tsp, 3, KV_COLS)[:, :ts]
    s_cmp, s_slc, s_winrows = kv_s[:, :, 0], kv_s[:, :, 1], kv_s[:, :, 2]
    n_pool = cache_cmp_kv.shape[0]
    tail_pad = ((0, 0), (0, PAGE_SIZE - ts), (0, 0))
    all_cmp = _gather_pages(page_table, cache_cmp_kv.reshape(n_pool, PAGE_SIZE, KV_COLS), jnp.pad(s_cmp, tail_pad))
    all_slc = _gather_pages(page_table, cache_slc_kv.reshape(n_pool, PAGE_SIZE, KV_COLS), jnp.pad(s_slc, tail_pad))
    tk_s = past + PAGES_PER_STEP * PAGE_SIZE
    all_cmp = all_cmp.reshape(bs * tk_s, KV_COLS)
    all_slc = all_slc.reshape(bs, tk_s, KV_COLS)
    wb = cache_win_kv.shape[1]
    buf = jnp.concatenate([cache_win_kv.reshape(bs, wb, KV_COLS), s_winrows], axis=1)
    s_win = buf[:, -wb:]
    win_tk = 640
    buf_pad = jnp.pad(buf, ((0, 0), (0, win_tk - buf.shape[1]), (0, 0)))
    ob_s = _nsa(slopes, proj, gsmall, pe2, w2, nb=bs, t=tsp, tq=tsp, row0=n_p, qpos0=past,
                cmp_src=(all_cmp, tk_s, 0, 0), slc3=all_slc, win3=buf_pad, win_kpos0=past - wb,
                tk_sel=512, tk_win=win_tk)

    ya = jnp.concatenate([ya_p, ya_s], axis=0)
    oc, os_, ow = (jnp.concatenate([a, b], axis=0) for a, b in zip(ob_p, ob_s))
    h1, h1b = _tail(x_all, ya, oc, os_, ow, w_merge.astype(bf16), w_br_a.astype(bf16), w_br_b.astype(bf16),
                    w_out.astype(bf16), ln1_g.reshape(1, D_MODEL), ln1_b.reshape(1, D_MODEL), 256)
    tab = _route(h1b, peer_wq.astype(bf16), peer_keys.astype(bf16), 256)
    y = _experts(h1b, h1, tab, peer_u.astype(bf16), peer_v.T.astype(bf16), ln2_g.reshape(1, D_MODEL),
                 ln2_b.reshape(1, D_MODEL), 512, 512)

    y_prompt = y[:n_p].reshape(bp, tp, D_MODEL)
    y_sample = y[n_p:].reshape(bs, tsp, D_MODEL)[:, :ts]
    kv5 = lambda a: a.reshape(a.shape[0], a.shape[1], 2, B_KV, B_HD)
    dt = x_prompt.dtype
    return (y_prompt, y_sample, kv5(p_cmp), kv5(p_slc), kv5(p_winrows[:, -min(WINDOW, tp):]),
            p_c.astype(dt), p_n.astype(dt), p_m.astype(dt),
            kv5(s_cmp), kv5(s_slc), kv5(s_win), s_c.astype(state_C.dtype), s_n.astype(state_C.dtype),
            s_m.astype(state_C.dtype))
```

```python
import functools

import jax
import jax.numpy as jnp
import numpy as np
from jax import lax
from jax.experimental import pallas as pl
from jax.experimental.pallas import tpu as pltpu

D_MODEL = 1024
A_HEADS, A_DQK, A_DV = 4, 128, 256
B_HEADS, B_KV, B_HD = 16, 4, 64
B_REP = B_HEADS // B_KV
CMP_BLOCK = 64
N_SEL = 16
WINDOW = 512
PAGE_SIZE = 128
P_HEADS, P_NKEYS, P_DHALF, P_TOPK = 8, 128, 128, 16
P_EXPERTS = P_NKEYS * P_NKEYS
DN_ALPHA = 2.0 ** 0.25
LN_EPS = 1e-5
NEG = -1e30

LANES = 128
KV_COLS = 2 * B_KV * B_HD
VMEM_LIMIT = 56 * 1024 * 1024

C_AQ, C_AK, C_AV, C_AO, C_BQ, C_CMP, C_SLC, C_WIN, C_SMALL, C_END = (
    0, 512, 1024, 2048, 3072, 4096, 4608, 5120, 5632, 5760)
G_I, G_F, G_GATE = 0, A_HEADS, 2 * A_HEADS

bf16 = jnp.bfloat16
f32 = jnp.float32


def _cparams(*sem):
    return pltpu.CompilerParams(dimension_semantics=sem, vmem_limit_bytes=VMEM_LIMIT)


def _full(shape):
    nd = len(shape)
    return pl.BlockSpec(shape, lambda *_: (0,) * nd)


def _proj_kernel(x_ref, w_ref, b_ref, o_ref):
    o_ref[...] = jnp.dot(x_ref[...], w_ref[...], preferred_element_type=f32) + b_ref[...]


def _proj(xb, w, b, tm):
    n, k = xb.shape
    e = w.shape[1]
    return pl.pallas_call(
        _proj_kernel,
        grid=(n // tm,),
        in_specs=[pl.BlockSpec((tm, k), lambda i: (i, 0)), _full((k, e)), _full((1, e))],
        out_specs=pl.BlockSpec((tm, e), lambda i: (i, 0)),
        out_shape=jax.ShapeDtypeStruct((n, e), f32),
        compiler_params=_cparams("parallel"),
        name="proj",
    )(xb, w, b)


def _proj_t_kernel(wt_ref, x_ref, b_ref, o_ref):
    o_ref[...] = lax.dot_general(wt_ref[...], x_ref[...], (((1,), (1,)), ((), ())),
                                 preferred_element_type=f32) + b_ref[...]


def _proj_t(wt, xb, bcol, tn):
    e, k = wt.shape
    n = xb.shape[0]
    return pl.pallas_call(
        _proj_t_kernel,
        grid=(n // tn,),
        in_specs=[_full((e, k)), pl.BlockSpec((tn, k), lambda i: (i, 0)), _full((e, 1))],
        out_specs=pl.BlockSpec((e, tn), lambda i: (0, i)),
        out_shape=jax.ShapeDtypeStruct((e, n), f32),
        compiler_params=_cparams("parallel"),
        name="proj_t",
    )(wt, xb, bcol)


def _mlstm_kernel(q_ref, k_ref, v_ref, ao_ref, g_ref, gt_ref, ng_ref, c0_ref, n0_ref, m0_ref,
                  y_ref, c_out, n_out, m_out, c_s, n_s, m_s, *, L, valid):
    h = pl.program_id(1)
    c = pl.program_id(2)

    @pl.when(c == 0)
    def _():
        c_s[...] = c0_ref[0, 0]
        n_s[...] = n0_ref[0, 0]
        m_s[...] = m0_ref[0, 0]

    q = q_ref[...]
    k = k_ref[...] * (A_DQK ** -0.5)
    v = v_ref[...]
    g = g_ref[...]
    gt = gt_ref[0]
    lane = lax.broadcasted_iota(jnp.int32, g.shape, 1)
    sub = lax.broadcasted_iota(jnp.int32, gt.shape, 0)
    i_col = jnp.sum(jnp.where(lane == G_I + h, g, 0.0), axis=1, keepdims=True)
    f_col = jnp.sum(jnp.where(lane == G_F + h, g, 0.0), axis=1, keepdims=True)
    i_row = jnp.sum(jnp.where(sub == G_I + h, gt, 0.0), axis=0, keepdims=True)
    f_row = jnp.sum(jnp.where(sub == G_F + h, gt, 0.0), axis=0, keepdims=True)
    lf_col = jax.nn.log_sigmoid(f_col)
    lf_row = jax.nn.log_sigmoid(f_row)
    t_col = lax.broadcasted_iota(jnp.int32, (L, 1), 0)
    s_row = lax.broadcasted_iota(jnp.int32, (1, L), 1)
    if valid < L:
        lf_col = jnp.where(t_col < valid, lf_col, 0.0)
        lf_row = jnp.where(s_row < valid, lf_row, 0.0)
        i_col = jnp.where(t_col < valid, i_col, NEG)
        i_row = jnp.where(s_row < valid, i_row, NEG)
    tt = lax.broadcasted_iota(jnp.int32, (L, L), 0)
    ss = lax.broadcasted_iota(jnp.int32, (L, L), 1)
    causal = ss <= tt
    b_col = jnp.sum(jnp.where(causal, lf_row, 0.0), axis=1, keepdims=True)
    b_row = jnp.sum(jnp.where(tt <= ss, lf_col, 0.0), axis=0, keepdims=True)
    m_prev = m_s[...]
    cmat = c_s[...]
    n_row = n_s[...]

    d_log = jnp.where(causal, b_col - b_row + i_row, NEG)
    inter = b_col + m_prev
    m_t = jnp.maximum(inter, jnp.max(d_log, axis=1, keepdims=True))
    qb = q.astype(bf16)
    qk = lax.dot_general(qb, k.astype(bf16), (((1,), (1,)), ((), ())), preferred_element_type=f32)
    smat = qk * jnp.exp(d_log - m_t)
    w_inter = jnp.exp(inter - m_t)
    vb = v.astype(bf16)
    num = (w_inter * jnp.dot(qb, cmat.astype(bf16), preferred_element_type=f32)
           + jnp.dot(smat.astype(bf16), vb, preferred_element_type=f32))
    den = w_inter * jnp.sum(q * n_row, axis=1, keepdims=True) + jnp.sum(smat, axis=1, keepdims=True)
    hid = num / jnp.maximum(jnp.abs(den), jnp.exp(-m_t))
    mu = jnp.mean(hid, axis=1, keepdims=True)
    var = jnp.mean(jnp.square(hid - mu), axis=1, keepdims=True)
    hid = (hid - mu) * lax.rsqrt(var + LN_EPS) * ng_ref[...]
    y_ref[...] = hid * jax.nn.sigmoid(ao_ref[...])

    b_end = b_col[L - 1:L, :]
    g_row = b_end - b_row + i_row
    m_new = jnp.maximum(b_end + m_prev, jnp.max(g_row, axis=1, keepdims=True))
    a = jnp.exp(b_end + m_prev - m_new)
    w_col = jnp.exp(b_end - b_col + i_col - m_new)
    kw = k * w_col
    c_new = a * cmat + lax.dot_general(kw.astype(bf16), vb, (((0,), (0,)), ((), ())),
                                       preferred_element_type=f32)
    n_new = a * n_row + jnp.sum(kw, axis=0, keepdims=True)
    c_s[...] = c_new
    n_s[...] = n_new
    m_s[...] = m_new

    @pl.when(c == pl.num_programs(2) - 1)
    def _():
        c_out[0, 0] = c_new
        n_out[0, 0] = n_new
        m_out[0, 0] = m_new


def _mlstm(proj, gt, norm_g, c0, n0, m0, *, row0, nb, t, L, valid):
    nc = t // L
    rb0 = row0 // L
    gt = gt[:8, row0:row0 + nb * t].reshape(8, nb * nc, L).transpose(1, 0, 2)
    rows = lambda b, h, c: rb0 + b * nc + c
    st = lambda b, h, c: (b, h, 0, 0)
    y, c_f, n_f, m_f = pl.pallas_call(
        functools.partial(_mlstm_kernel, L=L, valid=valid),
        grid=(nb, A_HEADS, nc),
        in_specs=[
            pl.BlockSpec((L, A_DQK), lambda b, h, c: (rows(b, h, c), C_AQ // A_DQK + h)),
            pl.BlockSpec((L, A_DQK), lambda b, h, c: (rows(b, h, c), C_AK // A_DQK + h)),
            pl.BlockSpec((L, A_DV), lambda b, h, c: (rows(b, h, c), C_AV // A_DV + h)),
            pl.BlockSpec((L, A_DV), lambda b, h, c: (rows(b, h, c), C_AO // A_DV + h)),
            pl.BlockSpec((L, LANES), lambda b, h, c: (rows(b, h, c), C_SMALL // LANES)),
            pl.BlockSpec((1, 8, L), lambda b, h, c: (b * nc + c, 0, 0)),
            pl.BlockSpec((1, A_DV), lambda b, h, c: (0, h)),
            pl.BlockSpec((1, 1, A_DQK, A_DV), st),
            pl.BlockSpec((1, 1, 1, A_DQK), st),
            pl.BlockSpec((1, 1, 1, 1), st),
        ],
        out_specs=[
            pl.BlockSpec((L, A_DV), lambda b, h, c: (b * nc + c, h)),
            pl.BlockSpec((1, 1, A_DQK, A_DV), st),
            pl.BlockSpec((1, 1, 1, A_DQK), st),
            pl.BlockSpec((1, 1, 1, 1), st),
        ],
        out_shape=[
            jax.ShapeDtypeStruct((nb * t, A_HEADS * A_DV), f32),
            jax.ShapeDtypeStruct((nb, A_HEADS, A_DQK, A_DV), f32),
            jax.ShapeDtypeStruct((nb, A_HEADS, 1, A_DQK), f32),
            jax.ShapeDtypeStruct((nb, A_HEADS, 1, 1), f32),
        ],
        scratch_shapes=[pltpu.VMEM((A_DQK, A_DV), f32), pltpu.VMEM((1, A_DQK), f32), pltpu.VMEM((1, 1), f32)],
        compiler_params=_cparams("parallel", "parallel", "arbitrary"),
        name="mlstm",
    )(proj, proj, proj, proj, proj, gt, norm_g, c0, n0, m0)
    return y, c_f, n_f[:, :, 0], m_f[:, :, 0, 0]


def _compress_kernel(x_ref, pe_ref, w_ref, o_ref, xf_ref, *, nblk):
    for l in range(CMP_BLOCK):
        xf_ref[:, l * LANES:(l + 1) * LANES] = x_ref[pl.ds(l, nblk, stride=CMP_BLOCK), :]
    xf = (xf_ref[...] + pe_ref[0]).astype(bf16)
    o_ref[...] = jnp.dot(xf, w_ref[0], preferred_element_type=f32)


def _compress(x2, pe2, w2, *, rows, steps, row0, colblk):
    nblk = rows // CMP_BLOCK
    kflat = CMP_BLOCK * LANES
    rb0 = row0 // rows
    ngrp = KV_COLS // LANES
    return pl.pallas_call(
        functools.partial(_compress_kernel, nblk=nblk),
        grid=(steps, ngrp),
        in_specs=[pl.BlockSpec((rows, LANES), lambda s, p: (rb0 + s, colblk * ngrp + p)),
                  pl.BlockSpec((1, 1, kflat), lambda s, p: (p // 2, 0, 0)),
                  pl.BlockSpec((1, kflat, LANES), lambda s, p: (p // 2, 0, 0))],
        out_specs=pl.BlockSpec((nblk, LANES), lambda s, p: (s, p)),
        out_shape=jax.ShapeDtypeStruct((steps * nblk, KV_COLS), f32),
        scratch_shapes=[pltpu.VMEM((nblk, kflat), f32)],
        compiler_params=_cparams("parallel", "parallel"),
        name="compress",
    )(x2, pe2, w2)


def _gate_col(g, lane_idx):
    lane = lax.broadcasted_iota(jnp.int32, g.shape, 1)
    return jax.nn.sigmoid(jnp.sum(jnp.where(lane == lane_idx, g, 0.0), axis=1, keepdims=True))


def _cmp_topk_kernel(slope_ref, q_ref, kc_ref, vc_ref, g_ref, o_ref, sel_ref, *, tq, nblk, qpos0):
    gi = pl.program_id(1)
    i = pl.program_id(2)
    qpos = qpos0 + i * tq + lax.broadcasted_iota(jnp.int32, (tq, 1), 0)
    j = lax.broadcasted_iota(jnp.int32, (1, nblk), 1)
    dist = qpos - ((j + 1) * CMP_BLOCK - 1)
    valid = dist >= 0
    distf = dist.astype(f32)
    kc = kc_ref[0, 0].astype(bf16)
    vc = vc_ref[0, 0].astype(bf16)
    g = g_ref[...]
    imp = jnp.zeros((tq, nblk), f32)
    for r in range(B_REP):
        qr = q_ref[0, 0, 0, r * tq:(r + 1) * tq, :]
        s = lax.dot_general(qr, kc, (((1,), (1,)), ((), ())), preferred_element_type=f32) * (B_HD ** -0.5)
        s = s - slope_ref[gi * B_REP + r] * distf
        s = jnp.where(valid, s, NEG)
        e = jnp.exp(s - jnp.max(s, axis=1, keepdims=True))
        p = jnp.where(valid, e / jnp.sum(e, axis=1, keepdims=True), 0.0)
        imp = imp + p
        o = jnp.dot(p.astype(bf16), vc, preferred_element_type=f32)
        o_ref[0, 0, 0, r * tq:(r + 1) * tq, :] = o * _gate_col(g, G_GATE + gi * B_REP + r)
    cur = qpos // CMP_BLOCK
    imp = jnp.where((j == cur) | (j == 0), float(B_REP + 1), imp)
    imp = jnp.where(j > cur, -1.0, imp)
    jf = j.astype(f32)
    sel = jnp.zeros((tq, nblk), f32)
    for _ in range(N_SEL):
        mx = jnp.max(imp, axis=1, keepdims=True)
        idx = jnp.min(jnp.where(imp == mx, jf, float(nblk)), axis=1, keepdims=True)
        hit = jf == idx
        sel = jnp.where(hit, 1.0, sel)
        imp = jnp.where(hit, NEG, imp)
    sel_ref[0, 0] = sel


def _cmp_topk(slopes, qt, kc, vc, gsmall, *, tq, qpos0, row0):
    nb, _, nqt, _, _ = qt.shape
    nblk = kc.shape[2]
    rb0 = row0 // tq
    return pl.pallas_call(
        functools.partial(_cmp_topk_kernel, tq=tq, nblk=nblk, qpos0=qpos0),
        grid=(nb, B_KV, nqt),
        in_specs=[
            pl.BlockSpec(memory_space=pltpu.SMEM),
            pl.BlockSpec((1, 1, 1, B_REP * tq, B_HD), lambda b, g, i: (b, g, i, 0, 0)),
            pl.BlockSpec((1, 1, nblk, B_HD), lambda b, g, i: (b, g, 0, 0)),
            pl.BlockSpec((1, 1, nblk, B_HD), lambda b, g, i: (b, g, 0, 0)),
            pl.BlockSpec((tq, LANES), lambda b, g, i: (rb0 + b * nqt + i, 0)),
        ],
        out_specs=[
            pl.BlockSpec((1, 1, 1, B_REP * tq, B_HD), lambda b, g, i: (b, g, i, 0, 0)),
            pl.BlockSpec((1, 1, tq, nblk), lambda b, g, i: (b, g, i, 0)),
        ],
        out_shape=[
            jax.ShapeDtypeStruct(qt.shape, f32),
            jax.ShapeDtypeStruct((nb, B_KV, nqt * tq, nblk), f32),
        ],
        compiler_params=_cparams("parallel", "parallel", "parallel"),
        name="cmp_topk",
    )(slopes, qt, kc, vc, gsmall)


def _attn_kernel(slope_ref, q_ref, k_ref, v_ref, g_ref, *rest, tq, tk, nkt, nblk, qpos0, kpos0, window,
                 gate_lane):
    if window is None:
        sel_ref, o_ref, m_s, l_s, acc_s = rest
    else:
        o_ref, m_s, l_s, acc_s = rest
    gi = pl.program_id(1)
    i = pl.program_id(2)
    qlo = qpos0 + i * tq
    qpos = qlo + lax.broadcasted_iota(jnp.int32, (tq, 1), 0)
    m_s[...] = jnp.full(m_s.shape, NEG, f32)
    l_s[...] = jnp.zeros(l_s.shape, f32)
    acc_s[...] = jnp.zeros(acc_s.shape, f32)
    kt_hi = jnp.minimum((qlo + tq - 1 - kpos0) // tk + 1, nkt)
    if window is None:
        kt_lo = 0
        selb = sel_ref[0, 0].astype(bf16)
    else:
        kt_lo = jnp.maximum(qlo - (window - 1) - kpos0, 0) // tk

    def body(kt, carry):
        k0 = pl.multiple_of(kt * tk, tk)
        kb = k_ref[0, 0, pl.ds(k0, tk), :]
        vb = v_ref[0, 0, pl.ds(k0, tk), :]
        kidx = k0 + lax.broadcasted_iota(jnp.int32, (1, tk), 1)
        dist = qpos - (kpos0 + kidx)
        mask = dist >= 0
        if window is None:
            blk = lax.broadcasted_iota(jnp.int32, (nblk, tk), 0)
            kblk = (k0 + lax.broadcasted_iota(jnp.int32, (nblk, tk), 1)) // CMP_BLOCK
            expand = jnp.where(blk == kblk, 1.0, 0.0).astype(bf16)
            mask = mask & (jnp.dot(selb, expand, preferred_element_type=f32) > 0.5)
        else:
            mask = mask & (dist < window)
        distf = dist.astype(f32)
        for r in range(B_REP):
            rows = slice(r * tq, (r + 1) * tq)
            s = lax.dot_general(q_ref[0, 0, 0, rows, :], kb, (((1,), (1,)), ((), ())),
                                preferred_element_type=f32) * (B_HD ** -0.5)
            s = jnp.where(mask, s - slope_ref[gi * B_REP + r] * distf, NEG)
            m_old = m_s[rows, :]
            m_new = jnp.maximum(m_old, jnp.max(s, axis=1, keepdims=True))
            alpha = jnp.exp(m_old - m_new)
            p = jnp.exp(s - m_new)
            l_s[rows, :] = alpha * l_s[rows, :] + jnp.sum(p, axis=1, keepdims=True)
            acc_s[rows, :] = alpha * acc_s[rows, :] + jnp.dot(p.astype(bf16), vb, preferred_element_type=f32)
            m_s[rows, :] = m_new
        return carry

    lax.fori_loop(kt_lo, kt_hi, body, 0)
    g = g_ref[...]
    for r in range(B_REP):
        rows = slice(r * tq, (r + 1) * tq)
        o_ref[0, 0, 0, rows, :] = acc_s[rows, :] / l_s[rows, :] * _gate_col(g, gate_lane + gi * B_REP + r)


def _attn(slopes, qt, kh, vh, gsmall, sel, *, tq, tk, qpos0, kpos0, window, gate_lane, row0):
    nb, _, nqt, _, _ = qt.shape
    tkk = kh.shape[2]
    nkt = tkk // tk
    nblk = None if sel is None else sel.shape[3]
    rb0 = row0 // tq
    in_specs = [
        pl.BlockSpec(memory_space=pltpu.SMEM),
        pl.BlockSpec((1, 1, 1, B_REP * tq, B_HD), lambda b, g, i: (b, g, i, 0, 0)),
        pl.BlockSpec((1, 1, tkk, B_HD), lambda b, g, i: (b, g, 0, 0)),
        pl.BlockSpec((1, 1, tkk, B_HD), lambda b, g, i: (b, g, 0, 0)),
        pl.BlockSpec((tq, LANES), lambda b, g, i: (rb0 + b * nqt + i, 0)),
    ]
    args = [slopes, qt, kh, vh, gsmall]
    if sel is not None:
        in_specs.append(pl.BlockSpec((1, 1, tq, nblk), lambda b, g, i: (b, g, i, 0)))
        args.append(sel)
    return pl.pallas_call(
        functools.partial(_attn_kernel, tq=tq, tk=tk, nkt=nkt, nblk=nblk, qpos0=qpos0, kpos0=kpos0,
                          window=window, gate_lane=gate_lane),
        grid=(nb, B_KV, nqt),
        in_specs=in_specs,
        out_specs=pl.BlockSpec((1, 1, 1, B_REP * tq, B_HD), lambda b, g, i: (b, g, i, 0, 0)),
        out_shape=jax.ShapeDtypeStruct(qt.shape, f32),
        scratch_shapes=[pltpu.VMEM((B_REP * tq, 1), f32), pltpu.VMEM((B_REP * tq, 1), f32),
                        pltpu.VMEM((B_REP * tq, B_HD), f32)],
        compiler_params=_cparams("parallel", "parallel", "parallel"),
        name="attn_sel" if window is None else "attn_win",
    )(*args)


PAD_PAGES = 4


def _gather_kernel(pt_ref, *refs):
    del pt_ref
    pages, tail_ref, o_ref = refs[:-2], refs[-2], refs[-1]
    for u, page in enumerate(pages):
        o_ref[0, u] = page[0]
    o_ref[0, len(pages)] = tail_ref[0]
    for u in range(len(pages) + 1, len(pages) + PAD_PAGES):
        o_ref[0, u] = jnp.zeros(o_ref.shape[2:], f32)


def _gather_pages(page_table, cache, tail):
    nb, n_pages = page_table.shape
    page = (1, PAGE_SIZE, KV_COLS)
    return pl.pallas_call(
        _gather_kernel,
        grid_spec=pltpu.PrefetchScalarGridSpec(
            num_scalar_prefetch=1,
            grid=(nb,),
            in_specs=[pl.BlockSpec(page, functools.partial(lambda b, pt, u: (pt[b, u], 0, 0), u=u))
                      for u in range(n_pages)] + [pl.BlockSpec(page, lambda b, pt: (b, 0, 0))],
            out_specs=pl.BlockSpec((1, n_pages + PAD_PAGES, PAGE_SIZE, KV_COLS), lambda b, pt: (b, 0, 0, 0)),
        ),
        out_shape=jax.ShapeDtypeStruct((nb, n_pages + PAD_PAGES, PAGE_SIZE, KV_COLS), f32),
        compiler_params=_cparams("parallel"),
        name="gather_pages",
    )(page_table, *([cache] * n_pages), tail)


def _layer_norm(z, g, b):
    mu = jnp.mean(z, axis=1, keepdims=True)
    var = jnp.mean(jnp.square(z - mu), axis=1, keepdims=True)
    return (z - mu) * lax.rsqrt(var + LN_EPS) * g + b


def _tail_kernel(x_ref, ya_ref, oc_ref, os_ref, ow_ref, wm_ref, wa_ref, wb_ref, wo_ref, g_ref, b_ref,
                 h_ref, hb_ref):
    x = x_ref[...]
    gates = jax.nn.sigmoid(jnp.dot(x.astype(bf16), wm_ref[...], preferred_element_type=f32))
    yb = oc_ref[...] + os_ref[...] + ow_ref[...]
    ma = jnp.dot(ya_ref[...].astype(bf16), wa_ref[...], preferred_element_type=f32)
    mb = jnp.dot(yb.astype(bf16), wb_ref[...], preferred_element_type=f32)
    merged = gates[:, :D_MODEL] * ma + gates[:, D_MODEL:] * mb
    z = DN_ALPHA * x + jnp.dot(merged.astype(bf16), wo_ref[...], preferred_element_type=f32)
    h = _layer_norm(z, g_ref[...], b_ref[...])
    h_ref[...] = h
    hb_ref[...] = h.astype(bf16)


def _tail(x, ya, oc, os_, ow, wm, wa, wb, wo, g, b, tm):
    n = x.shape[0]
    row = pl.BlockSpec((tm, D_MODEL), lambda i: (i, 0))
    return pl.pallas_call(
        _tail_kernel,
        grid=(n // tm,),
        in_specs=[row] * 5 + [_full(wm.shape), _full(wa.shape), _full(wb.shape), _full(wo.shape),
                              _full(g.shape), _full(b.shape)],
        out_specs=[row, row],
        out_shape=[jax.ShapeDtypeStruct((n, D_MODEL), f32), jax.ShapeDtypeStruct((n, D_MODEL), bf16)],
        compiler_params=_cparams("parallel"),
        name="tail",
    )(x, ya, oc, os_, ow, wm, wa, wb, wo, g, b)


def _top16(x):
    kk, tb = x.shape
    ji = lax.broadcasted_iota(jnp.int32, (kk, tb), 0).astype(f32)
    rank = jnp.full((kk, tb), float(P_TOPK), f32)
    vals = []
    for k in range(P_TOPK):
        mx = jnp.max(x, axis=0, keepdims=True)
        idx = jnp.min(jnp.where(x == mx, ji, float(kk)), axis=0, keepdims=True)
        hit = ji == idx
        rank = jnp.where(hit, float(k), rank)
        vals.append(mx)
        x = jnp.where(hit, NEG, x)
    return rank, vals


_CAND_ROWS8 = ((1, 8), (2, 5), (3, 4), (4, 3))


def _route_kernel(h_ref, wqt_ref, keys_ref, ta_ref, tb_ref):
    qpt = lax.dot_general(wqt_ref[...], h_ref[...], (((1,), (1,)), ((), ())), preferred_element_type=f32)
    tb = qpt.shape[1]
    sub16 = lax.broadcasted_iota(jnp.int32, (P_TOPK, tb), 0)
    sub8 = lax.broadcasted_iota(jnp.int32, (8, tb), 0)
    for p in range(P_HEADS):
        sc, rk, vl = [], [], []
        for c in range(2):
            qs = qpt[(2 * p + c) * P_DHALF:(2 * p + c + 1) * P_DHALF, :].astype(bf16)
            s = jnp.dot(keys_ref[p, c], qs, preferred_element_type=f32)
            r, v = _top16(s)
            sc.append(s)
            rk.append(r)
            vl.append(v)
        v0, v1 = vl
        col0 = functools.reduce(lambda acc, k: jnp.where(sub16 == k, v0[k], acc), range(P_TOPK), jnp.zeros((P_TOPK, tb), f32))
        col1 = functools.reduce(lambda acc, k: jnp.where(sub16 == k, v1[k], acc), range(P_TOPK), jnp.zeros((P_TOPK, tb), f32))
        segs = [v0[0] + col1]
        for k1, keep in _CAND_ROWS8:
            segs.append(jnp.where(sub8 < keep, v0[k1] + col1[0:8], NEG))
        first = jnp.where(sub8 < 2, v0[5], jnp.where(sub8 < 4, v0[6], v0[7]))
        second = jnp.where(sub8 % 2 == 0, v1[0], v1[1])
        segs.append(jnp.where(sub8 < 6, first + second, NEG))
        segs.append(col0[8:16] + v1[0])
        cand = jnp.concatenate(segs, axis=0)
        crank, cvals = _top16(cand)
        taken = jnp.where(crank < float(P_TOPK), 1.0, 0.0)
        z = functools.reduce(lambda acc, v: acc + jnp.exp(v - cvals[0]), cvals, jnp.zeros((1, tb), f32))
        cnt = [jnp.sum(taken[0:16], axis=0, keepdims=True)]
        for i in range(len(_CAND_ROWS8)):
            cnt.append(jnp.sum(taken[16 + 8 * i:24 + 8 * i], axis=0, keepdims=True))
        t5 = taken[48:56]
        for lo in (0, 2, 4):
            cnt.append(jnp.sum(jnp.where((sub8 >= lo) & (sub8 < lo + 2), t5, 0.0), axis=0, keepdims=True))
        for i in range(8):
            cnt.append(taken[56 + i:57 + i])
        n_a = functools.reduce(lambda acc, k: jnp.where(rk[0] == float(k), cnt[k], acc), range(P_TOPK),
                               jnp.zeros((P_NKEYS, tb), f32))
        ta_ref[p, 0] = n_a
        ta_ref[p, 1] = jnp.exp(sc[0] - v0[0])
        tb_ref[p, 0] = rk[1].astype(bf16)
        tb_ref[p, 1] = (jnp.exp(sc[1] - v1[0]) / z).astype(bf16)


def _route(hb, wqt, keys, tb):
    n = hb.shape[0]
    spec = pl.BlockSpec((P_HEADS, 2, P_NKEYS, tb), lambda i: (0, 0, 0, i))
    return pl.pallas_call(
        _route_kernel,
        grid=(n // tb,),
        in_specs=[pl.BlockSpec((tb, D_MODEL), lambda i: (i, 0)), _full(wqt.shape), _full(keys.shape)],
        out_specs=[spec, spec],
        out_shape=[jax.ShapeDtypeStruct((P_HEADS, 2, P_NKEYS, n), f32),
                   jax.ShapeDtypeStruct((P_HEADS, 2, P_NKEYS, n), bf16)],
        compiler_params=_cparams("parallel"),
        name="peer_route",
    )(hb, wqt, keys)


def _experts_kernel(hb_ref, h_ref, ta_ref, tb_ref, u_ref, vt_ref, g_ref, b_ref, y_ref, acc_s, ht_s, pt_s, *, te):
    j = pl.program_id(1)

    @pl.when(j == 0)
    def _():
        acc_s[...] = jnp.zeros(acc_s.shape, f32)

    ht_s[...] = lax.dot_general(u_ref[...], hb_ref[...], (((1,), (1,)), ((), ())), preferred_element_type=f32)
    for aa in range(te // P_NKEYS):
        a = j * (te // P_NKEYS) + aa
        n_rows = [ta_ref[p, 0, pl.ds(a, 1), :].astype(bf16) for p in range(P_HEADS)]
        e0_rows = [ta_ref[p, 1, pl.ds(a, 1), :].astype(bf16) for p in range(P_HEADS)]
        for lt in range(ht_s.shape[1] // LANES):
            ls = slice(lt * LANES, (lt + 1) * LANES)
            w = jnp.zeros((P_NKEYS, LANES), bf16)
            for p in range(P_HEADS):
                w = w + jnp.where(tb_ref[p, 0, :, ls] < n_rows[p][:, ls], e0_rows[p][:, ls] * tb_ref[p, 1, :, ls], 0.0)
            hs = ht_s[aa * P_NKEYS:(aa + 1) * P_NKEYS, ls]
            act = 0.5 * hs * (1.0 + lax.erf(hs * (0.5 ** 0.5)))
            pt_s[aa * P_NKEYS:(aa + 1) * P_NKEYS, ls] = w * act.astype(bf16)
    acc_s[...] += jnp.dot(vt_ref[...], pt_s[...], preferred_element_type=f32)

    @pl.when(j == pl.num_programs(1) - 1)
    def _():
        z = DN_ALPHA * h_ref[...] + acc_s[...].T
        y_ref[...] = _layer_norm(z, g_ref[...], b_ref[...])


def _experts(hb, h, ta, tbl, u, vt, g, b, tb, te):
    n = hb.shape[0]
    row = pl.BlockSpec((tb, D_MODEL), lambda i, j: (i, 0))
    tab = pl.BlockSpec((P_HEADS, 2, P_NKEYS, tb), lambda i, j: (0, 0, 0, i))
    return pl.pallas_call(
        functools.partial(_experts_kernel, te=te),
        grid=(n // tb, P_EXPERTS // te),
        in_specs=[row, row, tab, tab,
                  pl.BlockSpec((te, D_MODEL), lambda i, j: (j, 0)),
                  pl.BlockSpec((D_MODEL, te), lambda i, j: (0, j)),
                  pl.BlockSpec((1, D_MODEL), lambda i, j: (0, 0)),
                  pl.BlockSpec((1, D_MODEL), lambda i, j: (0, 0))],
        out_specs=row,
        out_shape=jax.ShapeDtypeStruct((n, D_MODEL), f32),
        scratch_shapes=[pltpu.VMEM((D_MODEL, tb), f32), pltpu.VMEM((te, tb), f32), pltpu.VMEM((te, tb), bf16)],
        compiler_params=_cparams("parallel", "arbitrary"),
        name="peer_experts",
    )(hb, h, ta, tbl, u, vt, g, b)


def _to_q_tiles(q2, nb, t, tq):
    q = q2.reshape(nb, t // tq, tq, B_KV, B_REP, B_HD).transpose(0, 3, 1, 4, 2, 5)
    return q.reshape(nb, B_KV, t // tq, B_REP * tq, B_HD).astype(bf16)


def _from_q_tiles(o, nb, t, tq):
    o = o.reshape(nb, B_KV, t // tq, B_REP, tq, B_HD).transpose(0, 2, 4, 1, 3, 5)
    return o.reshape(nb, t, B_HEADS * B_HD)


def _kv_heads(kv3):
    nb, tk, _ = kv3.shape
    kv = kv3.reshape(nb, tk, 2, B_KV, B_HD).transpose(2, 0, 3, 1, 4)
    return kv[0], kv[1]


def _nsa(slopes, proj, gsmall, pe2, w2, *, nb, t, tq, row0, qpos0, cmp_src, slc3, win3, win_kpos0, tk_sel, tk_win):
    qt = _to_q_tiles(proj[row0:row0 + nb * t, C_BQ:C_CMP], nb, t, tq)
    x2, cmp_rows, cmp_steps, cmp_colblk = cmp_src
    kvc = _compress(x2, pe2, w2, rows=cmp_rows, steps=cmp_steps, row0=0, colblk=cmp_colblk)
    kc, vc = _kv_heads(kvc.reshape(nb, -1, KV_COLS))
    o_cmp, sel = _cmp_topk(slopes, qt, kc, vc, gsmall, tq=tq, qpos0=qpos0, row0=row0)
    ks, vs = _kv_heads(slc3)
    o_slc = _attn(slopes, qt, ks.astype(bf16), vs.astype(bf16), gsmall, sel, tq=tq, tk=tk_sel, qpos0=qpos0,
                  kpos0=0, window=None, gate_lane=G_GATE + B_HEADS, row0=row0)
    kw, vw = _kv_heads(win3)
    o_win = _attn(slopes, qt, kw.astype(bf16), vw.astype(bf16), gsmall, None, tq=tq, tk=tk_win, qpos0=qpos0,
                  kpos0=win_kpos0, window=WINDOW, gate_lane=G_GATE + 2 * B_HEADS, row0=row0)
    return tuple(_from_q_tiles(o, nb, t, tq).reshape(nb * t, B_HEADS * B_HD) for o in (o_cmp, o_slc, o_win))


def kernel(x_prompt, x_sample, cache_cmp_kv, cache_slc_kv, cache_win_kv, state_C, state_n, state_m, page_table,
           w_in, b_in, norm_a_g, nsa_pe, nsa_w_cmp, w_br_a, w_br_b, w_merge, w_out, ln1_g, ln1_b,
           peer_wq, peer_keys, peer_u, peer_v, ln2_g, ln2_b):
    bp, tp, _ = x_prompt.shape
    bs, ts, _ = x_sample.shape
    tsp = 8
    n_p, n_s = bp * tp, bs * tsp
    past = page_table.shape[1] * PAGE_SIZE

    perm = np.concatenate([np.arange(0, 2048), np.arange(2056, 5640), np.arange(2048, 2056), np.arange(5640, 5688)])
    w_perm = jnp.pad(w_in[:, perm], ((0, 0), (0, C_END - perm.size)))
    b_perm = jnp.pad(b_in[perm], (0, C_END - perm.size))
    w_perm_b = w_perm.astype(bf16)
    slopes = jnp.asarray(2.0 ** (-8.0 * np.arange(1, B_HEADS + 1) / B_HEADS), f32)
    wc = nsa_w_cmp.reshape(2, CMP_BLOCK, 1, B_HD, 1, B_HD)
    eye2 = jnp.eye(2, dtype=f32).reshape(1, 1, 2, 1, 2, 1)
    w2 = (wc * eye2).reshape(2, CMP_BLOCK * LANES, LANES).astype(bf16)
    pe2 = jnp.tile(nsa_pe, (1, 1, 2)).reshape(2, 1, CMP_BLOCK * LANES)

    xs_pad = jnp.pad(x_sample, ((0, 0), (0, tsp - ts), (0, 0)))
    x_all = jnp.concatenate([x_prompt.reshape(n_p, D_MODEL), xs_pad.reshape(n_s, D_MODEL)], axis=0)
    xb = x_all.astype(bf16)
    proj = _proj(xb, w_perm_b, b_perm.reshape(1, C_END), 256)
    gt = _proj_t(w_perm_b[:, C_SMALL:].T, xb, b_perm[C_SMALL:].reshape(LANES, 1), 512)

    zc = jnp.zeros((bp, A_HEADS, A_DQK, A_DV), f32)
    zn = jnp.zeros((bp, A_HEADS, 1, A_DQK), f32)
    zm = jnp.zeros((bp, A_HEADS, 1, 1), f32)
    ng = norm_a_g.reshape(1, A_HEADS * A_DV)
    ya_p, p_c, p_n, p_m = _mlstm(proj, gt, ng, zc, zn, zm, row0=0, nb=bp, t=tp, L=256, valid=256)
    ya_s, s_c, s_n, s_m = _mlstm(proj, gt, ng, state_C, state_n.reshape(bs, A_HEADS, 1, A_DQK),
                                 state_m.reshape(bs, A_HEADS, 1, 1), row0=n_p, nb=bs, t=tsp, L=tsp, valid=ts)

    gsmall = proj[:, C_SMALL:C_END]
    kv_p = proj[:n_p, C_CMP:C_SMALL].reshape(bp, tp, 3, KV_COLS)
    p_cmp, p_slc, p_winrows = kv_p[:, :, 0], kv_p[:, :, 1], kv_p[:, :, 2]
    ob_p = _nsa(slopes, proj, gsmall, pe2, w2, nb=bp, t=tp, tq=256, row0=0, qpos0=0,
                cmp_src=(proj, n_p, 1, C_CMP // KV_COLS), slc3=p_slc, win3=p_winrows, win_kpos0=0,
                tk_sel=512, tk_win=256)

    kv_s = proj[n_p:, C_CMP:C_SMALL].reshape(bs, tsp, 3, KV_COLS)[:, :ts]
    s_cmp, s_slc, s_winrows = kv_s[:, :, 0], kv_s[:, :, 1], kv_s[:, :, 2]
    n_pool = cache_cmp_kv.shape[0]
    tail_pad = ((0, 0), (0, PAGE_SIZE - ts), (0, 0))
    all_cmp = _gather_pages(page_table, cache_cmp_kv.reshape(n_pool, PAGE_SIZE, KV_COLS), jnp.pad(s_cmp, tail_pad))
    all_slc = _gather_pages(page_table, cache_slc_kv.reshape(n_pool, PAGE_SIZE, KV_COLS), jnp.pad(s_slc, tail_pad))
    tk_s = past + PAD_PAGES * PAGE_SIZE
    seqs_per_step = 8
    all_cmp = all_cmp.reshape(bs * tk_s, KV_COLS)
    all_slc = all_slc.reshape(bs, tk_s, KV_COLS)
    wb = cache_win_kv.shape[1]
    buf = jnp.concatenate([cache_win_kv.reshape(bs, wb, KV_COLS), s_winrows], axis=1)
    s_win = buf[:, -wb:]
    win_tk = 640
    buf_pad = jnp.pad(buf, ((0, 0), (0, win_tk - buf.shape[1]), (0, 0)))
    ob_s = _nsa(slopes, proj, gsmall, pe2, w2, nb=bs, t=tsp, tq=tsp, row0=n_p, qpos0=past,
                cmp_src=(all_cmp, seqs_per_step * tk_s, bs // seqs_per_step, 0), slc3=all_slc, win3=buf_pad,
                win_kpos0=past - wb,
                tk_sel=512, tk_win=win_tk)

    ya = jnp.concatenate([ya_p, ya_s], axis=0)
    oc, os_, ow = (jnp.concatenate([a, b], axis=0) for a, b in zip(ob_p, ob_s))
    h1, h1b = _tail(x_all, ya, oc, os_, ow, w_merge.astype(bf16), w_br_a.astype(bf16), w_br_b.astype(bf16),
                    w_out.astype(bf16), ln1_g.reshape(1, D_MODEL), ln1_b.reshape(1, D_MODEL), 256)
    tab_a, tab_b = _route(h1b, peer_wq.T.astype(bf16), peer_keys.astype(bf16), 256)
    y = _experts(h1b, h1, tab_a, tab_b, peer_u.astype(bf16), peer_v.T.astype(bf16), ln2_g.reshape(1, D_MODEL),
                 ln2_b.reshape(1, D_MODEL), 512, 512)

    y_prompt = y[:n_p].reshape(bp, tp, D_MODEL)
    y_sample = y[n_p:].reshape(bs, tsp, D_MODEL)[:, :ts]
    kv5 = lambda a: a.reshape(a.shape[0], a.shape[1], 2, B_KV, B_HD)
    dt = x_prompt.dtype
    return (y_prompt, y_sample, kv5(p_cmp), kv5(p_slc), kv5(p_winrows[:, -min(WINDOW, tp):]),
            p_c.astype(dt), p_n.astype(dt), p_m.astype(dt),
            kv5(s_cmp), kv5(s_slc), kv5(s_win), s_c.astype(state_C.dtype), s_n.astype(state_C.dtype),
            s_m.astype(state_C.dtype))
```

```python
import functools

import jax
import jax.numpy as jnp
import numpy as np
from jax import lax
from jax.experimental import pallas as pl
from jax.experimental.pallas import tpu as pltpu

D_MODEL = 1024
A_HEADS, A_DQK, A_DV = 4, 128, 256
B_HEADS, B_KV, B_HD = 16, 4, 64
B_REP = B_HEADS // B_KV
CMP_BLOCK = 64
N_SEL = 16
WINDOW = 512
PAGE_SIZE = 128
P_HEADS, P_NKEYS, P_DHALF, P_TOPK = 8, 128, 128, 16
P_EXPERTS = P_NKEYS * P_NKEYS
DN_ALPHA = 2.0 ** 0.25
LN_EPS = 1e-5
NEG = -1e30

LANES = 128
KV_COLS = 2 * B_KV * B_HD
VMEM_LIMIT = 56 * 1024 * 1024

C_AQ, C_AK, C_AV, C_AO, C_BQ, C_CMP, C_SLC, C_WIN, C_SMALL, C_END = (
    0, 512, 1024, 2048, 3072, 4096, 4608, 5120, 5632, 5760)
G_I, G_F, G_GATE = 0, A_HEADS, 2 * A_HEADS

bf16 = jnp.bfloat16
f32 = jnp.float32


def _cparams(*sem):
    return pltpu.CompilerParams(dimension_semantics=sem, vmem_limit_bytes=VMEM_LIMIT)


def _full(shape):
    nd = len(shape)
    return pl.BlockSpec(shape, lambda *_: (0,) * nd)


def _proj_kernel(x_ref, w_ref, b_ref, o_ref, kvb_ref):
    res = jnp.dot(x_ref[...], w_ref[...], preferred_element_type=f32) + b_ref[...]
    o_ref[...] = res
    kvb_ref[...] = res[:, C_CMP:C_SMALL].astype(bf16)


def _proj(xb, w, b, tm):
    n, k = xb.shape
    e = w.shape[1]
    return pl.pallas_call(
        _proj_kernel,
        grid=(n // tm,),
        in_specs=[pl.BlockSpec((tm, k), lambda i: (i, 0)), _full((k, e)), _full((1, e))],
        out_specs=[pl.BlockSpec((tm, e), lambda i: (i, 0)), pl.BlockSpec((tm, C_SMALL - C_CMP), lambda i: (i, 0))],
        out_shape=[jax.ShapeDtypeStruct((n, e), f32), jax.ShapeDtypeStruct((n, C_SMALL - C_CMP), bf16)],
        compiler_params=_cparams("parallel"),
        name="proj",
    )(xb, w, b)


def _proj_t_kernel(wt_ref, x_ref, b_ref, o_ref):
    o_ref[...] = lax.dot_general(wt_ref[...], x_ref[...], (((1,), (1,)), ((), ())),
                                 preferred_element_type=f32) + b_ref[...]


def _proj_t(wt, xb, bcol, tn):
    e, k = wt.shape
    n = xb.shape[0]
    return pl.pallas_call(
        _proj_t_kernel,
        grid=(n // tn,),
        in_specs=[_full((e, k)), pl.BlockSpec((tn, k), lambda i: (i, 0)), _full((e, 1))],
        out_specs=pl.BlockSpec((e, tn), lambda i: (0, i)),
        out_shape=jax.ShapeDtypeStruct((e, n), f32),
        compiler_params=_cparams("parallel"),
        name="proj_t",
    )(wt, xb, bcol)


def _mlstm_kernel(q_ref, k_ref, v_ref, ao_ref, g_ref, gt_ref, ng_ref, c0_ref, n0_ref, m0_ref,
                  y_ref, c_out, n_out, m_out, c_s, n_s, m_s, *, L, valid):
    h = pl.program_id(1)
    c = pl.program_id(2)

    @pl.when(c == 0)
    def _():
        c_s[...] = c0_ref[0, 0]
        n_s[...] = n0_ref[0, 0]
        m_s[...] = m0_ref[0, 0]

    q = q_ref[...]
    k = k_ref[...] * (A_DQK ** -0.5)
    v = v_ref[...]
    g = g_ref[...]
    gt = gt_ref[0]
    lane = lax.broadcasted_iota(jnp.int32, g.shape, 1)
    sub = lax.broadcasted_iota(jnp.int32, gt.shape, 0)
    i_col = jnp.sum(jnp.where(lane == G_I + h, g, 0.0), axis=1, keepdims=True)
    f_col = jnp.sum(jnp.where(lane == G_F + h, g, 0.0), axis=1, keepdims=True)
    i_row = jnp.sum(jnp.where(sub == G_I + h, gt, 0.0), axis=0, keepdims=True)
    f_row = jnp.sum(jnp.where(sub == G_F + h, gt, 0.0), axis=0, keepdims=True)
    lf_col = jax.nn.log_sigmoid(f_col)
    lf_row = jax.nn.log_sigmoid(f_row)
    t_col = lax.broadcasted_iota(jnp.int32, (L, 1), 0)
    s_row = lax.broadcasted_iota(jnp.int32, (1, L), 1)
    if valid < L:
        lf_col = jnp.where(t_col < valid, lf_col, 0.0)
        lf_row = jnp.where(s_row < valid, lf_row, 0.0)
        i_col = jnp.where(t_col < valid, i_col, NEG)
        i_row = jnp.where(s_row < valid, i_row, NEG)
    tt = lax.broadcasted_iota(jnp.int32, (L, L), 0)
    ss = lax.broadcasted_iota(jnp.int32, (L, L), 1)
    causal = ss <= tt
    b_col = jnp.sum(jnp.where(causal, lf_row, 0.0), axis=1, keepdims=True)
    b_row = jnp.sum(jnp.where(tt <= ss, lf_col, 0.0), axis=0, keepdims=True)
    m_prev = m_s[...]
    cmat = c_s[...]
    n_row = n_s[...]

    d_log = jnp.where(causal, b_col - b_row + i_row, NEG)
    inter = b_col + m_prev
    m_t = jnp.maximum(inter, jnp.max(d_log, axis=1, keepdims=True))
    qb = q.astype(bf16)
    qk = lax.dot_general(qb, k.astype(bf16), (((1,), (1,)), ((), ())), preferred_element_type=f32)
    smat = qk * jnp.exp(d_log - m_t)
    w_inter = jnp.exp(inter - m_t)
    vb = v.astype(bf16)
    num = (w_inter * jnp.dot(qb, cmat.astype(bf16), preferred_element_type=f32)
           + jnp.dot(smat.astype(bf16), vb, preferred_element_type=f32))
    den = w_inter * jnp.sum(q * n_row, axis=1, keepdims=True) + jnp.sum(smat, axis=1, keepdims=True)
    hid = num / jnp.maximum(jnp.abs(den), jnp.exp(-m_t))
    mu = jnp.mean(hid, axis=1, keepdims=True)
    var = jnp.mean(jnp.square(hid - mu), axis=1, keepdims=True)
    hid = (hid - mu) * lax.rsqrt(var + LN_EPS) * ng_ref[...]
    y_ref[...] = hid * jax.nn.sigmoid(ao_ref[...])

    b_end = b_col[L - 1:L, :]
    g_row = b_end - b_row + i_row
    m_new = jnp.maximum(b_end + m_prev, jnp.max(g_row, axis=1, keepdims=True))
    a = jnp.exp(b_end + m_prev - m_new)
    w_col = jnp.exp(b_end - b_col + i_col - m_new)
    kw = k * w_col
    c_new = a * cmat + lax.dot_general(kw.astype(bf16), vb, (((0,), (0,)), ((), ())),
                                       preferred_element_type=f32)
    n_new = a * n_row + jnp.sum(kw, axis=0, keepdims=True)
    c_s[...] = c_new
    n_s[...] = n_new
    m_s[...] = m_new

    @pl.when(c == pl.num_programs(2) - 1)
    def _():
        c_out[0, 0] = c_new
        n_out[0, 0] = n_new
        m_out[0, 0] = m_new


def _mlstm(proj, gt, norm_g, c0, n0, m0, *, row0, nb, t, L, valid):
    nc = t // L
    rb0 = row0 // L
    gt = gt[:8, row0:row0 + nb * t].reshape(8, nb * nc, L).transpose(1, 0, 2)
    rows = lambda b, h, c: rb0 + b * nc + c
    st = lambda b, h, c: (b, h, 0, 0)
    y, c_f, n_f, m_f = pl.pallas_call(
        functools.partial(_mlstm_kernel, L=L, valid=valid),
        grid=(nb, A_HEADS, nc),
        in_specs=[
            pl.BlockSpec((L, A_DQK), lambda b, h, c: (rows(b, h, c), C_AQ // A_DQK + h)),
            pl.BlockSpec((L, A_DQK), lambda b, h, c: (rows(b, h, c), C_AK // A_DQK + h)),
            pl.BlockSpec((L, A_DV), lambda b, h, c: (rows(b, h, c), C_AV // A_DV + h)),
            pl.BlockSpec((L, A_DV), lambda b, h, c: (rows(b, h, c), C_AO // A_DV + h)),
            pl.BlockSpec((L, LANES), lambda b, h, c: (rows(b, h, c), C_SMALL // LANES)),
            pl.BlockSpec((1, 8, L), lambda b, h, c: (b * nc + c, 0, 0)),
            pl.BlockSpec((1, A_DV), lambda b, h, c: (0, h)),
            pl.BlockSpec((1, 1, A_DQK, A_DV), st),
            pl.BlockSpec((1, 1, 1, A_DQK), st),
            pl.BlockSpec((1, 1, 1, 1), st),
        ],
        out_specs=[
            pl.BlockSpec((L, A_DV), lambda b, h, c: (b * nc + c, h)),
            pl.BlockSpec((1, 1, A_DQK, A_DV), st),
            pl.BlockSpec((1, 1, 1, A_DQK), st),
            pl.BlockSpec((1, 1, 1, 1), st),
        ],
        out_shape=[
            jax.ShapeDtypeStruct((nb * t, A_HEADS * A_DV), f32),
            jax.ShapeDtypeStruct((nb, A_HEADS, A_DQK, A_DV), f32),
            jax.ShapeDtypeStruct((nb, A_HEADS, 1, A_DQK), f32),
            jax.ShapeDtypeStruct((nb, A_HEADS, 1, 1), f32),
        ],
        scratch_shapes=[pltpu.VMEM((A_DQK, A_DV), f32), pltpu.VMEM((1, A_DQK), f32), pltpu.VMEM((1, 1), f32)],
        compiler_params=_cparams("parallel", "parallel", "arbitrary"),
        name="mlstm",
    )(proj, proj, proj, proj, proj, gt, norm_g, c0, n0, m0)
    return y, c_f, n_f[:, :, 0], m_f[:, :, 0, 0]


def _compress_kernel(x_ref, pe_ref, w_ref, o_ref, xf_ref, *, nblk):
    for l in range(CMP_BLOCK):
        xf_ref[:, l * LANES:(l + 1) * LANES] = x_ref[pl.ds(l, nblk, stride=CMP_BLOCK), :]
    xf = (xf_ref[...] + pe_ref[0]).astype(bf16)
    o_ref[...] = jnp.dot(xf, w_ref[0], preferred_element_type=f32)


def _compress(x2, pe2, w2, *, rows, steps, row0, colblk):
    nblk = rows // CMP_BLOCK
    kflat = CMP_BLOCK * LANES
    rb0 = row0 // rows
    ngrp = KV_COLS // LANES
    return pl.pallas_call(
        functools.partial(_compress_kernel, nblk=nblk),
        grid=(steps, ngrp),
        in_specs=[pl.BlockSpec((rows, LANES), lambda s, p: (rb0 + s, colblk * ngrp + p)),
                  pl.BlockSpec((1, 1, kflat), lambda s, p: (p // 2, 0, 0)),
                  pl.BlockSpec((1, kflat, LANES), lambda s, p: (p // 2, 0, 0))],
        out_specs=pl.BlockSpec((nblk, LANES), lambda s, p: (s, p)),
        out_shape=jax.ShapeDtypeStruct((steps * nblk, KV_COLS), f32),
        scratch_shapes=[pltpu.VMEM((nblk, kflat), f32)],
        compiler_params=_cparams("parallel", "parallel"),
        name="compress",
    )(x2, pe2, w2)


def _gate_col(g, lane_idx):
    lane = lax.broadcasted_iota(jnp.int32, g.shape, 1)
    return jax.nn.sigmoid(jnp.sum(jnp.where(lane == lane_idx, g, 0.0), axis=1, keepdims=True))


def _cmp_topk_kernel(slope_ref, q_ref, kc_ref, vc_ref, g_ref, o_ref, sel_ref, *, tq, nblk, qpos0):
    gi = pl.program_id(1)
    i = pl.program_id(2)
    qpos = qpos0 + i * tq + lax.broadcasted_iota(jnp.int32, (tq, 1), 0)
    j = lax.broadcasted_iota(jnp.int32, (1, nblk), 1)
    dist = qpos - ((j + 1) * CMP_BLOCK - 1)
    valid = dist >= 0
    distf = dist.astype(f32)
    kc = kc_ref[0, 0].astype(bf16)
    vc = vc_ref[0, 0].astype(bf16)
    g = g_ref[...]
    imp = jnp.zeros((tq, nblk), f32)
    for r in range(B_REP):
        qr = q_ref[0, 0, 0, r * tq:(r + 1) * tq, :]
        s = lax.dot_general(qr, kc, (((1,), (1,)), ((), ())), preferred_element_type=f32) * (B_HD ** -0.5)
        s = s - slope_ref[gi * B_REP + r] * distf
        s = jnp.where(valid, s, NEG)
        e = jnp.exp(s - jnp.max(s, axis=1, keepdims=True))
        p = jnp.where(valid, e / jnp.sum(e, axis=1, keepdims=True), 0.0)
        imp = imp + p
        o = jnp.dot(p.astype(bf16), vc, preferred_element_type=f32)
        o_ref[0, 0, 0, r * tq:(r + 1) * tq, :] = o * _gate_col(g, G_GATE + gi * B_REP + r)
    cur = qpos // CMP_BLOCK
    imp = jnp.where((j == cur) | (j == 0), float(B_REP + 1), imp)
    imp = jnp.where(j > cur, -1.0, imp)
    jf = j.astype(f32)
    sel = jnp.zeros((tq, nblk), f32)
    for _ in range(N_SEL):
        mx = jnp.max(imp, axis=1, keepdims=True)
        idx = jnp.min(jnp.where(imp == mx, jf, float(nblk)), axis=1, keepdims=True)
        hit = jf == idx
        sel = jnp.where(hit, 1.0, sel)
        imp = jnp.where(hit, NEG, imp)
    sel_ref[0, 0] = sel


def _cmp_topk(slopes, qt, kc, vc, gsmall, *, tq, qpos0, row0):
    nb, _, nqt, _, _ = qt.shape
    nblk = kc.shape[2]
    rb0 = row0 // tq
    return pl.pallas_call(
        functools.partial(_cmp_topk_kernel, tq=tq, nblk=nblk, qpos0=qpos0),
        grid=(nb, B_KV, nqt),
        in_specs=[
            pl.BlockSpec(memory_space=pltpu.SMEM),
            pl.BlockSpec((1, 1, 1, B_REP * tq, B_HD), lambda b, g, i: (b, g, i, 0, 0)),
            pl.BlockSpec((1, 1, nblk, B_HD), lambda b, g, i: (b, g, 0, 0)),
            pl.BlockSpec((1, 1, nblk, B_HD), lambda b, g, i: (b, g, 0, 0)),
            pl.BlockSpec((tq, LANES), lambda b, g, i: (rb0 + b * nqt + i, 0)),
        ],
        out_specs=[
            pl.BlockSpec((1, 1, 1, B_REP * tq, B_HD), lambda b, g, i: (b, g, i, 0, 0)),
            pl.BlockSpec((1, 1, tq, nblk), lambda b, g, i: (b, g, i, 0)),
        ],
        out_shape=[
            jax.ShapeDtypeStruct(qt.shape, f32),
            jax.ShapeDtypeStruct((nb, B_KV, nqt * tq, nblk), f32),
        ],
        compiler_params=_cparams("parallel", "parallel", "parallel"),
        name="cmp_topk",
    )(slopes, qt, kc, vc, gsmall)


def _attn_kernel(slope_ref, q_ref, k_ref, v_ref, g_ref, *rest, tq, tk, nkt, nblk, qpos0, kpos0, window,
                 gate_lane):
    if window is None:
        sel_ref, o_ref, m_s, l_s, acc_s = rest
    else:
        o_ref, m_s, l_s, acc_s = rest
    gi = pl.program_id(1)
    i = pl.program_id(2)
    qlo = qpos0 + i * tq
    qpos = qlo + lax.broadcasted_iota(jnp.int32, (tq, 1), 0)
    m_s[...] = jnp.full(m_s.shape, NEG, f32)
    l_s[...] = jnp.zeros(l_s.shape, f32)
    acc_s[...] = jnp.zeros(acc_s.shape, f32)
    kt_hi = jnp.minimum((qlo + tq - 1 - kpos0) // tk + 1, nkt)
    if window is None:
        kt_lo = 0
        selb = sel_ref[0, 0].astype(bf16)
    else:
        kt_lo = jnp.maximum(qlo - (window - 1) - kpos0, 0) // tk

    def body(kt, carry):
        k0 = pl.multiple_of(kt * tk, tk)
        kb = k_ref[0, 0, pl.ds(k0, tk), :]
        vb = v_ref[0, 0, pl.ds(k0, tk), :]
        kidx = k0 + lax.broadcasted_iota(jnp.int32, (1, tk), 1)
        dist = qpos - (kpos0 + kidx)
        mask = dist >= 0
        if window is None:
            blk = lax.broadcasted_iota(jnp.int32, (nblk, tk), 0)
            kblk = (k0 + lax.broadcasted_iota(jnp.int32, (nblk, tk), 1)) // CMP_BLOCK
            expand = jnp.where(blk == kblk, 1.0, 0.0).astype(bf16)
            mask = mask & (jnp.dot(selb, expand, preferred_element_type=f32) > 0.5)
        else:
            mask = mask & (dist < window)
        distf = dist.astype(f32)
        for r in range(B_REP):
            rows = slice(r * tq, (r + 1) * tq)
            s = lax.dot_general(q_ref[0, 0, 0, rows, :], kb, (((1,), (1,)), ((), ())),
                                preferred_element_type=f32) * (B_HD ** -0.5)
            s = jnp.where(mask, s - slope_ref[gi * B_REP + r] * distf, NEG)
            m_old = m_s[rows, :]
            m_new = jnp.maximum(m_old, jnp.max(s, axis=1, keepdims=True))
            alpha = jnp.exp(m_old - m_new)
            p = jnp.exp(s - m_new)
            l_s[rows, :] = alpha * l_s[rows, :] + jnp.sum(p, axis=1, keepdims=True)
            acc_s[rows, :] = alpha * acc_s[rows, :] + jnp.dot(p.astype(bf16), vb, preferred_element_type=f32)
            m_s[rows, :] = m_new
        return carry

    lax.fori_loop(kt_lo, kt_hi, body, 0)
    g = g_ref[...]
    for r in range(B_REP):
        rows = slice(r * tq, (r + 1) * tq)
        o_ref[0, 0, 0, rows, :] = acc_s[rows, :] / l_s[rows, :] * _gate_col(g, gate_lane + gi * B_REP + r)


def _attn(slopes, qt, kh, vh, gsmall, sel, *, tq, tk, qpos0, kpos0, window, gate_lane, row0):
    nb, _, nqt, _, _ = qt.shape
    tkk = kh.shape[2]
    nkt = tkk // tk
    nblk = None if sel is None else sel.shape[3]
    rb0 = row0 // tq
    in_specs = [
        pl.BlockSpec(memory_space=pltpu.SMEM),
        pl.BlockSpec((1, 1, 1, B_REP * tq, B_HD), lambda b, g, i: (b, g, i, 0, 0)),
        pl.BlockSpec((1, 1, tkk, B_HD), lambda b, g, i: (b, g, 0, 0)),
        pl.BlockSpec((1, 1, tkk, B_HD), lambda b, g, i: (b, g, 0, 0)),
        pl.BlockSpec((tq, LANES), lambda b, g, i: (rb0 + b * nqt + i, 0)),
    ]
    args = [slopes, qt, kh, vh, gsmall]
    if sel is not None:
        in_specs.append(pl.BlockSpec((1, 1, tq, nblk), lambda b, g, i: (b, g, i, 0)))
        args.append(sel)
    return pl.pallas_call(
        functools.partial(_attn_kernel, tq=tq, tk=tk, nkt=nkt, nblk=nblk, qpos0=qpos0, kpos0=kpos0,
                          window=window, gate_lane=gate_lane),
        grid=(nb, B_KV, nqt),
        in_specs=in_specs,
        out_specs=pl.BlockSpec((1, 1, 1, B_REP * tq, B_HD), lambda b, g, i: (b, g, i, 0, 0)),
        out_shape=jax.ShapeDtypeStruct(qt.shape, f32),
        scratch_shapes=[pltpu.VMEM((B_REP * tq, 1), f32), pltpu.VMEM((B_REP * tq, 1), f32),
                        pltpu.VMEM((B_REP * tq, B_HD), f32)],
        compiler_params=_cparams("parallel", "parallel", "parallel"),
        name="attn_sel" if window is None else "attn_win",
    )(*args)


def _roll_lanes(x, shift):
    return x if shift == 0 else pltpu.roll(x, shift, axis=1)


def _softmax_rows(s, mask):
    s = jnp.where(mask, s, NEG)
    e = jnp.exp(s - jnp.max(s, axis=1, keepdims=True))
    return jnp.where(mask, e, 0.0), jnp.sum(e, axis=1, keepdims=True)


def _decode_kernel(slope_ref, q_ref, g_ref, kvc_ref, slc_ref, win_ref, exp_ref, oc_ref, os_ref, ow_ref, *,
                   tq, nblk, qpos0, win_kpos0):
    nrow = B_HEADS * tq
    q = q_ref[...]
    gs = g_ref[...]
    lane_grp = lax.broadcasted_iota(jnp.int32, (tq, B_KV * B_HD), 1) // B_HD
    qm = []
    for g in range(B_KV):
        qg = q[:, g * B_KV * B_HD:(g + 1) * B_KV * B_HD]
        for r in range(B_REP):
            qm.append(jnp.where(lane_grp == g, _roll_lanes(qg, ((g - r) % B_REP) * B_HD), 0.0))
    qm = jnp.concatenate(qm, axis=0).astype(bf16)
    row = lax.broadcasted_iota(jnp.int32, (nrow, 1), 0)
    qpos = qpos0 + row % tq
    slope = functools.reduce(lambda acc, h: jnp.where(row // tq == h, slope_ref[h], acc), range(B_HEADS),
                             jnp.zeros((nrow, 1), f32))
    scale = B_HD ** -0.5
    nt = (((1,), (1,)), ((), ()))

    def emit(o_ref, o, branch):
        for g in range(B_KV):
            acc = jnp.zeros((tq, B_KV * B_HD), f32)
            for r in range(B_REP):
                h = g * B_REP + r
                gate = jax.nn.sigmoid(gs[:, G_GATE + branch * B_HEADS + h:G_GATE + branch * B_HEADS + h + 1])
                oh = jnp.where(lane_grp == g, o[h * tq:(h + 1) * tq, :] * gate, 0.0)
                acc = acc + _roll_lanes(oh, ((r - g) % B_REP) * B_HD)
            o_ref[:, g * B_KV * B_HD:(g + 1) * B_KV * B_HD] = acc

    kvc = kvc_ref[0]
    j = lax.broadcasted_iota(jnp.int32, (1, nblk), 1)
    dist = qpos - ((j + 1) * CMP_BLOCK - 1)
    s = lax.dot_general(qm, kvc[:, :B_KV * B_HD].astype(bf16), nt, preferred_element_type=f32) * scale
    e, l = _softmax_rows(s - slope * dist.astype(f32), dist >= 0)
    p = e / l
    emit(oc_ref, jnp.dot(p.astype(bf16), kvc[:, B_KV * B_HD:].astype(bf16), preferred_element_type=f32), 0)
    imp = jnp.concatenate(
        [functools.reduce(lambda a, b: a + b, [p[(g * B_REP + r) * tq:(g * B_REP + r + 1) * tq] for r in range(B_REP)])
         for g in range(B_KV)], axis=0)
    cur = (qpos0 + lax.broadcasted_iota(jnp.int32, (B_KV * tq, 1), 0) % tq) // CMP_BLOCK
    imp = jnp.where((j == cur) | (j == 0), float(B_REP + 1), imp)
    imp = jnp.where(j > cur, -1.0, imp)
    jf = j.astype(f32)
    sel = jnp.zeros(imp.shape, f32)
    for _ in range(N_SEL):
        mx = jnp.max(imp, axis=1, keepdims=True)
        idx = jnp.min(jnp.where(imp == mx, jf, float(nblk)), axis=1, keepdims=True)
        hit = jf == idx
        sel = jnp.where(hit, 1.0, sel)
        imp = jnp.where(hit, NEG, imp)
    sel_rows = jnp.concatenate([sel[g * tq:(g + 1) * tq] for g in range(B_KV) for _ in range(B_REP)], axis=0)

    kv = slc_ref[0]
    tk = kv.shape[0]
    dist = qpos - lax.broadcasted_iota(jnp.int32, (1, tk), 1)
    picked = jnp.dot(sel_rows.astype(bf16), exp_ref[...], preferred_element_type=f32) > 0.5
    s = lax.dot_general(qm, kv[:, :B_KV * B_HD].astype(bf16), nt, preferred_element_type=f32) * scale
    e, l = _softmax_rows(s - slope * dist.astype(f32), picked & (dist >= 0))
    emit(os_ref, jnp.dot(e.astype(bf16), kv[:, B_KV * B_HD:].astype(bf16), preferred_element_type=f32) / l, 1)

    kv = win_ref[0]
    dist = qpos - (win_kpos0 + lax.broadcasted_iota(jnp.int32, (1, kv.shape[0]), 1))
    s = lax.dot_general(qm, kv[:, :B_KV * B_HD].astype(bf16), nt, preferred_element_type=f32) * scale
    e, l = _softmax_rows(s - slope * dist.astype(f32), (dist >= 0) & (dist < WINDOW))
    emit(ow_ref, jnp.dot(e.astype(bf16), kv[:, B_KV * B_HD:].astype(bf16), preferred_element_type=f32) / l, 2)


def _decode(slopes, proj, kvc, slc, win, expand, *, nb, tq, row0, qpos0, win_kpos0):
    nblk = kvc.shape[1]
    rb0 = row0 // tq
    out = pl.BlockSpec((tq, B_HEADS * B_HD), lambda b: (b, 0))
    return pl.pallas_call(
        functools.partial(_decode_kernel, tq=tq, nblk=nblk, qpos0=qpos0, win_kpos0=win_kpos0),
        grid=(nb,),
        in_specs=[pl.BlockSpec(memory_space=pltpu.SMEM),
                  pl.BlockSpec((tq, B_HEADS * B_HD), lambda b: (rb0 + b, C_BQ // (B_HEADS * B_HD))),
                  pl.BlockSpec((tq, LANES), lambda b: (rb0 + b, C_SMALL // LANES)),
                  pl.BlockSpec((1,) + kvc.shape[1:], lambda b: (b, 0, 0)),
                  pl.BlockSpec((1,) + slc.shape[1:], lambda b: (b, 0, 0)),
                  pl.BlockSpec((1,) + win.shape[1:], lambda b: (b, 0, 0)),
                  _full(expand.shape)],
        out_specs=[out, out, out],
        out_shape=[jax.ShapeDtypeStruct((nb * tq, B_HEADS * B_HD), f32)] * 3,
        compiler_params=_cparams("parallel"),
        name="nsa_decode",
    )(slopes, proj, proj, kvc, slc, win, expand)


GRP_LANES = B_KV * B_HD
_NT = (((1,), (1,)), ((), ()))


def _masked_queries(q, tq):
    lane_grp = lax.broadcasted_iota(jnp.int32, (tq, GRP_LANES), 1) // B_HD
    rows = []
    for g in range(B_KV):
        qg = q[:, g * GRP_LANES:(g + 1) * GRP_LANES]
        for r in range(B_REP):
            rows.append(jnp.where(lane_grp == g, _roll_lanes(qg, ((g - r) % B_REP) * B_HD), 0.0))
    return jnp.concatenate(rows, axis=0).astype(bf16)


def _group_columns(slope_ref, g, tq, qlo):
    row = lax.broadcasted_iota(jnp.int32, (B_REP * tq, 1), 0)
    slope = functools.reduce(lambda acc, r: jnp.where(row // tq == r, slope_ref[g * B_REP + r], acc), range(B_REP),
                             jnp.zeros((B_REP * tq, 1), f32))
    return qlo + row % tq, slope


def _emit_group(o_ref, og, gs, branch, g, tq):
    lane_grp = lax.broadcasted_iota(jnp.int32, (tq, GRP_LANES), 1) // B_HD
    acc = jnp.zeros((tq, GRP_LANES), f32)
    for r in range(B_REP):
        c = G_GATE + branch * B_HEADS + g * B_REP + r
        oh = jnp.where(lane_grp == g, og[r * tq:(r + 1) * tq, :] * jax.nn.sigmoid(gs[:, c:c + 1]), 0.0)
        acc = acc + _roll_lanes(oh, ((r - g) % B_REP) * B_HD)
    o_ref[:, g * GRP_LANES:(g + 1) * GRP_LANES] = acc


def _cmp_nat_kernel(slope_ref, q_ref, g_ref, kvc_ref, o_ref, sel_ref, *, tq, nblk):
    qlo = pl.program_id(1) * tq
    qm = _masked_queries(q_ref[...], tq)
    gs = g_ref[...]
    kvc = kvc_ref[0]
    kc = kvc[:, :GRP_LANES].astype(bf16)
    vc = kvc[:, GRP_LANES:].astype(bf16)
    j = lax.broadcasted_iota(jnp.int32, (1, nblk), 1)
    jf = j.astype(f32)
    for g in range(B_KV):
        qpos, slope = _group_columns(slope_ref, g, tq, qlo)
        dist = qpos - ((j + 1) * CMP_BLOCK - 1)
        s = lax.dot_general(qm[g * B_REP * tq:(g + 1) * B_REP * tq], kc, _NT, preferred_element_type=f32) * (B_HD ** -0.5)
        e, l = _softmax_rows(s - slope * dist.astype(f32), dist >= 0)
        p = e / l
        _emit_group(o_ref, jnp.dot(p.astype(bf16), vc, preferred_element_type=f32), gs, 0, g, tq)
        imp = functools.reduce(lambda a, b: a + b, [p[r * tq:(r + 1) * tq] for r in range(B_REP)])
        cur = qpos[:tq] // CMP_BLOCK
        imp = jnp.where((j == cur) | (j == 0), float(B_REP + 1), imp)
        imp = jnp.where(j > cur, -1.0, imp)
        sel = jnp.zeros(imp.shape, f32)
        for _ in range(N_SEL):
            mx = jnp.max(imp, axis=1, keepdims=True)
            idx = jnp.min(jnp.where(imp == mx, jf, float(nblk)), axis=1, keepdims=True)
            hit = jf == idx
            sel = jnp.where(hit, 1.0, sel)
            imp = jnp.where(hit, NEG, imp)
        sel_ref[0, 0, g * tq:(g + 1) * tq, :] = sel


def _cmp_nat(slopes, proj, kvc, *, nb, t, tq):
    nqt = t // tq
    nblk = kvc.shape[1]
    return pl.pallas_call(
        functools.partial(_cmp_nat_kernel, tq=tq, nblk=nblk),
        grid=(nb, nqt),
        in_specs=[pl.BlockSpec(memory_space=pltpu.SMEM),
                  pl.BlockSpec((tq, B_HEADS * B_HD), lambda b, i: (b * nqt + i, C_BQ // (B_HEADS * B_HD))),
                  pl.BlockSpec((tq, LANES), lambda b, i: (b * nqt + i, C_SMALL // LANES)),
                  pl.BlockSpec((1, nblk, KV_COLS), lambda b, i: (b, 0, 0))],
        out_specs=[pl.BlockSpec((tq, B_HEADS * B_HD), lambda b, i: (b * nqt + i, 0)),
                   pl.BlockSpec((1, 1, B_KV * tq, nblk), lambda b, i: (b, i, 0, 0))],
        out_shape=[jax.ShapeDtypeStruct((nb * t, B_HEADS * B_HD), f32),
                   jax.ShapeDtypeStruct((nb, nqt, B_KV * tq, nblk), f32)],
        compiler_params=_cparams("parallel", "parallel"),
        name="nsa_cmp",
    )(slopes, proj, proj, kvc)


def _sel_nat_kernel(slope_ref, q_ref, g_ref, kv_ref, sel_ref, o_ref, qm_s, m_s, l_s, acc_s, *, tq, tk, nblk):
    qlo = pl.program_id(1) * tq
    qm_s[...] = _masked_queries(q_ref[...], tq)
    m_s[...] = jnp.full(m_s.shape, NEG, f32)
    l_s[...] = jnp.zeros(l_s.shape, f32)
    acc_s[...] = jnp.zeros(acc_s.shape, f32)
    selb = sel_ref[0, 0].astype(bf16)
    grows = B_REP * tq

    def body(kt, carry):
        k0 = pl.multiple_of(kt * tk, tk)
        kb = kv_ref[pl.ds(k0, tk), :GRP_LANES]
        vb = kv_ref[pl.ds(k0, tk), GRP_LANES:]
        kidx = k0 + lax.broadcasted_iota(jnp.int32, (1, tk), 1)
        blk = lax.broadcasted_iota(jnp.int32, (nblk, tk), 0)
        kblk = (k0 + lax.broadcasted_iota(jnp.int32, (nblk, tk), 1)) // CMP_BLOCK
        picked = jnp.dot(selb, jnp.where(blk == kblk, 1.0, 0.0).astype(bf16), preferred_element_type=f32) > 0.5
        for g in range(B_KV):
            rows = slice(g * grows, (g + 1) * grows)
            qpos, slope = _group_columns(slope_ref, g, tq, qlo)
            dist = qpos - kidx
            mask = jnp.concatenate([picked[g * tq:(g + 1) * tq]] * B_REP, axis=0) & (dist >= 0)
            s = lax.dot_general(qm_s[rows, :], kb, _NT, preferred_element_type=f32) * (B_HD ** -0.5)
            s = jnp.where(mask, s - slope * dist.astype(f32), NEG)
            m_old = m_s[rows, :]
            m_new = jnp.maximum(m_old, jnp.max(s, axis=1, keepdims=True))
            alpha = jnp.exp(m_old - m_new)
            p = jnp.exp(s - m_new)
            l_s[rows, :] = alpha * l_s[rows, :] + jnp.sum(p, axis=1, keepdims=True)
            acc_s[rows, :] = alpha * acc_s[rows, :] + jnp.dot(p.astype(bf16), vb, preferred_element_type=f32)
            m_s[rows, :] = m_new
        return carry

    lax.fori_loop(0, (qlo + tq - 1) // tk + 1, body, 0)
    gs = g_ref[...]
    for g in range(B_KV):
        rows = slice(g * grows, (g + 1) * grows)
        _emit_group(o_ref, acc_s[rows, :] / l_s[rows, :], gs, 1, g, tq)


def _win_nat_kernel(slope_ref, q_ref, g_ref, kv_ref, o_ref, *, tq, t):
    qlo = pl.program_id(1) * tq
    span = WINDOW + tq
    k0 = pl.multiple_of(jnp.clip(qlo - WINDOW, 0, t - span), LANES)
    qm = _masked_queries(q_ref[...], tq)
    gs = g_ref[...]
    kb = kv_ref[pl.ds(k0, span), :GRP_LANES]
    vb = kv_ref[pl.ds(k0, span), GRP_LANES:]
    kidx = k0 + lax.broadcasted_iota(jnp.int32, (1, span), 1)
    for g in range(B_KV):
        qpos, slope = _group_columns(slope_ref, g, tq, qlo)
        dist = qpos - kidx
        s = lax.dot_general(qm[g * B_REP * tq:(g + 1) * B_REP * tq], kb, _NT, preferred_element_type=f32) * (B_HD ** -0.5)
        e, l = _softmax_rows(s - slope * dist.astype(f32), (dist >= 0) & (dist < WINDOW))
        _emit_group(o_ref, jnp.dot(e.astype(bf16), vb, preferred_element_type=f32) / l, gs, 2, g, tq)


def _attn_nat(slopes, proj, kvb, sel, *, nb, t, tq, tk, branch):
    nqt = t // tq
    in_specs = [pl.BlockSpec(memory_space=pltpu.SMEM),
                pl.BlockSpec((tq, B_HEADS * B_HD), lambda b, i: (b * nqt + i, C_BQ // (B_HEADS * B_HD))),
                pl.BlockSpec((tq, LANES), lambda b, i: (b * nqt + i, C_SMALL // LANES)),
                pl.BlockSpec((t, KV_COLS), lambda b, i: (b, branch))]
    args = [slopes, proj, proj, kvb]
    if sel is None:
        body, scratch, name = functools.partial(_win_nat_kernel, tq=tq, t=t), [], "nsa_win"
    else:
        nblk = sel.shape[3]
        in_specs.append(pl.BlockSpec((1, 1, B_KV * tq, nblk), lambda b, i: (b, i, 0, 0)))
        args.append(sel)
        body = functools.partial(_sel_nat_kernel, tq=tq, tk=tk, nblk=nblk)
        scratch = [pltpu.VMEM((B_HEADS * tq, GRP_LANES), bf16), pltpu.VMEM((B_HEADS * tq, 1), f32),
                   pltpu.VMEM((B_HEADS * tq, 1), f32), pltpu.VMEM((B_HEADS * tq, GRP_LANES), f32)]
        name = "nsa_sel"
    return pl.pallas_call(
        body,
        grid=(nb, nqt),
        in_specs=in_specs,
        out_specs=pl.BlockSpec((tq, B_HEADS * B_HD), lambda b, i: (b * nqt + i, 0)),
        out_shape=jax.ShapeDtypeStruct((nb * t, B_HEADS * B_HD), f32),
        scratch_shapes=scratch,
        compiler_params=_cparams("parallel", "parallel"),
        name=name,
    )(*args)


PAD_PAGES = 4


def _gather_kernel(pt_ref, *refs):
    del pt_ref
    pages, tail_ref, o_ref = refs[:-2], refs[-2], refs[-1]
    for u, page in enumerate(pages):
        o_ref[0, u] = page[0]
    o_ref[0, len(pages)] = tail_ref[0]
    for u in range(len(pages) + 1, len(pages) + PAD_PAGES):
        o_ref[0, u] = jnp.zeros(o_ref.shape[2:], f32)


def _gather_pages(page_table, cache, tail):
    nb, n_pages = page_table.shape
    page = (1, PAGE_SIZE, KV_COLS)
    return pl.pallas_call(
        _gather_kernel,
        grid_spec=pltpu.PrefetchScalarGridSpec(
            num_scalar_prefetch=1,
            grid=(nb,),
            in_specs=[pl.BlockSpec(page, functools.partial(lambda b, pt, u: (pt[b, u], 0, 0), u=u))
                      for u in range(n_pages)] + [pl.BlockSpec(page, lambda b, pt: (b, 0, 0))],
            out_specs=pl.BlockSpec((1, n_pages + PAD_PAGES, PAGE_SIZE, KV_COLS), lambda b, pt: (b, 0, 0, 0)),
        ),
        out_shape=jax.ShapeDtypeStruct((nb, n_pages + PAD_PAGES, PAGE_SIZE, KV_COLS), f32),
        compiler_params=_cparams("parallel"),
        name="gather_pages",
    )(page_table, *([cache] * n_pages), tail)


def _layer_norm(z, g, b):
    mu = jnp.mean(z, axis=1, keepdims=True)
    var = jnp.mean(jnp.square(z - mu), axis=1, keepdims=True)
    return (z - mu) * lax.rsqrt(var + LN_EPS) * g + b


def _tail_kernel(x_ref, ya_ref, oc_ref, os_ref, ow_ref, wm_ref, wa_ref, wb_ref, wo_ref, g_ref, b_ref,
                 h_ref, hb_ref):
    x = x_ref[...]
    gates = jax.nn.sigmoid(jnp.dot(x.astype(bf16), wm_ref[...], preferred_element_type=f32))
    yb = oc_ref[...] + os_ref[...] + ow_ref[...]
    ma = jnp.dot(ya_ref[...].astype(bf16), wa_ref[...], preferred_element_type=f32)
    mb = jnp.dot(yb.astype(bf16), wb_ref[...], preferred_element_type=f32)
    merged = gates[:, :D_MODEL] * ma + gates[:, D_MODEL:] * mb
    z = DN_ALPHA * x + jnp.dot(merged.astype(bf16), wo_ref[...], preferred_element_type=f32)
    h = _layer_norm(z, g_ref[...], b_ref[...])
    h_ref[...] = h
    hb_ref[...] = h.astype(bf16)


def _tail(x, ya, oc, os_, ow, wm, wa, wb, wo, g, b, tm):
    n = x.shape[0]
    row = pl.BlockSpec((tm, D_MODEL), lambda i: (i, 0))
    return pl.pallas_call(
        _tail_kernel,
        grid=(n // tm,),
        in_specs=[row] * 5 + [_full(wm.shape), _full(wa.shape), _full(wb.shape), _full(wo.shape),
                              _full(g.shape), _full(b.shape)],
        out_specs=[row, row],
        out_shape=[jax.ShapeDtypeStruct((n, D_MODEL), f32), jax.ShapeDtypeStruct((n, D_MODEL), bf16)],
        compiler_params=_cparams("parallel"),
        name="tail",
    )(x, ya, oc, os_, ow, wm, wa, wb, wo, g, b)


def _top16(x):
    kk, tb = x.shape
    ji = lax.broadcasted_iota(jnp.int32, (kk, tb), 0).astype(f32)
    rank = jnp.full((kk, tb), float(P_TOPK), f32)
    vals = []
    for k in range(P_TOPK):
        mx = jnp.max(x, axis=0, keepdims=True)
        idx = jnp.min(jnp.where(x == mx, ji, float(kk)), axis=0, keepdims=True)
        hit = ji == idx
        rank = jnp.where(hit, float(k), rank)
        vals.append(mx)
        x = jnp.where(hit, NEG, x)
    return rank, vals


_CAND_ROWS8 = ((1, 8), (2, 5), (3, 4), (4, 3))


def _route_kernel(h_ref, wqt_ref, keys_ref, ta_ref, tb_ref):
    qpt = lax.dot_general(wqt_ref[...], h_ref[...], (((1,), (1,)), ((), ())), preferred_element_type=f32)
    tb = qpt.shape[1]
    sub16 = lax.broadcasted_iota(jnp.int32, (P_TOPK, tb), 0)
    sub8 = lax.broadcasted_iota(jnp.int32, (8, tb), 0)
    for p in range(P_HEADS):
        sc, rk, vl = [], [], []
        for c in range(2):
            qs = qpt[(2 * p + c) * P_DHALF:(2 * p + c + 1) * P_DHALF, :].astype(bf16)
            s = jnp.dot(keys_ref[p, c], qs, preferred_element_type=f32)
            r, v = _top16(s)
            sc.append(s)
            rk.append(r)
            vl.append(v)
        v0, v1 = vl
        col0 = functools.reduce(lambda acc, k: jnp.where(sub16 == k, v0[k], acc), range(P_TOPK), jnp.zeros((P_TOPK, tb), f32))
        col1 = functools.reduce(lambda acc, k: jnp.where(sub16 == k, v1[k], acc), range(P_TOPK), jnp.zeros((P_TOPK, tb), f32))
        segs = [v0[0] + col1]
        for k1, keep in _CAND_ROWS8:
            segs.append(jnp.where(sub8 < keep, v0[k1] + col1[0:8], NEG))
        first = jnp.where(sub8 < 2, v0[5], jnp.where(sub8 < 4, v0[6], v0[7]))
        second = jnp.where(sub8 % 2 == 0, v1[0], v1[1])
        segs.append(jnp.where(sub8 < 6, first + second, NEG))
        segs.append(col0[8:16] + v1[0])
        cand = jnp.concatenate(segs, axis=0)
        crank, cvals = _top16(cand)
        taken = jnp.where(crank < float(P_TOPK), 1.0, 0.0)
        z = functools.reduce(lambda acc, v: acc + jnp.exp(v - cvals[0]), cvals, jnp.zeros((1, tb), f32))
        cnt = [jnp.sum(taken[0:16], axis=0, keepdims=True)]
        for i in range(len(_CAND_ROWS8)):
            cnt.append(jnp.sum(taken[16 + 8 * i:24 + 8 * i], axis=0, keepdims=True))
        t5 = taken[48:56]
        for lo in (0, 2, 4):
            cnt.append(jnp.sum(jnp.where((sub8 >= lo) & (sub8 < lo + 2), t5, 0.0), axis=0, keepdims=True))
        for i in range(8):
            cnt.append(taken[56 + i:57 + i])
        n_a = functools.reduce(lambda acc, k: jnp.where(rk[0] == float(k), cnt[k], acc), range(P_TOPK),
                               jnp.zeros((P_NKEYS, tb), f32))
        ta_ref[p, 0] = n_a
        ta_ref[p, 1] = jnp.exp(sc[0] - v0[0])
        tb_ref[p, 0] = rk[1].astype(bf16)
        tb_ref[p, 1] = (jnp.exp(sc[1] - v1[0]) / z).astype(bf16)


def _route(hb, wqt, keys, tb):
    n = hb.shape[0]
    spec = pl.BlockSpec((P_HEADS, 2, P_NKEYS, tb), lambda i: (0, 0, 0, i))
    return pl.pallas_call(
        _route_kernel,
        grid=(n // tb,),
        in_specs=[pl.BlockSpec((tb, D_MODEL), lambda i: (i, 0)), _full(wqt.shape), _full(keys.shape)],
        out_specs=[spec, spec],
        out_shape=[jax.ShapeDtypeStruct((P_HEADS, 2, P_NKEYS, n), f32),
                   jax.ShapeDtypeStruct((P_HEADS, 2, P_NKEYS, n), bf16)],
        compiler_params=_cparams("parallel"),
        name="peer_route",
    )(hb, wqt, keys)


def _experts_kernel(hb_ref, h_ref, ta_ref, tb_ref, u_ref, vt_ref, g_ref, b_ref, y_ref, acc_s, ht_s, pt_s, *, te):
    j = pl.program_id(1)

    @pl.when(j == 0)
    def _():
        acc_s[...] = jnp.zeros(acc_s.shape, f32)

    ht_s[...] = lax.dot_general(u_ref[...], hb_ref[...], (((1,), (1,)), ((), ())), preferred_element_type=f32)
    for aa in range(te // P_NKEYS):
        a = j * (te // P_NKEYS) + aa
        n_rows = [ta_ref[p, 0, pl.ds(a, 1), :].astype(bf16) for p in range(P_HEADS)]
        e0_rows = [ta_ref[p, 1, pl.ds(a, 1), :].astype(bf16) for p in range(P_HEADS)]
        for lt in range(ht_s.shape[1] // LANES):
            ls = slice(lt * LANES, (lt + 1) * LANES)
            w = jnp.zeros((P_NKEYS, LANES), bf16)
            for p in range(P_HEADS):
                w = w + jnp.where(tb_ref[p, 0, :, ls] < n_rows[p][:, ls], e0_rows[p][:, ls] * tb_ref[p, 1, :, ls], 0.0)
            hs = ht_s[aa * P_NKEYS:(aa + 1) * P_NKEYS, ls]
            act = 0.5 * hs * (1.0 + lax.erf(hs * (0.5 ** 0.5)))
            pt_s[aa * P_NKEYS:(aa + 1) * P_NKEYS, ls] = w * act.astype(bf16)
    acc_s[...] += jnp.dot(vt_ref[...], pt_s[...], preferred_element_type=f32)

    @pl.when(j == pl.num_programs(1) - 1)
    def _():
        z = DN_ALPHA * h_ref[...] + acc_s[...].T
        y_ref[...] = _layer_norm(z, g_ref[...], b_ref[...])


def _experts(hb, h, ta, tbl, u, vt, g, b, tb, te):
    n = hb.shape[0]
    row = pl.BlockSpec((tb, D_MODEL), lambda i, j: (i, 0))
    tab = pl.BlockSpec((P_HEADS, 2, P_NKEYS, tb), lambda i, j: (0, 0, 0, i))
    return pl.pallas_call(
        functools.partial(_experts_kernel, te=te),
        grid=(n // tb, P_EXPERTS // te),
        in_specs=[row, row, tab, tab,
                  pl.BlockSpec((te, D_MODEL), lambda i, j: (j, 0)),
                  pl.BlockSpec((D_MODEL, te), lambda i, j: (0, j)),
                  pl.BlockSpec((1, D_MODEL), lambda i, j: (0, 0)),
                  pl.BlockSpec((1, D_MODEL), lambda i, j: (0, 0))],
        out_specs=row,
        out_shape=jax.ShapeDtypeStruct((n, D_MODEL), f32),
        scratch_shapes=[pltpu.VMEM((D_MODEL, tb), f32), pltpu.VMEM((te, tb), f32), pltpu.VMEM((te, tb), bf16)],
        compiler_params=_cparams("parallel", "arbitrary"),
        name="peer_experts",
    )(hb, h, ta, tbl, u, vt, g, b)


def _to_q_tiles(q2, nb, t, tq):
    q = q2.reshape(nb, t // tq, tq, B_KV, B_REP, B_HD).transpose(0, 3, 1, 4, 2, 5)
    return q.reshape(nb, B_KV, t // tq, B_REP * tq, B_HD).astype(bf16)


def _from_q_tiles(o, nb, t, tq):
    o = o.reshape(nb, B_KV, t // tq, B_REP, tq, B_HD).transpose(0, 2, 4, 1, 3, 5)
    return o.reshape(nb, t, B_HEADS * B_HD)


def _kv_heads(kv3):
    nb, tk, _ = kv3.shape
    kv = kv3.reshape(nb, tk, 2, B_KV, B_HD).transpose(2, 0, 3, 1, 4)
    return kv[0], kv[1]


def _nsa(slopes, proj, gsmall, pe2, w2, *, nb, t, tq, row0, qpos0, cmp_src, slc3, win3, win_kpos0, tk_sel, tk_win):
    qt = _to_q_tiles(proj[row0:row0 + nb * t, C_BQ:C_CMP], nb, t, tq)
    x2, cmp_rows, cmp_steps, cmp_colblk = cmp_src
    kvc = _compress(x2, pe2, w2, rows=cmp_rows, steps=cmp_steps, row0=0, colblk=cmp_colblk)
    kc, vc = _kv_heads(kvc.reshape(nb, -1, KV_COLS))
    o_cmp, sel = _cmp_topk(slopes, qt, kc, vc, gsmall, tq=tq, qpos0=qpos0, row0=row0)
    ks, vs = _kv_heads(slc3)
    o_slc = _attn(slopes, qt, ks.astype(bf16), vs.astype(bf16), gsmall, sel, tq=tq, tk=tk_sel, qpos0=qpos0,
                  kpos0=0, window=None, gate_lane=G_GATE + B_HEADS, row0=row0)
    kw, vw = _kv_heads(win3)
    o_win = _attn(slopes, qt, kw.astype(bf16), vw.astype(bf16), gsmall, None, tq=tq, tk=tk_win, qpos0=qpos0,
                  kpos0=win_kpos0, window=WINDOW, gate_lane=G_GATE + 2 * B_HEADS, row0=row0)
    return tuple(_from_q_tiles(o, nb, t, tq).reshape(nb * t, B_HEADS * B_HD) for o in (o_cmp, o_slc, o_win))


def kernel(x_prompt, x_sample, cache_cmp_kv, cache_slc_kv, cache_win_kv, state_C, state_n, state_m, page_table,
           w_in, b_in, norm_a_g, nsa_pe, nsa_w_cmp, w_br_a, w_br_b, w_merge, w_out, ln1_g, ln1_b,
           peer_wq, peer_keys, peer_u, peer_v, ln2_g, ln2_b):
    bp, tp, _ = x_prompt.shape
    bs, ts, _ = x_sample.shape
    tsp = 8
    n_p, n_s = bp * tp, bs * tsp
    past = page_table.shape[1] * PAGE_SIZE

    perm = np.concatenate([np.arange(0, 2048), np.arange(2056, 5640), np.arange(2048, 2056), np.arange(5640, 5688)])
    w_perm = jnp.pad(w_in[:, perm], ((0, 0), (0, C_END - perm.size)))
    b_perm = jnp.pad(b_in[perm], (0, C_END - perm.size))
    w_perm_b = w_perm.astype(bf16)
    slopes = jnp.asarray(2.0 ** (-8.0 * np.arange(1, B_HEADS + 1) / B_HEADS), f32)
    wc = nsa_w_cmp.reshape(2, CMP_BLOCK, 1, B_HD, 1, B_HD)
    eye2 = jnp.eye(2, dtype=f32).reshape(1, 1, 2, 1, 2, 1)
    w2 = (wc * eye2).reshape(2, CMP_BLOCK * LANES, LANES).astype(bf16)
    pe2 = jnp.tile(nsa_pe, (1, 1, 2)).reshape(2, 1, CMP_BLOCK * LANES)

    xs_pad = jnp.pad(x_sample, ((0, 0), (0, tsp - ts), (0, 0)))
    x_all = jnp.concatenate([x_prompt.reshape(n_p, D_MODEL), xs_pad.reshape(n_s, D_MODEL)], axis=0)
    xb = x_all.astype(bf16)
    proj, kvb = _proj(xb, w_perm_b, b_perm.reshape(1, C_END), 256)
    gt = _proj_t(w_perm_b[:, C_SMALL:].T, xb, b_perm[C_SMALL:].reshape(LANES, 1), 512)

    zc = jnp.zeros((bp, A_HEADS, A_DQK, A_DV), f32)
    zn = jnp.zeros((bp, A_HEADS, 1, A_DQK), f32)
    zm = jnp.zeros((bp, A_HEADS, 1, 1), f32)
    ng = norm_a_g.reshape(1, A_HEADS * A_DV)
    ya_p, p_c, p_n, p_m = _mlstm(proj, gt, ng, zc, zn, zm, row0=0, nb=bp, t=tp, L=256, valid=256)
    ya_s, s_c, s_n, s_m = _mlstm(proj, gt, ng, state_C, state_n.reshape(bs, A_HEADS, 1, A_DQK),
                                 state_m.reshape(bs, A_HEADS, 1, 1), row0=n_p, nb=bs, t=tsp, L=tsp, valid=ts)

    kv_p = proj[:n_p, C_CMP:C_SMALL].reshape(bp, tp, 3, KV_COLS)
    p_cmp, p_slc, p_winrows = kv_p[:, :, 0], kv_p[:, :, 1], kv_p[:, :, 2]
    kvc_p = _compress(proj, pe2, w2, rows=n_p, steps=1, row0=0, colblk=C_CMP // KV_COLS)
    oc_p, sel_p = _cmp_nat(slopes, proj, kvc_p.reshape(bp, tp // CMP_BLOCK, KV_COLS), nb=bp, t=tp, tq=128)
    os_p = _attn_nat(slopes, proj, kvb, sel_p, nb=bp, t=tp, tq=128, tk=512, branch=1)
    ow_p = _attn_nat(slopes, proj, kvb, None, nb=bp, t=tp, tq=128, tk=512, branch=2)
    ob_p = (oc_p, os_p, ow_p)

    kv_s = proj[n_p:, C_CMP:C_SMALL].reshape(bs, tsp, 3, KV_COLS)[:, :ts]
    s_cmp, s_slc, s_winrows = kv_s[:, :, 0], kv_s[:, :, 1], kv_s[:, :, 2]
    n_pool = cache_cmp_kv.shape[0]
    tail_pad = ((0, 0), (0, PAGE_SIZE - ts), (0, 0))
    all_cmp = _gather_pages(page_table, cache_cmp_kv.reshape(n_pool, PAGE_SIZE, KV_COLS), jnp.pad(s_cmp, tail_pad))
    all_slc = _gather_pages(page_table, cache_slc_kv.reshape(n_pool, PAGE_SIZE, KV_COLS), jnp.pad(s_slc, tail_pad))
    tk_s = past + PAD_PAGES * PAGE_SIZE
    seqs_per_step = 8
    all_cmp = all_cmp.reshape(bs * tk_s, KV_COLS)
    all_slc = all_slc.reshape(bs, tk_s, KV_COLS)
    wb = cache_win_kv.shape[1]
    buf = jnp.concatenate([cache_win_kv.reshape(bs, wb, KV_COLS), s_winrows], axis=1)
    s_win = buf[:, -wb:]
    win_tk = 640
    buf_pad = jnp.pad(buf, ((0, 0), (0, win_tk - buf.shape[1]), (0, 0)))
    kvc_s = _compress(all_cmp, pe2, w2, rows=seqs_per_step * tk_s, steps=bs // seqs_per_step, row0=0, colblk=0)
    expand = jnp.asarray(np.arange(tk_s)[None, :] // CMP_BLOCK == np.arange(tk_s // CMP_BLOCK)[:, None], bf16)
    ob_s = _decode(slopes, proj, kvc_s.reshape(bs, tk_s // CMP_BLOCK, KV_COLS), all_slc, buf_pad, expand,
                   nb=bs, tq=tsp, row0=n_p, qpos0=past, win_kpos0=past - wb)

    ya = jnp.concatenate([ya_p, ya_s], axis=0)
    oc, os_, ow = (jnp.concatenate([a, b], axis=0) for a, b in zip(ob_p, ob_s))
    h1, h1b = _tail(x_all, ya, oc, os_, ow, w_merge.astype(bf16), w_br_a.astype(bf16), w_br_b.astype(bf16),
                    w_out.astype(bf16), ln1_g.reshape(1, D_MODEL), ln1_b.reshape(1, D_MODEL), 256)
    tab_a, tab_b = _route(h1b, peer_wq.T.astype(bf16), peer_keys.astype(bf16), 256)
    y = _experts(h1b, h1, tab_a, tab_b, peer_u.astype(bf16), peer_v.T.astype(bf16), ln2_g.reshape(1, D_MODEL),
                 ln2_b.reshape(1, D_MODEL), 512, 512)

    y_prompt = y[:n_p].reshape(bp, tp, D_MODEL)
    y_sample = y[n_p:].reshape(bs, tsp, D_MODEL)[:, :ts]
    kv5 = lambda a: a.reshape(a.shape[0], a.shape[1], 2, B_KV, B_HD)
    dt = x_prompt.dtype
    return (y_prompt, y_sample, kv5(p_cmp), kv5(p_slc), kv5(p_winrows[:, -min(WINDOW, tp):]),
            p_c.astype(dt), p_n.astype(dt), p_m.astype(dt),
            kv5(s_cmp), kv5(s_slc), kv5(s_win), s_c.astype(state_C.dtype), s_n.astype(state_C.dtype),
            s_m.astype(state_C.dtype))
```

```python
import functools

import jax
import jax.numpy as jnp
import numpy as np
from jax import lax
from jax.experimental import pallas as pl
from jax.experimental.pallas import tpu as pltpu

D_MODEL = 1024
A_HEADS, A_DQK, A_DV = 4, 128, 256
B_HEADS, B_KV, B_HD = 16, 4, 64
B_REP = B_HEADS // B_KV
CMP_BLOCK = 64
N_SEL = 16
WINDOW = 512
PAGE_SIZE = 128
P_HEADS, P_NKEYS, P_DHALF, P_TOPK = 8, 128, 128, 16
P_EXPERTS = P_NKEYS * P_NKEYS
DN_ALPHA = 2.0 ** 0.25
LN_EPS = 1e-5
NEG = -1e30

LANES = 128
KV_COLS = 2 * B_KV * B_HD
VMEM_LIMIT = 56 * 1024 * 1024

C_AQ, C_AK, C_AV, C_AO, C_BQ, C_CMP, C_SLC, C_WIN, C_SMALL, C_END = (
    0, 512, 1024, 2048, 3072, 4096, 4608, 5120, 5632, 5760)
G_I, G_F, G_GATE = 0, A_HEADS, 2 * A_HEADS

bf16 = jnp.bfloat16
f32 = jnp.float32
_NT_DIMS = (((1,), (1,)), ((), ()))


def _cparams(*sem):
    return pltpu.CompilerParams(dimension_semantics=sem, vmem_limit_bytes=VMEM_LIMIT)


def _full(shape):
    nd = len(shape)
    return pl.BlockSpec(shape, lambda *_: (0,) * nd)


def _proj_kernel(x_ref, w_ref, b_ref, o_ref, kvb_ref):
    res = jnp.dot(x_ref[...], w_ref[...], preferred_element_type=f32) + b_ref[...]
    o_ref[...] = res
    kvb_ref[...] = res[:, C_CMP:C_SMALL].astype(bf16)


def _proj(xb, w, b, tm):
    n, k = xb.shape
    e = w.shape[1]
    return pl.pallas_call(
        _proj_kernel,
        grid=(n // tm,),
        in_specs=[pl.BlockSpec((tm, k), lambda i: (i, 0)), _full((k, e)), _full((1, e))],
        out_specs=[pl.BlockSpec((tm, e), lambda i: (i, 0)), pl.BlockSpec((tm, C_SMALL - C_CMP), lambda i: (i, 0))],
        out_shape=[jax.ShapeDtypeStruct((n, e), f32), jax.ShapeDtypeStruct((n, C_SMALL - C_CMP), bf16)],
        compiler_params=_cparams("parallel"),
        name="proj",
    )(xb, w, b)


def _proj_t_kernel(wt_ref, x_ref, b_ref, o_ref):
    o_ref[...] = lax.dot_general(wt_ref[...], x_ref[...], (((1,), (1,)), ((), ())),
                                 preferred_element_type=f32) + b_ref[...]


def _proj_t(wt, xb, bcol, tn):
    e, k = wt.shape
    n = xb.shape[0]
    return pl.pallas_call(
        _proj_t_kernel,
        grid=(n // tn,),
        in_specs=[_full((e, k)), pl.BlockSpec((tn, k), lambda i: (i, 0)), _full((e, 1))],
        out_specs=pl.BlockSpec((e, tn), lambda i: (0, i)),
        out_shape=jax.ShapeDtypeStruct((e, n), f32),
        compiler_params=_cparams("parallel"),
        name="proj_t",
    )(wt, xb, bcol)


def _proj_kvt_kernel(wt_ref, x_ref, b_ref, oc_ref, os_ref, ow_ref):
    res = lax.dot_general(wt_ref[...], x_ref[...], _NT_DIMS, preferred_element_type=f32) + b_ref[...]
    for i, o_ref in enumerate((oc_ref, os_ref, ow_ref)):
        o_ref[0] = res[i * KV_COLS:(i + 1) * KV_COLS]


def _proj_kvt(wt, xb, bcol, *, row0, nb, t, tn):
    e, k = wt.shape
    nt = t // tn
    rb0 = row0 // tn
    out = pl.BlockSpec((1, KV_COLS, tn), lambda b, i: (b, 0, i))
    return pl.pallas_call(
        _proj_kvt_kernel,
        grid=(nb, nt),
        in_specs=[_full((e, k)), pl.BlockSpec((tn, k), lambda b, i: (rb0 + b * nt + i, 0)), _full((e, 1))],
        out_specs=[out, out, out],
        out_shape=[jax.ShapeDtypeStruct((nb, KV_COLS, t), f32)] * 3,
        compiler_params=_cparams("parallel", "parallel"),
        name="proj_kvt",
    )(wt, xb, bcol)


def _mlstm_kernel(q_ref, k_ref, v_ref, ao_ref, g_ref, gt_ref, ng_ref, c0_ref, n0_ref, m0_ref,
                  y_ref, c_out, n_out, m_out, c_s, n_s, m_s, *, L, valid):
    h = pl.program_id(1)
    c = pl.program_id(2)

    @pl.when(c == 0)
    def _():
        c_s[...] = c0_ref[0, 0]
        n_s[...] = n0_ref[0, 0]
        m_s[...] = m0_ref[0, 0]

    q = q_ref[...]
    k = k_ref[...] * (A_DQK ** -0.5)
    v = v_ref[...]
    g = g_ref[...]
    gt = gt_ref[0]
    lane = lax.broadcasted_iota(jnp.int32, g.shape, 1)
    sub = lax.broadcasted_iota(jnp.int32, gt.shape, 0)
    i_col = jnp.sum(jnp.where(lane == G_I + h, g, 0.0), axis=1, keepdims=True)
    f_col = jnp.sum(jnp.where(lane == G_F + h, g, 0.0), axis=1, keepdims=True)
    i_row = jnp.sum(jnp.where(sub == G_I + h, gt, 0.0), axis=0, keepdims=True)
    f_row = jnp.sum(jnp.where(sub == G_F + h, gt, 0.0), axis=0, keepdims=True)
    lf_col = jax.nn.log_sigmoid(f_col)
    lf_row = jax.nn.log_sigmoid(f_row)
    t_col = lax.broadcasted_iota(jnp.int32, (L, 1), 0)
    s_row = lax.broadcasted_iota(jnp.int32, (1, L), 1)
    if valid < L:
        lf_col = jnp.where(t_col < valid, lf_col, 0.0)
        lf_row = jnp.where(s_row < valid, lf_row, 0.0)
        i_col = jnp.where(t_col < valid, i_col, NEG)
        i_row = jnp.where(s_row < valid, i_row, NEG)
    tt = lax.broadcasted_iota(jnp.int32, (L, L), 0)
    ss = lax.broadcasted_iota(jnp.int32, (L, L), 1)
    causal = ss <= tt
    b_col = jnp.sum(jnp.where(causal, lf_row, 0.0), axis=1, keepdims=True)
    b_row = jnp.sum(jnp.where(tt <= ss, lf_col, 0.0), axis=0, keepdims=True)
    m_prev = m_s[...]
    cmat = c_s[...]
    n_row = n_s[...]

    d_log = jnp.where(causal, b_col - b_row + i_row, NEG)
    inter = b_col + m_prev
    m_t = jnp.maximum(inter, jnp.max(d_log, axis=1, keepdims=True))
    qb = q.astype(bf16)
    qk = lax.dot_general(qb, k.astype(bf16), (((1,), (1,)), ((), ())), preferred_element_type=f32)
    smat = qk * jnp.exp(d_log - m_t)
    w_inter = jnp.exp(inter - m_t)
    vb = v.astype(bf16)
    num = (w_inter * jnp.dot(qb, cmat.astype(bf16), preferred_element_type=f32)
           + jnp.dot(smat.astype(bf16), vb, preferred_element_type=f32))
    den = w_inter * jnp.sum(q * n_row, axis=1, keepdims=True) + jnp.sum(smat, axis=1, keepdims=True)
    hid = num / jnp.maximum(jnp.abs(den), jnp.exp(-m_t))
    mu = jnp.mean(hid, axis=1, keepdims=True)
    var = jnp.mean(jnp.square(hid - mu), axis=1, keepdims=True)
    hid = (hid - mu) * lax.rsqrt(var + LN_EPS) * ng_ref[...]
    y_ref[...] = hid * jax.nn.sigmoid(ao_ref[...])

    b_end = b_col[L - 1:L, :]
    g_row = b_end - b_row + i_row
    m_new = jnp.maximum(b_end + m_prev, jnp.max(g_row, axis=1, keepdims=True))
    a = jnp.exp(b_end + m_prev - m_new)
    w_col = jnp.exp(b_end - b_col + i_col - m_new)
    kw = k * w_col
    c_new = a * cmat + lax.dot_general(kw.astype(bf16), vb, (((0,), (0,)), ((), ())),
                                       preferred_element_type=f32)
    n_new = a * n_row + jnp.sum(kw, axis=0, keepdims=True)
    c_s[...] = c_new
    n_s[...] = n_new
    m_s[...] = m_new

    @pl.when(c == pl.num_programs(2) - 1)
    def _():
        c_out[0, 0] = c_new
        n_out[0, 0] = n_new
        m_out[0, 0] = m_new


def _into(buf, kern):
    if buf is None:
        return kern, [], [], {}
    return (lambda buf_ref, *refs: kern(*refs)), [pl.BlockSpec(memory_space=pl.ANY)], [buf], {0: 0}


def _mlstm(proj, gt, norm_g, c0, n0, m0, *, row0, nb, t, L, valid, y_buf=None):
    nc = t // L
    rb0 = row0 // L
    gt = gt[:8, row0:row0 + nb * t].reshape(8, nb * nc, L).transpose(1, 0, 2)
    rows = lambda b, h, c: rb0 + b * nc + c
    st = lambda b, h, c: (b, h, 0, 0)
    kern, alias_specs, alias_args, aliases = _into(y_buf, functools.partial(_mlstm_kernel, L=L, valid=valid))
    y, c_f, n_f, m_f = pl.pallas_call(
        kern,
        grid=(nb, A_HEADS, nc),
        input_output_aliases=aliases,
        in_specs=alias_specs + [
            pl.BlockSpec((L, A_DQK), lambda b, h, c: (rows(b, h, c), C_AQ // A_DQK + h)),
            pl.BlockSpec((L, A_DQK), lambda b, h, c: (rows(b, h, c), C_AK // A_DQK + h)),
            pl.BlockSpec((L, A_DV), lambda b, h, c: (rows(b, h, c), C_AV // A_DV + h)),
            pl.BlockSpec((L, A_DV), lambda b, h, c: (rows(b, h, c), C_AO // A_DV + h)),
            pl.BlockSpec((L, LANES), lambda b, h, c: (rows(b, h, c), C_SMALL // LANES)),
            pl.BlockSpec((1, 8, L), lambda b, h, c: (b * nc + c, 0, 0)),
            pl.BlockSpec((1, A_DV), lambda b, h, c: (0, h)),
            pl.BlockSpec((1, 1, A_DQK, A_DV), st),
            pl.BlockSpec((1, 1, 1, A_DQK), st),
            pl.BlockSpec((1, 1, 1, 1), st),
        ],
        out_specs=[
            pl.BlockSpec((L, A_DV), lambda b, h, c: (rows(b, h, c), h)),
            pl.BlockSpec((1, 1, A_DQK, A_DV), st),
            pl.BlockSpec((1, 1, 1, A_DQK), st),
            pl.BlockSpec((1, 1, 1, 1), st),
        ],
        out_shape=[
            jax.ShapeDtypeStruct((proj.shape[0], A_HEADS * A_DV), f32),
            jax.ShapeDtypeStruct((nb, A_HEADS, A_DQK, A_DV), f32),
            jax.ShapeDtypeStruct((nb, A_HEADS, 1, A_DQK), f32),
            jax.ShapeDtypeStruct((nb, A_HEADS, 1, 1), f32),
        ],
        scratch_shapes=[pltpu.VMEM((A_DQK, A_DV), f32), pltpu.VMEM((1, A_DQK), f32), pltpu.VMEM((1, 1), f32)],
        compiler_params=_cparams("parallel", "parallel", "arbitrary"),
        name="mlstm",
    )(*alias_args, proj, proj, proj, proj, proj, gt, norm_g, c0, n0, m0)
    return y, c_f, n_f[:, :, 0], m_f[:, :, 0, 0]


def _compress_kernel(x_ref, pe_ref, w_ref, o_ref, xf_ref, *, nblk):
    for l in range(CMP_BLOCK):
        xf_ref[:, l * LANES:(l + 1) * LANES] = x_ref[pl.ds(l, nblk, stride=CMP_BLOCK), :]
    xf = (xf_ref[...] + pe_ref[0]).astype(bf16)
    o_ref[...] = jnp.dot(xf, w_ref[0], preferred_element_type=f32)


def _compress(x2, pe2, w2, *, rows, steps, row0, colblk):
    nblk = rows // CMP_BLOCK
    kflat = CMP_BLOCK * LANES
    rb0 = row0 // rows
    ngrp = KV_COLS // LANES
    return pl.pallas_call(
        functools.partial(_compress_kernel, nblk=nblk),
        grid=(steps, ngrp),
        in_specs=[pl.BlockSpec((rows, LANES), lambda s, p: (rb0 + s, colblk * ngrp + p)),
                  pl.BlockSpec((1, 1, kflat), lambda s, p: (p // 2, 0, 0)),
                  pl.BlockSpec((1, kflat, LANES), lambda s, p: (p // 2, 0, 0))],
        out_specs=pl.BlockSpec((nblk, LANES), lambda s, p: (s, p)),
        out_shape=jax.ShapeDtypeStruct((steps * nblk, KV_COLS), f32),
        scratch_shapes=[pltpu.VMEM((nblk, kflat), f32)],
        compiler_params=_cparams("parallel", "parallel"),
        name="compress",
    )(x2, pe2, w2)


def _gate_col(g, lane_idx):
    lane = lax.broadcasted_iota(jnp.int32, g.shape, 1)
    return jax.nn.sigmoid(jnp.sum(jnp.where(lane == lane_idx, g, 0.0), axis=1, keepdims=True))


def _cmp_topk_kernel(slope_ref, q_ref, kc_ref, vc_ref, g_ref, o_ref, sel_ref, *, tq, nblk, qpos0):
    gi = pl.program_id(1)
    i = pl.program_id(2)
    qpos = qpos0 + i * tq + lax.broadcasted_iota(jnp.int32, (tq, 1), 0)
    j = lax.broadcasted_iota(jnp.int32, (1, nblk), 1)
    dist = qpos - ((j + 1) * CMP_BLOCK - 1)
    valid = dist >= 0
    distf = dist.astype(f32)
    kc = kc_ref[0, 0].astype(bf16)
    vc = vc_ref[0, 0].astype(bf16)
    g = g_ref[...]
    imp = jnp.zeros((tq, nblk), f32)
    for r in range(B_REP):
        qr = q_ref[0, 0, 0, r * tq:(r + 1) * tq, :]
        s = lax.dot_general(qr, kc, (((1,), (1,)), ((), ())), preferred_element_type=f32) * (B_HD ** -0.5)
        s = s - slope_ref[gi * B_REP + r] * distf
        s = jnp.where(valid, s, NEG)
        e = jnp.exp(s - jnp.max(s, axis=1, keepdims=True))
        p = jnp.where(valid, e / jnp.sum(e, axis=1, keepdims=True), 0.0)
        imp = imp + p
        o = jnp.dot(p.astype(bf16), vc, preferred_element_type=f32)
        o_ref[0, 0, 0, r * tq:(r + 1) * tq, :] = o * _gate_col(g, G_GATE + gi * B_REP + r)
    cur = qpos // CMP_BLOCK
    imp = jnp.where((j == cur) | (j == 0), float(B_REP + 1), imp)
    imp = jnp.where(j > cur, -1.0, imp)
    jf = j.astype(f32)
    sel = jnp.zeros((tq, nblk), f32)
    for _ in range(N_SEL):
        mx = jnp.max(imp, axis=1, keepdims=True)
        idx = jnp.min(jnp.where(imp == mx, jf, float(nblk)), axis=1, keepdims=True)
        hit = jf == idx
        sel = jnp.where(hit, 1.0, sel)
        imp = jnp.where(hit, NEG, imp)
    sel_ref[0, 0] = sel


def _cmp_topk(slopes, qt, kc, vc, gsmall, *, tq, qpos0, row0):
    nb, _, nqt, _, _ = qt.shape
    nblk = kc.shape[2]
    rb0 = row0 // tq
    return pl.pallas_call(
        functools.partial(_cmp_topk_kernel, tq=tq, nblk=nblk, qpos0=qpos0),
        grid=(nb, B_KV, nqt),
        in_specs=[
            pl.BlockSpec(memory_space=pltpu.SMEM),
            pl.BlockSpec((1, 1, 1, B_REP * tq, B_HD), lambda b, g, i: (b, g, i, 0, 0)),
            pl.BlockSpec((1, 1, nblk, B_HD), lambda b, g, i: (b, g, 0, 0)),
            pl.BlockSpec((1, 1, nblk, B_HD), lambda b, g, i: (b, g, 0, 0)),
            pl.BlockSpec((tq, LANES), lambda b, g, i: (rb0 + b * nqt + i, 0)),
        ],
        out_specs=[
            pl.BlockSpec((1, 1, 1, B_REP * tq, B_HD), lambda b, g, i: (b, g, i, 0, 0)),
            pl.BlockSpec((1, 1, tq, nblk), lambda b, g, i: (b, g, i, 0)),
        ],
        out_shape=[
            jax.ShapeDtypeStruct(qt.shape, f32),
            jax.ShapeDtypeStruct((nb, B_KV, nqt * tq, nblk), f32),
        ],
        compiler_params=_cparams("parallel", "parallel", "parallel"),
        name="cmp_topk",
    )(slopes, qt, kc, vc, gsmall)


def _attn_kernel(slope_ref, q_ref, k_ref, v_ref, g_ref, *rest, tq, tk, nkt, nblk, qpos0, kpos0, window,
                 gate_lane):
    if window is None:
        sel_ref, o_ref, m_s, l_s, acc_s = rest
    else:
        o_ref, m_s, l_s, acc_s = rest
    gi = pl.program_id(1)
    i = pl.program_id(2)
    qlo = qpos0 + i * tq
    qpos = qlo + lax.broadcasted_iota(jnp.int32, (tq, 1), 0)
    m_s[...] = jnp.full(m_s.shape, NEG, f32)
    l_s[...] = jnp.zeros(l_s.shape, f32)
    acc_s[...] = jnp.zeros(acc_s.shape, f32)
    kt_hi = jnp.minimum((qlo + tq - 1 - kpos0) // tk + 1, nkt)
    if window is None:
        kt_lo = 0
        selb = sel_ref[0, 0].astype(bf16)
    else:
        kt_lo = jnp.maximum(qlo - (window - 1) - kpos0, 0) // tk

    def body(kt, carry):
        k0 = pl.multiple_of(kt * tk, tk)
        kb = k_ref[0, 0, pl.ds(k0, tk), :]
        vb = v_ref[0, 0, pl.ds(k0, tk), :]
        kidx = k0 + lax.broadcasted_iota(jnp.int32, (1, tk), 1)
        dist = qpos - (kpos0 + kidx)
        mask = dist >= 0
        if window is None:
            blk = lax.broadcasted_iota(jnp.int32, (nblk, tk), 0)
            kblk = (k0 + lax.broadcasted_iota(jnp.int32, (nblk, tk), 1)) // CMP_BLOCK
            expand = jnp.where(blk == kblk, 1.0, 0.0).astype(bf16)
            mask = mask & (jnp.dot(selb, expand, preferred_element_type=f32) > 0.5)
        else:
            mask = mask & (dist < window)
        distf = dist.astype(f32)
        for r in range(B_REP):
            rows = slice(r * tq, (r + 1) * tq)
            s = lax.dot_general(q_ref[0, 0, 0, rows, :], kb, (((1,), (1,)), ((), ())),
                                preferred_element_type=f32) * (B_HD ** -0.5)
            s = jnp.where(mask, s - slope_ref[gi * B_REP + r] * distf, NEG)
            m_old = m_s[rows, :]
            m_new = jnp.maximum(m_old, jnp.max(s, axis=1, keepdims=True))
            alpha = jnp.exp(m_old - m_new)
            p = jnp.exp(s - m_new)
            l_s[rows, :] = alpha * l_s[rows, :] + jnp.sum(p, axis=1, keepdims=True)
            acc_s[rows, :] = alpha * acc_s[rows, :] + jnp.dot(p.astype(bf16), vb, preferred_element_type=f32)
            m_s[rows, :] = m_new
        return carry

    lax.fori_loop(kt_lo, kt_hi, body, 0)
    g = g_ref[...]
    for r in range(B_REP):
        rows = slice(r * tq, (r + 1) * tq)
        o_ref[0, 0, 0, rows, :] = acc_s[rows, :] / l_s[rows, :] * _gate_col(g, gate_lane + gi * B_REP + r)


def _attn(slopes, qt, kh, vh, gsmall, sel, *, tq, tk, qpos0, kpos0, window, gate_lane, row0):
    nb, _, nqt, _, _ = qt.shape
    tkk = kh.shape[2]
    nkt = tkk // tk
    nblk = None if sel is None else sel.shape[3]
    rb0 = row0 // tq
    in_specs = [
        pl.BlockSpec(memory_space=pltpu.SMEM),
        pl.BlockSpec((1, 1, 1, B_REP * tq, B_HD), lambda b, g, i: (b, g, i, 0, 0)),
        pl.BlockSpec((1, 1, tkk, B_HD), lambda b, g, i: (b, g, 0, 0)),
        pl.BlockSpec((1, 1, tkk, B_HD), lambda b, g, i: (b, g, 0, 0)),
        pl.BlockSpec((tq, LANES), lambda b, g, i: (rb0 + b * nqt + i, 0)),
    ]
    args = [slopes, qt, kh, vh, gsmall]
    if sel is not None:
        in_specs.append(pl.BlockSpec((1, 1, tq, nblk), lambda b, g, i: (b, g, i, 0)))
        args.append(sel)
    return pl.pallas_call(
        functools.partial(_attn_kernel, tq=tq, tk=tk, nkt=nkt, nblk=nblk, qpos0=qpos0, kpos0=kpos0,
                          window=window, gate_lane=gate_lane),
        grid=(nb, B_KV, nqt),
        in_specs=in_specs,
        out_specs=pl.BlockSpec((1, 1, 1, B_REP * tq, B_HD), lambda b, g, i: (b, g, i, 0, 0)),
        out_shape=jax.ShapeDtypeStruct(qt.shape, f32),
        scratch_shapes=[pltpu.VMEM((B_REP * tq, 1), f32), pltpu.VMEM((B_REP * tq, 1), f32),
                        pltpu.VMEM((B_REP * tq, B_HD), f32)],
        compiler_params=_cparams("parallel", "parallel", "parallel"),
        name="attn_sel" if window is None else "attn_win",
    )(*args)


def _roll_lanes(x, shift):
    return x if shift == 0 else pltpu.roll(x, shift, axis=1)


def _softmax_rows(s, mask):
    s = jnp.where(mask, s, NEG)
    e = jnp.exp(s - jnp.max(s, axis=1, keepdims=True))
    return jnp.where(mask, e, 0.0), jnp.sum(e, axis=1, keepdims=True)


def _decode_kernel(slope_ref, q_ref, g_ref, kvc_ref, pages, slc_new_ref, win_ref, win_new_ref, exp_ref,
                   oc_ref, os_ref, ow_ref, kt_s, vt_s, wkt_s, wvt_s, *, tq, nblk, qpos0, win_kpos0):
    nrow = B_HEADS * tq
    q = q_ref[...]
    gs = g_ref[...]
    lane_grp = lax.broadcasted_iota(jnp.int32, (tq, B_KV * B_HD), 1) // B_HD
    qm = []
    for g in range(B_KV):
        qg = q[:, g * B_KV * B_HD:(g + 1) * B_KV * B_HD]
        for r in range(B_REP):
            qm.append(jnp.where(lane_grp == g, _roll_lanes(qg, ((g - r) % B_REP) * B_HD), 0.0))
    qm = jnp.concatenate(qm, axis=0).astype(bf16)
    row = lax.broadcasted_iota(jnp.int32, (nrow, 1), 0)
    qpos = qpos0 + row % tq
    slope = functools.reduce(lambda acc, h: jnp.where(row // tq == h, slope_ref[h], acc), range(B_HEADS),
                             jnp.zeros((nrow, 1), f32))
    scale = B_HD ** -0.5
    nt = (((1,), (1,)), ((), ()))

    def emit(o_ref, o, branch):
        for g in range(B_KV):
            acc = jnp.zeros((tq, B_KV * B_HD), f32)
            for r in range(B_REP):
                h = g * B_REP + r
                gate = jax.nn.sigmoid(gs[:, G_GATE + branch * B_HEADS + h:G_GATE + branch * B_HEADS + h + 1])
                oh = jnp.where(lane_grp == g, o[h * tq:(h + 1) * tq, :] * gate, 0.0)
                acc = acc + _roll_lanes(oh, ((r - g) % B_REP) * B_HD)
            o_ref[:, g * B_KV * B_HD:(g + 1) * B_KV * B_HD] = acc

    kvc = kvc_ref[0]
    j = lax.broadcasted_iota(jnp.int32, (1, nblk), 1)
    dist = qpos - ((j + 1) * CMP_BLOCK - 1)
    s = lax.dot_general(qm, kvc[:, :B_KV * B_HD].astype(bf16), nt, preferred_element_type=f32) * scale
    e, l = _softmax_rows(s - slope * dist.astype(f32), dist >= 0)
    p = e / l
    emit(oc_ref, jnp.dot(p.astype(bf16), kvc[:, B_KV * B_HD:].astype(bf16), preferred_element_type=f32), 0)
    imp = jnp.concatenate(
        [functools.reduce(lambda a, b: a + b, [p[(g * B_REP + r) * tq:(g * B_REP + r + 1) * tq] for r in range(B_REP)])
         for g in range(B_KV)], axis=0)
    cur = (qpos0 + lax.broadcasted_iota(jnp.int32, (B_KV * tq, 1), 0) % tq) // CMP_BLOCK
    imp = jnp.where((j == cur) | (j == 0), float(B_REP + 1), imp)
    imp = jnp.where(j > cur, -1.0, imp)
    jf = j.astype(f32)
    sel = jnp.zeros(imp.shape, f32)
    for _ in range(N_SEL):
        mx = jnp.max(imp, axis=1, keepdims=True)
        idx = jnp.min(jnp.where(imp == mx, jf, float(nblk)), axis=1, keepdims=True)
        hit = jf == idx
        sel = jnp.where(hit, 1.0, sel)
        imp = jnp.where(hit, NEG, imp)
    sel_rows = jnp.concatenate([sel[g * tq:(g + 1) * tq] for g in range(B_KV) for _ in range(B_REP)], axis=0)

    def transposed_kv(parts, kt_s, vt_s):
        for u, part in enumerate(parts):
            w = part.shape[-1]
            kt_s[:, u * w:(u + 1) * w] = part[0, :GRP_LANES, :].astype(bf16)
            vt_s[:, u * w:(u + 1) * w] = part[0, GRP_LANES:, :].astype(bf16)

    def attend(kt_s, vt_s, mask):
        s = jnp.dot(qm, kt_s[...], preferred_element_type=f32) * scale
        e, l = _softmax_rows(s - slope * dist.astype(f32), mask)
        return lax.dot_general(e.astype(bf16), vt_s[...], nt, preferred_element_type=f32) / l

    transposed_kv(list(pages) + [slc_new_ref], kt_s, vt_s)
    dist = qpos - lax.broadcasted_iota(jnp.int32, (1, kt_s.shape[1]), 1)
    picked = jnp.dot(sel_rows.astype(bf16), exp_ref[...], preferred_element_type=f32) > 0.5
    emit(os_ref, attend(kt_s, vt_s, picked & (dist >= 0)), 1)

    wb = win_ref.shape[-1]
    wkt_s[:, :wb] = win_ref[0, :GRP_LANES, :].astype(bf16)
    wvt_s[:, :wb] = win_ref[0, GRP_LANES:, :].astype(bf16)
    wkt_s[:, wb:] = win_new_ref[0, :GRP_LANES, :].astype(bf16)
    wvt_s[:, wb:] = win_new_ref[0, GRP_LANES:, :].astype(bf16)
    dist = qpos - (win_kpos0 + lax.broadcasted_iota(jnp.int32, (1, wkt_s.shape[1]), 1))
    emit(ow_ref, attend(wkt_s, wvt_s, (dist >= 0) & (dist < WINDOW)), 2)


def _decode(page_table, bufs, slopes, proj, kvc, slc_cache_t, slc_new_t, win_t, win_new_t, expand, *,
            tq, row0, qpos0, win_kpos0):
    nb, n_pages = page_table.shape
    nblk = kvc.shape[1]
    rb0 = row0 // tq
    tk = (n_pages + 1) * PAGE_SIZE
    twin = win_t.shape[-1] + win_new_t.shape[-1]
    out = pl.BlockSpec((tq, B_HEADS * B_HD), lambda b, pt: (rb0 + b, 0))
    page = (1, KV_COLS, PAGE_SIZE)

    def body(pt_ref, oc_buf, os_buf, ow_buf, slope_ref, q_ref, g_ref, kvc_ref, *refs):
        _decode_kernel(slope_ref, q_ref, g_ref, kvc_ref, refs[:n_pages], *refs[n_pages:], tq=tq, nblk=nblk,
                       qpos0=qpos0, win_kpos0=win_kpos0)

    return pl.pallas_call(
        body,
        grid_spec=pltpu.PrefetchScalarGridSpec(
            num_scalar_prefetch=1,
            grid=(nb,),
            in_specs=[pl.BlockSpec(memory_space=pl.ANY)] * 3
            + [pl.BlockSpec(memory_space=pltpu.SMEM),
               pl.BlockSpec((tq, B_HEADS * B_HD), lambda b, pt: (rb0 + b, C_BQ // (B_HEADS * B_HD))),
               pl.BlockSpec((tq, LANES), lambda b, pt: (rb0 + b, C_SMALL // LANES)),
               pl.BlockSpec((1,) + kvc.shape[1:], lambda b, pt: (b, 0, 0))]
            + [pl.BlockSpec(page, functools.partial(lambda b, pt, u: (pt[b, u], 0, 0), u=u)) for u in range(n_pages)]
            + [pl.BlockSpec((1,) + slc_new_t.shape[1:], lambda b, pt: (b, 0, 0)),
               pl.BlockSpec((1,) + win_t.shape[1:], lambda b, pt: (b, 0, 0)),
               pl.BlockSpec((1,) + win_new_t.shape[1:], lambda b, pt: (b, 0, 0)),
               pl.BlockSpec(expand.shape, lambda b, pt: (0, 0))],
            out_specs=[out, out, out],
            scratch_shapes=[pltpu.VMEM((GRP_LANES, tk), bf16), pltpu.VMEM((GRP_LANES, tk), bf16),
                            pltpu.VMEM((GRP_LANES, twin), bf16), pltpu.VMEM((GRP_LANES, twin), bf16)],
        ),
        out_shape=[jax.ShapeDtypeStruct(b.shape, f32) for b in bufs],
        input_output_aliases={1: 0, 2: 1, 3: 2},
        compiler_params=_cparams("parallel"),
        name="nsa_decode",
    )(page_table, *bufs, slopes, proj, proj, kvc, *([slc_cache_t] * n_pages), slc_new_t, win_t, win_new_t, expand)


GRP_LANES = B_KV * B_HD
_NT = (((1,), (1,)), ((), ()))


def _masked_queries(q, tq):
    lane_grp = lax.broadcasted_iota(jnp.int32, (tq, GRP_LANES), 1) // B_HD
    rows = []
    for g in range(B_KV):
        qg = q[:, g * GRP_LANES:(g + 1) * GRP_LANES]
        for r in range(B_REP):
            rows.append(jnp.where(lane_grp == g, _roll_lanes(qg, ((g - r) % B_REP) * B_HD), 0.0))
    return jnp.concatenate(rows, axis=0).astype(bf16)


def _group_columns(slope_ref, g, tq, qlo):
    row = lax.broadcasted_iota(jnp.int32, (B_REP * tq, 1), 0)
    slope = functools.reduce(lambda acc, r: jnp.where(row // tq == r, slope_ref[g * B_REP + r], acc), range(B_REP),
                             jnp.zeros((B_REP * tq, 1), f32))
    return qlo + row % tq, slope


def _emit_group(o_ref, og, gs, branch, g, tq):
    lane_grp = lax.broadcasted_iota(jnp.int32, (tq, GRP_LANES), 1) // B_HD
    acc = jnp.zeros((tq, GRP_LANES), f32)
    for r in range(B_REP):
        c = G_GATE + branch * B_HEADS + g * B_REP + r
        oh = jnp.where(lane_grp == g, og[r * tq:(r + 1) * tq, :] * jax.nn.sigmoid(gs[:, c:c + 1]), 0.0)
        acc = acc + _roll_lanes(oh, ((r - g) % B_REP) * B_HD)
    o_ref[:, g * GRP_LANES:(g + 1) * GRP_LANES] = acc


def _cmp_nat_kernel(slope_ref, q_ref, g_ref, kvc_ref, o_ref, sel_ref, *, tq, nblk):
    qlo = pl.program_id(1) * tq
    qm = _masked_queries(q_ref[...], tq)
    gs = g_ref[...]
    kvc = kvc_ref[0]
    kc = kvc[:, :GRP_LANES].astype(bf16)
    vc = kvc[:, GRP_LANES:].astype(bf16)
    j = lax.broadcasted_iota(jnp.int32, (1, nblk), 1)
    jf = j.astype(f32)
    for g in range(B_KV):
        qpos, slope = _group_columns(slope_ref, g, tq, qlo)
        dist = qpos - ((j + 1) * CMP_BLOCK - 1)
        s = lax.dot_general(qm[g * B_REP * tq:(g + 1) * B_REP * tq], kc, _NT, preferred_element_type=f32) * (B_HD ** -0.5)
        e, l = _softmax_rows(s - slope * dist.astype(f32), dist >= 0)
        p = e / l
        _emit_group(o_ref, jnp.dot(p.astype(bf16), vc, preferred_element_type=f32), gs, 0, g, tq)
        imp = functools.reduce(lambda a, b: a + b, [p[r * tq:(r + 1) * tq] for r in range(B_REP)])
        cur = qpos[:tq] // CMP_BLOCK
        imp = jnp.where((j == cur) | (j == 0), float(B_REP + 1), imp)
        imp = jnp.where(j > cur, -1.0, imp)
        sel = jnp.zeros(imp.shape, f32)
        for _ in range(N_SEL):
            mx = jnp.max(imp, axis=1, keepdims=True)
            idx = jnp.min(jnp.where(imp == mx, jf, float(nblk)), axis=1, keepdims=True)
            hit = jf == idx
            sel = jnp.where(hit, 1.0, sel)
            imp = jnp.where(hit, NEG, imp)
        sel_ref[0, 0, g * tq:(g + 1) * tq, :] = sel


def _cmp_nat(slopes, proj, kvc, *, nb, t, tq):
    nqt = t // tq
    nblk = kvc.shape[1]
    return pl.pallas_call(
        functools.partial(_cmp_nat_kernel, tq=tq, nblk=nblk),
        grid=(nb, nqt),
        in_specs=[pl.BlockSpec(memory_space=pltpu.SMEM),
                  pl.BlockSpec((tq, B_HEADS * B_HD), lambda b, i: (b * nqt + i, C_BQ // (B_HEADS * B_HD))),
                  pl.BlockSpec((tq, LANES), lambda b, i: (b * nqt + i, C_SMALL // LANES)),
                  pl.BlockSpec((1, nblk, KV_COLS), lambda b, i: (b, 0, 0))],
        out_specs=[pl.BlockSpec((tq, B_HEADS * B_HD), lambda b, i: (b * nqt + i, 0)),
                   pl.BlockSpec((1, 1, B_KV * tq, nblk), lambda b, i: (b, i, 0, 0))],
        out_shape=[jax.ShapeDtypeStruct((proj.shape[0], B_HEADS * B_HD), f32),
                   jax.ShapeDtypeStruct((nb, nqt, B_KV * tq, nblk), f32)],
        compiler_params=_cparams("parallel", "parallel"),
        name="nsa_cmp",
    )(slopes, proj, proj, kvc)


def _sel_nat_kernel(slope_ref, q_ref, g_ref, kv_ref, sel_ref, o_ref, qm_s, m_s, l_s, acc_s, *, tq, tk, nblk):
    qlo = pl.program_id(1) * tq
    qm_s[...] = _masked_queries(q_ref[...], tq)
    m_s[...] = jnp.full(m_s.shape, NEG, f32)
    l_s[...] = jnp.zeros(l_s.shape, f32)
    acc_s[...] = jnp.zeros(acc_s.shape, f32)
    selb = sel_ref[0, 0].astype(bf16)
    grows = B_REP * tq

    def body(kt, carry):
        k0 = pl.multiple_of(kt * tk, tk)
        kb = kv_ref[pl.ds(k0, tk), :GRP_LANES]
        vb = kv_ref[pl.ds(k0, tk), GRP_LANES:]
        kidx = k0 + lax.broadcasted_iota(jnp.int32, (1, tk), 1)
        blk = lax.broadcasted_iota(jnp.int32, (nblk, tk), 0)
        kblk = (k0 + lax.broadcasted_iota(jnp.int32, (nblk, tk), 1)) // CMP_BLOCK
        picked = jnp.dot(selb, jnp.where(blk == kblk, 1.0, 0.0).astype(bf16), preferred_element_type=f32) > 0.5
        for g in range(B_KV):
            rows = slice(g * grows, (g + 1) * grows)
            qpos, slope = _group_columns(slope_ref, g, tq, qlo)
            dist = qpos - kidx
            mask = jnp.concatenate([picked[g * tq:(g + 1) * tq]] * B_REP, axis=0) & (dist >= 0)
            s = lax.dot_general(qm_s[rows, :], kb, _NT, preferred_element_type=f32) * (B_HD ** -0.5)
            s = jnp.where(mask, s - slope * dist.astype(f32), NEG)
            m_old = m_s[rows, :]
            m_new = jnp.maximum(m_old, jnp.max(s, axis=1, keepdims=True))
            alpha = jnp.exp(m_old - m_new)
            p = jnp.exp(s - m_new)
            l_s[rows, :] = alpha * l_s[rows, :] + jnp.sum(p, axis=1, keepdims=True)
            acc_s[rows, :] = alpha * acc_s[rows, :] + jnp.dot(p.astype(bf16), vb, preferred_element_type=f32)
            m_s[rows, :] = m_new
        return carry

    lax.fori_loop(0, (qlo + tq - 1) // tk + 1, body, 0)
    gs = g_ref[...]
    for g in range(B_KV):
        rows = slice(g * grows, (g + 1) * grows)
        _emit_group(o_ref, acc_s[rows, :] / l_s[rows, :], gs, 1, g, tq)


def _win_nat_kernel(slope_ref, q_ref, g_ref, kv_ref, o_ref, *, tq, t):
    qlo = pl.program_id(1) * tq
    span = WINDOW + tq
    k0 = pl.multiple_of(jnp.clip(qlo - WINDOW, 0, t - span), LANES)
    qm = _masked_queries(q_ref[...], tq)
    gs = g_ref[...]
    kb = kv_ref[pl.ds(k0, span), :GRP_LANES]
    vb = kv_ref[pl.ds(k0, span), GRP_LANES:]
    kidx = k0 + lax.broadcasted_iota(jnp.int32, (1, span), 1)
    for g in range(B_KV):
        qpos, slope = _group_columns(slope_ref, g, tq, qlo)
        dist = qpos - kidx
        s = lax.dot_general(qm[g * B_REP * tq:(g + 1) * B_REP * tq], kb, _NT, preferred_element_type=f32) * (B_HD ** -0.5)
        e, l = _softmax_rows(s - slope * dist.astype(f32), (dist >= 0) & (dist < WINDOW))
        _emit_group(o_ref, jnp.dot(e.astype(bf16), vb, preferred_element_type=f32) / l, gs, 2, g, tq)


def _attn_nat(slopes, proj, kvb, sel, *, nb, t, tq, tk, branch):
    nqt = t // tq
    in_specs = [pl.BlockSpec(memory_space=pltpu.SMEM),
                pl.BlockSpec((tq, B_HEADS * B_HD), lambda b, i: (b * nqt + i, C_BQ // (B_HEADS * B_HD))),
                pl.BlockSpec((tq, LANES), lambda b, i: (b * nqt + i, C_SMALL // LANES)),
                pl.BlockSpec((t, KV_COLS), lambda b, i: (b, branch))]
    args = [slopes, proj, proj, kvb]
    if sel is None:
        body, scratch, name = functools.partial(_win_nat_kernel, tq=tq, t=t), [], "nsa_win"
    else:
        nblk = sel.shape[3]
        in_specs.append(pl.BlockSpec((1, 1, B_KV * tq, nblk), lambda b, i: (b, i, 0, 0)))
        args.append(sel)
        body = functools.partial(_sel_nat_kernel, tq=tq, tk=tk, nblk=nblk)
        scratch = [pltpu.VMEM((B_HEADS * tq, GRP_LANES), bf16), pltpu.VMEM((B_HEADS * tq, 1), f32),
                   pltpu.VMEM((B_HEADS * tq, 1), f32), pltpu.VMEM((B_HEADS * tq, GRP_LANES), f32)]
        name = "nsa_sel"
    return pl.pallas_call(
        body,
        grid=(nb, nqt),
        in_specs=in_specs,
        out_specs=pl.BlockSpec((tq, B_HEADS * B_HD), lambda b, i: (b * nqt + i, 0)),
        out_shape=jax.ShapeDtypeStruct((proj.shape[0], B_HEADS * B_HD), f32),
        scratch_shapes=scratch,
        compiler_params=_cparams("parallel", "parallel"),
        name=name,
    )(*args)


PAD_PAGES = 4


def _gather_kernel(pt_ref, *refs):
    del pt_ref
    pages, tail_ref, o_ref = refs[:-2], refs[-2], refs[-1]
    for u, page in enumerate(pages):
        o_ref[0, u] = page[0]
    o_ref[0, len(pages)] = tail_ref[0]
    for u in range(len(pages) + 1, len(pages) + PAD_PAGES):
        o_ref[0, u] = jnp.zeros(o_ref.shape[2:], f32)


def _gather_pages(page_table, cache, tail):
    nb, n_pages = page_table.shape
    page = (1, PAGE_SIZE, KV_COLS)
    return pl.pallas_call(
        _gather_kernel,
        grid_spec=pltpu.PrefetchScalarGridSpec(
            num_scalar_prefetch=1,
            grid=(nb,),
            in_specs=[pl.BlockSpec(page, functools.partial(lambda b, pt, u: (pt[b, u], 0, 0), u=u))
                      for u in range(n_pages)] + [pl.BlockSpec(page, lambda b, pt: (b, 0, 0))],
            out_specs=pl.BlockSpec((1, n_pages + PAD_PAGES, PAGE_SIZE, KV_COLS), lambda b, pt: (b, 0, 0, 0)),
        ),
        out_shape=jax.ShapeDtypeStruct((nb, n_pages + PAD_PAGES, PAGE_SIZE, KV_COLS), f32),
        compiler_params=_cparams("parallel"),
        name="gather_pages",
    )(page_table, *([cache] * n_pages), tail)


def _layer_norm(z, g, b):
    mu = jnp.mean(z, axis=1, keepdims=True)
    var = jnp.mean(jnp.square(z - mu), axis=1, keepdims=True)
    return (z - mu) * lax.rsqrt(var + LN_EPS) * g + b


def _tail_kernel(x_ref, ya_ref, oc_ref, os_ref, ow_ref, wm_ref, wa_ref, wb_ref, wo_ref, g_ref, b_ref,
                 h_ref, hb_ref):
    x = x_ref[...]
    gates = jax.nn.sigmoid(jnp.dot(x.astype(bf16), wm_ref[...], preferred_element_type=f32))
    yb = oc_ref[...] + os_ref[...] + ow_ref[...]
    ma = jnp.dot(ya_ref[...].astype(bf16), wa_ref[...], preferred_element_type=f32)
    mb = jnp.dot(yb.astype(bf16), wb_ref[...], preferred_element_type=f32)
    merged = gates[:, :D_MODEL] * ma + gates[:, D_MODEL:] * mb
    z = DN_ALPHA * x + jnp.dot(merged.astype(bf16), wo_ref[...], preferred_element_type=f32)
    h = _layer_norm(z, g_ref[...], b_ref[...])
    h_ref[...] = h
    hb_ref[...] = h.astype(bf16)


def _tail(x, ya, oc, os_, ow, wm, wa, wb, wo, g, b, tm):
    n = x.shape[0]
    row = pl.BlockSpec((tm, D_MODEL), lambda i: (i, 0))
    return pl.pallas_call(
        _tail_kernel,
        grid=(n // tm,),
        in_specs=[row] * 5 + [_full(wm.shape), _full(wa.shape), _full(wb.shape), _full(wo.shape),
                              _full(g.shape), _full(b.shape)],
        out_specs=[row, row],
        out_shape=[jax.ShapeDtypeStruct((n, D_MODEL), f32), jax.ShapeDtypeStruct((n, D_MODEL), bf16)],
        compiler_params=_cparams("parallel"),
        name="tail",
    )(x, ya, oc, os_, ow, wm, wa, wb, wo, g, b)


def _top16(x):
    kk, tb = x.shape
    ji = lax.broadcasted_iota(jnp.int32, (kk, tb), 0).astype(f32)
    rank = jnp.full((kk, tb), float(P_TOPK), f32)
    vals = []
    for k in range(P_TOPK):
        mx = jnp.max(x, axis=0, keepdims=True)
        idx = jnp.min(jnp.where(x == mx, ji, float(kk)), axis=0, keepdims=True)
        hit = ji == idx
        rank = jnp.where(hit, float(k), rank)
        vals.append(mx)
        x = jnp.where(hit, NEG, x)
    return rank, vals


_CAND_ROWS8 = ((1, 8), (2, 5), (3, 4), (4, 3))


def _route_kernel(h_ref, wqt_ref, keys_ref, ta_ref, tb_ref):
    qpt = lax.dot_general(wqt_ref[...], h_ref[...], (((1,), (1,)), ((), ())), preferred_element_type=f32)
    tb = qpt.shape[1]
    sub16 = lax.broadcasted_iota(jnp.int32, (P_TOPK, tb), 0)
    sub8 = lax.broadcasted_iota(jnp.int32, (8, tb), 0)
    for p in range(P_HEADS):
        sc, rk, vl = [], [], []
        for c in range(2):
            qs = qpt[(2 * p + c) * P_DHALF:(2 * p + c + 1) * P_DHALF, :].astype(bf16)
            s = jnp.dot(keys_ref[p, c], qs, preferred_element_type=f32)
            r, v = _top16(s)
            sc.append(s)
            rk.append(r)
            vl.append(v)
        v0, v1 = vl
        col0 = functools.reduce(lambda acc, k: jnp.where(sub16 == k, v0[k], acc), range(P_TOPK), jnp.zeros((P_TOPK, tb), f32))
        col1 = functools.reduce(lambda acc, k: jnp.where(sub16 == k, v1[k], acc), range(P_TOPK), jnp.zeros((P_TOPK, tb), f32))
        segs = [v0[0] + col1]
        for k1, keep in _CAND_ROWS8:
            segs.append(jnp.where(sub8 < keep, v0[k1] + col1[0:8], NEG))
        first = jnp.where(sub8 < 2, v0[5], jnp.where(sub8 < 4, v0[6], v0[7]))
        second = jnp.where(sub8 % 2 == 0, v1[0], v1[1])
        segs.append(jnp.where(sub8 < 6, first + second, NEG))
        segs.append(col0[8:16] + v1[0])
        cand = jnp.concatenate(segs, axis=0)
        crank, cvals = _top16(cand)
        taken = jnp.where(crank < float(P_TOPK), 1.0, 0.0)
        z = functools.reduce(lambda acc, v: acc + jnp.exp(v - cvals[0]), cvals, jnp.zeros((1, tb), f32))
        cnt = [jnp.sum(taken[0:16], axis=0, keepdims=True)]
        for i in range(len(_CAND_ROWS8)):
            cnt.append(jnp.sum(taken[16 + 8 * i:24 + 8 * i], axis=0, keepdims=True))
        t5 = taken[48:56]
        for lo in (0, 2, 4):
            cnt.append(jnp.sum(jnp.where((sub8 >= lo) & (sub8 < lo + 2), t5, 0.0), axis=0, keepdims=True))
        for i in range(8):
            cnt.append(taken[56 + i:57 + i])
        n_a = functools.reduce(lambda acc, k: jnp.where(rk[0] == float(k), cnt[k], acc), range(P_TOPK),
                               jnp.zeros((P_NKEYS, tb), f32))
        ta_ref[p, 0] = n_a
        ta_ref[p, 1] = jnp.exp(sc[0] - v0[0])
        tb_ref[p, 0] = rk[1].astype(bf16)
        tb_ref[p, 1] = (jnp.exp(sc[1] - v1[0]) / z).astype(bf16)


def _route(hb, wqt, keys, tb):
    n = hb.shape[0]
    spec = pl.BlockSpec((P_HEADS, 2, P_NKEYS, tb), lambda i: (0, 0, 0, i))
    return pl.pallas_call(
        _route_kernel,
        grid=(n // tb,),
        in_specs=[pl.BlockSpec((tb, D_MODEL), lambda i: (i, 0)), _full(wqt.shape), _full(keys.shape)],
        out_specs=[spec, spec],
        out_shape=[jax.ShapeDtypeStruct((P_HEADS, 2, P_NKEYS, n), f32),
                   jax.ShapeDtypeStruct((P_HEADS, 2, P_NKEYS, n), bf16)],
        compiler_params=_cparams("parallel"),
        name="peer_route",
    )(hb, wqt, keys)


def _experts_kernel(hb_ref, h_ref, ta_ref, tb_ref, u_ref, vt_ref, g_ref, b_ref, y_ref, acc_s, ht_s, pt_s, *, te):
    j = pl.program_id(1)

    @pl.when(j == 0)
    def _():
        acc_s[...] = jnp.zeros(acc_s.shape, f32)

    ht_s[...] = lax.dot_general(u_ref[...], hb_ref[...], (((1,), (1,)), ((), ())), preferred_element_type=f32)
    for aa in range(te // P_NKEYS):
        a = j * (te // P_NKEYS) + aa
        n_rows = [ta_ref[p, 0, pl.ds(a, 1), :].astype(bf16) for p in range(P_HEADS)]
        e0_rows = [ta_ref[p, 1, pl.ds(a, 1), :].astype(bf16) for p in range(P_HEADS)]
        for lt in range(ht_s.shape[1] // LANES):
            ls = slice(lt * LANES, (lt + 1) * LANES)
            w = jnp.zeros((P_NKEYS, LANES), bf16)
            for p in range(P_HEADS):
                w = w + jnp.where(tb_ref[p, 0, :, ls] < n_rows[p][:, ls], e0_rows[p][:, ls] * tb_ref[p, 1, :, ls], 0.0)
            hs = ht_s[aa * P_NKEYS:(aa + 1) * P_NKEYS, ls]
            act = 0.5 * hs * (1.0 + lax.erf(hs * (0.5 ** 0.5)))
            pt_s[aa * P_NKEYS:(aa + 1) * P_NKEYS, ls] = w * act.astype(bf16)
    acc_s[...] += jnp.dot(vt_ref[...], pt_s[...], preferred_element_type=f32)

    @pl.when(j == pl.num_programs(1) - 1)
    def _():
        z = DN_ALPHA * h_ref[...] + acc_s[...].T
        y_ref[...] = _layer_norm(z, g_ref[...], b_ref[...])


def _experts(hb, h, ta, tbl, u, vt, g, b, tb, te):
    n = hb.shape[0]
    row = pl.BlockSpec((tb, D_MODEL), lambda i, j: (i, 0))
    tab = pl.BlockSpec((P_HEADS, 2, P_NKEYS, tb), lambda i, j: (0, 0, 0, i))
    return pl.pallas_call(
        functools.partial(_experts_kernel, te=te),
        grid=(n // tb, P_EXPERTS // te),
        in_specs=[row, row, tab, tab,
                  pl.BlockSpec((te, D_MODEL), lambda i, j: (j, 0)),
                  pl.BlockSpec((D_MODEL, te), lambda i, j: (0, j)),
                  pl.BlockSpec((1, D_MODEL), lambda i, j: (0, 0)),
                  pl.BlockSpec((1, D_MODEL), lambda i, j: (0, 0))],
        out_specs=row,
        out_shape=jax.ShapeDtypeStruct((n, D_MODEL), f32),
        scratch_shapes=[pltpu.VMEM((D_MODEL, tb), f32), pltpu.VMEM((te, tb), f32), pltpu.VMEM((te, tb), bf16)],
        compiler_params=_cparams("parallel", "arbitrary"),
        name="peer_experts",
    )(hb, h, ta, tbl, u, vt, g, b)


def _to_q_tiles(q2, nb, t, tq):
    q = q2.reshape(nb, t // tq, tq, B_KV, B_REP, B_HD).transpose(0, 3, 1, 4, 2, 5)
    return q.reshape(nb, B_KV, t // tq, B_REP * tq, B_HD).astype(bf16)


def _from_q_tiles(o, nb, t, tq):
    o = o.reshape(nb, B_KV, t // tq, B_REP, tq, B_HD).transpose(0, 2, 4, 1, 3, 5)
    return o.reshape(nb, t, B_HEADS * B_HD)


def _kv_heads(kv3):
    nb, tk, _ = kv3.shape
    kv = kv3.reshape(nb, tk, 2, B_KV, B_HD).transpose(2, 0, 3, 1, 4)
    return kv[0], kv[1]


def _nsa(slopes, proj, gsmall, pe2, w2, *, nb, t, tq, row0, qpos0, cmp_src, slc3, win3, win_kpos0, tk_sel, tk_win):
    qt = _to_q_tiles(proj[row0:row0 + nb * t, C_BQ:C_CMP], nb, t, tq)
    x2, cmp_rows, cmp_steps, cmp_colblk = cmp_src
    kvc = _compress(x2, pe2, w2, rows=cmp_rows, steps=cmp_steps, row0=0, colblk=cmp_colblk)
    kc, vc = _kv_heads(kvc.reshape(nb, -1, KV_COLS))
    o_cmp, sel = _cmp_topk(slopes, qt, kc, vc, gsmall, tq=tq, qpos0=qpos0, row0=row0)
    ks, vs = _kv_heads(slc3)
    o_slc = _attn(slopes, qt, ks.astype(bf16), vs.astype(bf16), gsmall, sel, tq=tq, tk=tk_sel, qpos0=qpos0,
                  kpos0=0, window=None, gate_lane=G_GATE + B_HEADS, row0=row0)
    kw, vw = _kv_heads(win3)
    o_win = _attn(slopes, qt, kw.astype(bf16), vw.astype(bf16), gsmall, None, tq=tq, tk=tk_win, qpos0=qpos0,
                  kpos0=win_kpos0, window=WINDOW, gate_lane=G_GATE + 2 * B_HEADS, row0=row0)
    return tuple(_from_q_tiles(o, nb, t, tq).reshape(nb * t, B_HEADS * B_HD) for o in (o_cmp, o_slc, o_win))


def kernel(x_prompt, x_sample, cache_cmp_kv, cache_slc_kv, cache_win_kv, state_C, state_n, state_m, page_table,
           w_in, b_in, norm_a_g, nsa_pe, nsa_w_cmp, w_br_a, w_br_b, w_merge, w_out, ln1_g, ln1_b,
           peer_wq, peer_keys, peer_u, peer_v, ln2_g, ln2_b):
    bp, tp, _ = x_prompt.shape
    bs, ts, _ = x_sample.shape
    tsp = 8
    n_p, n_s = bp * tp, bs * tsp
    past = page_table.shape[1] * PAGE_SIZE

    perm = np.concatenate([np.arange(0, 2048), np.arange(2056, 5640), np.arange(2048, 2056), np.arange(5640, 5688)])
    w_perm = jnp.pad(w_in[:, perm], ((0, 0), (0, C_END - perm.size)))
    b_perm = jnp.pad(b_in[perm], (0, C_END - perm.size))
    w_perm_b = w_perm.astype(bf16)
    slopes = jnp.asarray(2.0 ** (-8.0 * np.arange(1, B_HEADS + 1) / B_HEADS), f32)
    wc = nsa_w_cmp.reshape(2, CMP_BLOCK, 1, B_HD, 1, B_HD)
    eye2 = jnp.eye(2, dtype=f32).reshape(1, 1, 2, 1, 2, 1)
    w2 = (wc * eye2).reshape(2, CMP_BLOCK * LANES, LANES).astype(bf16)
    pe2 = jnp.tile(nsa_pe, (1, 1, 2)).reshape(2, 1, CMP_BLOCK * LANES)

    xs_pad = jnp.pad(x_sample, ((0, 0), (0, tsp - ts), (0, 0)))
    x_all = jnp.concatenate([x_prompt.reshape(n_p, D_MODEL), xs_pad.reshape(n_s, D_MODEL)], axis=0)
    xb = x_all.astype(bf16)
    proj, kvb = _proj(xb, w_perm_b, b_perm.reshape(1, C_END), 256)
    gt = _proj_t(w_perm_b[:, C_SMALL:].T, xb, b_perm[C_SMALL:].reshape(LANES, 1), 512)

    zc = jnp.zeros((bp, A_HEADS, A_DQK, A_DV), f32)
    zn = jnp.zeros((bp, A_HEADS, 1, A_DQK), f32)
    zm = jnp.zeros((bp, A_HEADS, 1, 1), f32)
    ng = norm_a_g.reshape(1, A_HEADS * A_DV)
    ya, p_c, p_n, p_m = _mlstm(proj, gt, ng, zc, zn, zm, row0=0, nb=bp, t=tp, L=256, valid=256)
    ya, s_c, s_n, s_m = _mlstm(proj, gt, ng, state_C, state_n.reshape(bs, A_HEADS, 1, A_DQK),
                               state_m.reshape(bs, A_HEADS, 1, 1), row0=n_p, nb=bs, t=tsp, L=tsp, valid=ts, y_buf=ya)

    wt_kv = w_perm_b[:, C_CMP:C_SMALL].T
    b_kv = b_perm[C_CMP:C_SMALL].reshape(C_SMALL - C_CMP, 1)
    kvt_p = _proj_kvt(wt_kv, xb, b_kv, row0=0, nb=bp, t=tp, tn=512)
    kvt_s = _proj_kvt(wt_kv, xb, b_kv, row0=n_p, nb=1, t=n_s, tn=512)
    kvt_s = [a.reshape(KV_COLS, bs, tsp).transpose(1, 0, 2) for a in kvt_s]
    to_rows = lambda a: a.reshape(a.shape[0], 2, B_KV, B_HD, a.shape[2]).transpose(0, 4, 1, 2, 3)
    new_lanes = lambda a: jnp.pad(a, ((0, 0), (0, 0), (0, LANES - tsp)))

    kvc_p = _compress(proj, pe2, w2, rows=n_p, steps=1, row0=0, colblk=C_CMP // KV_COLS)
    oc, sel_p = _cmp_nat(slopes, proj, kvc_p.reshape(bp, tp // CMP_BLOCK, KV_COLS), nb=bp, t=tp, tq=128)
    os_ = _attn_nat(slopes, proj, kvb, sel_p, nb=bp, t=tp, tq=128, tk=512, branch=1)
    ow = _attn_nat(slopes, proj, kvb, None, nb=bp, t=tp, tq=128, tk=512, branch=2)

    n_pool = cache_cmp_kv.shape[0]
    s_cmp_rows = kvt_s[0][:, :, :ts].transpose(0, 2, 1)
    all_cmp = _gather_pages(page_table, cache_cmp_kv.reshape(n_pool, PAGE_SIZE, KV_COLS),
                            jnp.pad(s_cmp_rows, ((0, 0), (0, PAGE_SIZE - ts), (0, 0))))
    tk_s = past + PAD_PAGES * PAGE_SIZE
    seqs_per_step = 8
    kvc_s = _compress(all_cmp.reshape(bs * tk_s, KV_COLS), pe2, w2, rows=seqs_per_step * tk_s,
                      steps=bs // seqs_per_step, row0=0, colblk=0)
    wb = cache_win_kv.shape[1]
    cache_t = lambda c: c.transpose(0, 2, 3, 4, 1).reshape(c.shape[0], KV_COLS, c.shape[1])
    win_t = cache_t(cache_win_kv)
    s_win_t = jnp.concatenate([win_t[:, :, ts:], kvt_s[2][:, :, :ts]], axis=2)
    tk_sel = past + PAGE_SIZE
    nblk_s = tk_s // CMP_BLOCK
    expand = jnp.asarray(np.arange(tk_sel)[None, :] // CMP_BLOCK == np.arange(nblk_s)[:, None], bf16)
    oc, os_, ow = _decode(page_table, (oc, os_, ow), slopes, proj, kvc_s.reshape(bs, nblk_s, KV_COLS),
                          cache_t(cache_slc_kv), new_lanes(kvt_s[1]), win_t, new_lanes(kvt_s[2]), expand,
                          tq=tsp, row0=n_p, qpos0=past, win_kpos0=past - wb)

    h1, h1b = _tail(x_all, ya, oc, os_, ow, w_merge.astype(bf16), w_br_a.astype(bf16), w_br_b.astype(bf16),
                    w_out.astype(bf16), ln1_g.reshape(1, D_MODEL), ln1_b.reshape(1, D_MODEL), 256)
    tab_a, tab_b = _route(h1b, peer_wq.T.astype(bf16), peer_keys.astype(bf16), 256)
    y = _experts(h1b, h1, tab_a, tab_b, peer_u.astype(bf16), peer_v.T.astype(bf16), ln2_g.reshape(1, D_MODEL),
                 ln2_b.reshape(1, D_MODEL), 512, 1024)

    y_prompt = y[:n_p].reshape(bp, tp, D_MODEL)
    y_sample = y[n_p:].reshape(bs, tsp, D_MODEL)[:, :ts]
    dt = x_prompt.dtype
    return (y_prompt, y_sample, to_rows(kvt_p[0]), to_rows(kvt_p[1]), to_rows(kvt_p[2][:, :, -min(WINDOW, tp):]),
            p_c.astype(dt), p_n.astype(dt), p_m.astype(dt),
            to_rows(kvt_s[0][:, :, :ts]), to_rows(kvt_s[1][:, :, :ts]), to_rows(s_win_t),
            s_c.astype(state_C.dtype), s_n.astype(state_C.dtype), s_m.astype(state_C.dtype))
```

```python
import functools

import jax
import jax.numpy as jnp
import numpy as np
from jax import lax
from jax.experimental import pallas as pl
from jax.experimental.pallas import tpu as pltpu

D_MODEL = 1024
A_HEADS, A_DQK, A_DV = 4, 128, 256
B_HEADS, B_KV, B_HD = 16, 4, 64
B_REP = B_HEADS // B_KV
CMP_BLOCK = 64
N_SEL = 16
WINDOW = 512
PAGE_SIZE = 128
P_HEADS, P_NKEYS, P_DHALF, P_TOPK = 8, 128, 128, 16
P_EXPERTS = P_NKEYS * P_NKEYS
DN_ALPHA = 2.0 ** 0.25
LN_EPS = 1e-5
NEG = -1e30

LANES = 128
KV_COLS = 2 * B_KV * B_HD
VMEM_LIMIT = 56 * 1024 * 1024

C_AQ, C_AK, C_AV, C_AO, C_BQ, C_CMP, C_SLC, C_WIN, C_SMALL, C_END = (
    0, 512, 1024, 2048, 3072, 4096, 4608, 5120, 5632, 5760)
G_I, G_F, G_GATE = 0, A_HEADS, 2 * A_HEADS

bf16 = jnp.bfloat16
f32 = jnp.float32
_NT_DIMS = (((1,), (1,)), ((), ()))


def _cparams(*sem, flags=None):
    return pltpu.CompilerParams(dimension_semantics=sem, vmem_limit_bytes=VMEM_LIMIT, flags=flags)


def _full(shape):
    nd = len(shape)
    return pl.BlockSpec(shape, lambda *_: (0,) * nd)


def _proj_kernel(x_ref, w_ref, b_ref, o_ref, kvb_ref):
    res = jnp.dot(x_ref[...], w_ref[...], preferred_element_type=f32) + b_ref[...]
    o_ref[...] = res
    kvb_ref[...] = res[:, C_CMP:C_SMALL].astype(bf16)


def _proj(xb, w, b, tm):
    n, k = xb.shape
    e = w.shape[1]
    return pl.pallas_call(
        _proj_kernel,
        grid=(n // tm,),
        in_specs=[pl.BlockSpec((tm, k), lambda i: (i, 0)), _full((k, e)), _full((1, e))],
        out_specs=[pl.BlockSpec((tm, e), lambda i: (i, 0)), pl.BlockSpec((tm, C_SMALL - C_CMP), lambda i: (i, 0))],
        out_shape=[jax.ShapeDtypeStruct((n, e), f32), jax.ShapeDtypeStruct((n, C_SMALL - C_CMP), bf16)],
        compiler_params=_cparams("parallel"),
        name="proj",
    )(xb, w, b)


def _proj_t_kernel(wt_ref, x_ref, b_ref, o_ref):
    o_ref[...] = lax.dot_general(wt_ref[...], x_ref[...], (((1,), (1,)), ((), ())),
                                 preferred_element_type=f32) + b_ref[...]


def _proj_t(wt, xb, bcol, tn):
    e, k = wt.shape
    n = xb.shape[0]
    return pl.pallas_call(
        _proj_t_kernel,
        grid=(n // tn,),
        in_specs=[_full((e, k)), pl.BlockSpec((tn, k), lambda i: (i, 0)), _full((e, 1))],
        out_specs=pl.BlockSpec((e, tn), lambda i: (0, i)),
        out_shape=jax.ShapeDtypeStruct((e, n), f32),
        compiler_params=_cparams("parallel"),
        name="proj_t",
    )(wt, xb, bcol)


def _proj_kvt_kernel(wt_ref, x_ref, b_ref, oc_ref, os_ref, ow_ref):
    res = lax.dot_general(wt_ref[...], x_ref[...], _NT_DIMS, preferred_element_type=f32) + b_ref[...]
    for i, o_ref in enumerate((oc_ref, os_ref, ow_ref)):
        o_ref[0] = res[i * KV_COLS:(i + 1) * KV_COLS]


def _proj_kvt(wt, xb, bcol, *, row0, nb, t, tn):
    e, k = wt.shape
    nt = t // tn
    rb0 = row0 // tn
    out = pl.BlockSpec((1, KV_COLS, tn), lambda b, i: (b, 0, i))
    return pl.pallas_call(
        _proj_kvt_kernel,
        grid=(nb, nt),
        in_specs=[_full((e, k)), pl.BlockSpec((tn, k), lambda b, i: (rb0 + b * nt + i, 0)), _full((e, 1))],
        out_specs=[out, out, out],
        out_shape=[jax.ShapeDtypeStruct((nb, KV_COLS, t), f32)] * 3,
        compiler_params=_cparams("parallel", "parallel"),
        name="proj_kvt",
    )(wt, xb, bcol)


def _mlstm_kernel(q_ref, k_ref, v_ref, ao_ref, g_ref, gt_ref, ng_ref, c0_ref, n0_ref, m0_ref,
                  y_ref, c_out, n_out, m_out, c_s, n_s, m_s, *, L, valid):
    h = pl.program_id(1)
    c = pl.program_id(2)

    @pl.when(c == 0)
    def _():
        c_s[...] = c0_ref[0, 0]
        n_s[...] = n0_ref[0, 0]
        m_s[...] = m0_ref[0, 0]

    q = q_ref[...]
    k = k_ref[...] * (A_DQK ** -0.5)
    v = v_ref[...]
    g = g_ref[...]
    gt = gt_ref[0]
    lane = lax.broadcasted_iota(jnp.int32, g.shape, 1)
    sub = lax.broadcasted_iota(jnp.int32, gt.shape, 0)
    i_col = jnp.sum(jnp.where(lane == G_I + h, g, 0.0), axis=1, keepdims=True)
    f_col = jnp.sum(jnp.where(lane == G_F + h, g, 0.0), axis=1, keepdims=True)
    i_row = jnp.sum(jnp.where(sub == G_I + h, gt, 0.0), axis=0, keepdims=True)
    f_row = jnp.sum(jnp.where(sub == G_F + h, gt, 0.0), axis=0, keepdims=True)
    lf_col = jax.nn.log_sigmoid(f_col)
    lf_row = jax.nn.log_sigmoid(f_row)
    t_col = lax.broadcasted_iota(jnp.int32, (L, 1), 0)
    s_row = lax.broadcasted_iota(jnp.int32, (1, L), 1)
    if valid < L:
        lf_col = jnp.where(t_col < valid, lf_col, 0.0)
        lf_row = jnp.where(s_row < valid, lf_row, 0.0)
        i_col = jnp.where(t_col < valid, i_col, NEG)
        i_row = jnp.where(s_row < valid, i_row, NEG)
    tt = lax.broadcasted_iota(jnp.int32, (L, L), 0)
    ss = lax.broadcasted_iota(jnp.int32, (L, L), 1)
    causal = ss <= tt
    b_col = jnp.sum(jnp.where(causal, lf_row, 0.0), axis=1, keepdims=True)
    b_row = jnp.sum(jnp.where(tt <= ss, lf_col, 0.0), axis=0, keepdims=True)
    m_prev = m_s[...]
    cmat = c_s[...]
    n_row = n_s[...]

    d_log = jnp.where(causal, b_col - b_row + i_row, NEG)
    inter = b_col + m_prev
    m_t = jnp.maximum(inter, jnp.max(d_log, axis=1, keepdims=True))
    qb = q.astype(bf16)
    qk = lax.dot_general(qb, k.astype(bf16), (((1,), (1,)), ((), ())), preferred_element_type=f32)
    smat = qk * jnp.exp(d_log - m_t)
    w_inter = jnp.exp(inter - m_t)
    vb = v.astype(bf16)
    num = (w_inter * jnp.dot(qb, cmat.astype(bf16), preferred_element_type=f32)
           + jnp.dot(smat.astype(bf16), vb, preferred_element_type=f32))
    den = w_inter * jnp.sum(q * n_row, axis=1, keepdims=True) + jnp.sum(smat, axis=1, keepdims=True)
    hid = num / jnp.maximum(jnp.abs(den), jnp.exp(-m_t))
    mu = jnp.mean(hid, axis=1, keepdims=True)
    var = jnp.mean(jnp.square(hid - mu), axis=1, keepdims=True)
    hid = (hid - mu) * lax.rsqrt(var + LN_EPS) * ng_ref[...]
    y_ref[...] = hid * jax.nn.sigmoid(ao_ref[...])

    b_end = b_col[L - 1:L, :]
    g_row = b_end - b_row + i_row
    m_new = jnp.maximum(b_end + m_prev, jnp.max(g_row, axis=1, keepdims=True))
    a = jnp.exp(b_end + m_prev - m_new)
    w_col = jnp.exp(b_end - b_col + i_col - m_new)
    kw = k * w_col
    c_new = a * cmat + lax.dot_general(kw.astype(bf16), vb, (((0,), (0,)), ((), ())),
                                       preferred_element_type=f32)
    n_new = a * n_row + jnp.sum(kw, axis=0, keepdims=True)
    c_s[...] = c_new
    n_s[...] = n_new
    m_s[...] = m_new

    @pl.when(c == pl.num_programs(2) - 1)
    def _():
        c_out[0, 0] = c_new
        n_out[0, 0] = n_new
        m_out[0, 0] = m_new


def _into(buf, kern):
    if buf is None:
        return kern, [], [], {}
    return (lambda buf_ref, *refs: kern(*refs)), [pl.BlockSpec(memory_space=pl.ANY)], [buf], {0: 0}


def _mlstm(proj, gt, norm_g, c0, n0, m0, *, row0, nb, t, L, valid, y_buf=None):
    nc = t // L
    rb0 = row0 // L
    gt = gt[:8, row0:row0 + nb * t].reshape(8, nb * nc, L).transpose(1, 0, 2)
    rows = lambda b, h, c: rb0 + b * nc + c
    st = lambda b, h, c: (b, h, 0, 0)
    kern, alias_specs, alias_args, aliases = _into(y_buf, functools.partial(_mlstm_kernel, L=L, valid=valid))
    y, c_f, n_f, m_f = pl.pallas_call(
        kern,
        grid=(nb, A_HEADS, nc),
        input_output_aliases=aliases,
        in_specs=alias_specs + [
            pl.BlockSpec((L, A_DQK), lambda b, h, c: (rows(b, h, c), C_AQ // A_DQK + h)),
            pl.BlockSpec((L, A_DQK), lambda b, h, c: (rows(b, h, c), C_AK // A_DQK + h)),
            pl.BlockSpec((L, A_DV), lambda b, h, c: (rows(b, h, c), C_AV // A_DV + h)),
            pl.BlockSpec((L, A_DV), lambda b, h, c: (rows(b, h, c), C_AO // A_DV + h)),
            pl.BlockSpec((L, LANES), lambda b, h, c: (rows(b, h, c), C_SMALL // LANES)),
            pl.BlockSpec((1, 8, L), lambda b, h, c: (b * nc + c, 0, 0)),
            pl.BlockSpec((1, A_DV), lambda b, h, c: (0, h)),
            pl.BlockSpec((1, 1, A_DQK, A_DV), st),
            pl.BlockSpec((1, 1, 1, A_DQK), st),
            pl.BlockSpec((1, 1, 1, 1), st),
        ],
        out_specs=[
            pl.BlockSpec((L, A_DV), lambda b, h, c: (rows(b, h, c), h)),
            pl.BlockSpec((1, 1, A_DQK, A_DV), st),
            pl.BlockSpec((1, 1, 1, A_DQK), st),
            pl.BlockSpec((1, 1, 1, 1), st),
        ],
        out_shape=[
            jax.ShapeDtypeStruct((proj.shape[0], A_HEADS * A_DV), f32),
            jax.ShapeDtypeStruct((nb, A_HEADS, A_DQK, A_DV), f32),
            jax.ShapeDtypeStruct((nb, A_HEADS, 1, A_DQK), f32),
            jax.ShapeDtypeStruct((nb, A_HEADS, 1, 1), f32),
        ],
        scratch_shapes=[pltpu.VMEM((A_DQK, A_DV), f32), pltpu.VMEM((1, A_DQK), f32), pltpu.VMEM((1, 1), f32)],
        compiler_params=_cparams("parallel", "parallel", "arbitrary"),
        name="mlstm",
    )(*alias_args, proj, proj, proj, proj, proj, gt, norm_g, c0, n0, m0)
    return y, c_f, n_f[:, :, 0], m_f[:, :, 0, 0]


def _compress_kernel(x_ref, pe_ref, w_ref, o_ref, xf_ref, *, nblk):
    for l in range(CMP_BLOCK):
        xf_ref[:, l * LANES:(l + 1) * LANES] = x_ref[pl.ds(l, nblk, stride=CMP_BLOCK), :]
    xf = (xf_ref[...] + pe_ref[0]).astype(bf16)
    o_ref[...] = jnp.dot(xf, w_ref[0], preferred_element_type=f32)


def _compress(x2, pe2, w2, *, rows, steps, row0, colblk):
    nblk = rows // CMP_BLOCK
    kflat = CMP_BLOCK * LANES
    rb0 = row0 // rows
    ngrp = KV_COLS // LANES
    return pl.pallas_call(
        functools.partial(_compress_kernel, nblk=nblk),
        grid=(steps, ngrp),
        in_specs=[pl.BlockSpec((rows, LANES), lambda s, p: (rb0 + s, colblk * ngrp + p)),
                  pl.BlockSpec((1, 1, kflat), lambda s, p: (p // 2, 0, 0)),
                  pl.BlockSpec((1, kflat, LANES), lambda s, p: (p // 2, 0, 0))],
        out_specs=pl.BlockSpec((nblk, LANES), lambda s, p: (s, p)),
        out_shape=jax.ShapeDtypeStruct((steps * nblk, KV_COLS), f32),
        scratch_shapes=[pltpu.VMEM((nblk, kflat), f32)],
        compiler_params=_cparams("parallel", "parallel"),
        name="compress",
    )(x2, pe2, w2)


def _gate_col(g, lane_idx):
    lane = lax.broadcasted_iota(jnp.int32, g.shape, 1)
    return jax.nn.sigmoid(jnp.sum(jnp.where(lane == lane_idx, g, 0.0), axis=1, keepdims=True))


def _cmp_topk_kernel(slope_ref, q_ref, kc_ref, vc_ref, g_ref, o_ref, sel_ref, *, tq, nblk, qpos0):
    gi = pl.program_id(1)
    i = pl.program_id(2)
    qpos = qpos0 + i * tq + lax.broadcasted_iota(jnp.int32, (tq, 1), 0)
    j = lax.broadcasted_iota(jnp.int32, (1, nblk), 1)
    dist = qpos - ((j + 1) * CMP_BLOCK - 1)
    valid = dist >= 0
    distf = dist.astype(f32)
    kc = kc_ref[0, 0].astype(bf16)
    vc = vc_ref[0, 0].astype(bf16)
    g = g_ref[...]
    imp = jnp.zeros((tq, nblk), f32)
    for r in range(B_REP):
        qr = q_ref[0, 0, 0, r * tq:(r + 1) * tq, :]
        s = lax.dot_general(qr, kc, (((1,), (1,)), ((), ())), preferred_element_type=f32) * (B_HD ** -0.5)
        s = s - slope_ref[gi * B_REP + r] * distf
        s = jnp.where(valid, s, NEG)
        e = jnp.exp(s - jnp.max(s, axis=1, keepdims=True))
        p = jnp.where(valid, e / jnp.sum(e, axis=1, keepdims=True), 0.0)
        imp = imp + p
        o = jnp.dot(p.astype(bf16), vc, preferred_element_type=f32)
        o_ref[0, 0, 0, r * tq:(r + 1) * tq, :] = o * _gate_col(g, G_GATE + gi * B_REP + r)
    cur = qpos // CMP_BLOCK
    imp = jnp.where((j == cur) | (j == 0), float(B_REP + 1), imp)
    imp = jnp.where(j > cur, -1.0, imp)
    jf = j.astype(f32)
    sel = jnp.zeros((tq, nblk), f32)
    for _ in range(N_SEL):
        mx = jnp.max(imp, axis=1, keepdims=True)
        idx = jnp.min(jnp.where(imp == mx, jf, float(nblk)), axis=1, keepdims=True)
        hit = jf == idx
        sel = jnp.where(hit, 1.0, sel)
        imp = jnp.where(hit, NEG, imp)
    sel_ref[0, 0] = sel


def _cmp_topk(slopes, qt, kc, vc, gsmall, *, tq, qpos0, row0):
    nb, _, nqt, _, _ = qt.shape
    nblk = kc.shape[2]
    rb0 = row0 // tq
    return pl.pallas_call(
        functools.partial(_cmp_topk_kernel, tq=tq, nblk=nblk, qpos0=qpos0),
        grid=(nb, B_KV, nqt),
        in_specs=[
            pl.BlockSpec(memory_space=pltpu.SMEM),
            pl.BlockSpec((1, 1, 1, B_REP * tq, B_HD), lambda b, g, i: (b, g, i, 0, 0)),
            pl.BlockSpec((1, 1, nblk, B_HD), lambda b, g, i: (b, g, 0, 0)),
            pl.BlockSpec((1, 1, nblk, B_HD), lambda b, g, i: (b, g, 0, 0)),
            pl.BlockSpec((tq, LANES), lambda b, g, i: (rb0 + b * nqt + i, 0)),
        ],
        out_specs=[
            pl.BlockSpec((1, 1, 1, B_REP * tq, B_HD), lambda b, g, i: (b, g, i, 0, 0)),
            pl.BlockSpec((1, 1, tq, nblk), lambda b, g, i: (b, g, i, 0)),
        ],
        out_shape=[
            jax.ShapeDtypeStruct(qt.shape, f32),
            jax.ShapeDtypeStruct((nb, B_KV, nqt * tq, nblk), f32),
        ],
        compiler_params=_cparams("parallel", "parallel", "parallel"),
        name="cmp_topk",
    )(slopes, qt, kc, vc, gsmall)


def _attn_kernel(slope_ref, q_ref, k_ref, v_ref, g_ref, *rest, tq, tk, nkt, nblk, qpos0, kpos0, window,
                 gate_lane):
    if window is None:
        sel_ref, o_ref, m_s, l_s, acc_s = rest
    else:
        o_ref, m_s, l_s, acc_s = rest
    gi = pl.program_id(1)
    i = pl.program_id(2)
    qlo = qpos0 + i * tq
    qpos = qlo + lax.broadcasted_iota(jnp.int32, (tq, 1), 0)
    m_s[...] = jnp.full(m_s.shape, NEG, f32)
    l_s[...] = jnp.zeros(l_s.shape, f32)
    acc_s[...] = jnp.zeros(acc_s.shape, f32)
    kt_hi = jnp.minimum((qlo + tq - 1 - kpos0) // tk + 1, nkt)
    if window is None:
        kt_lo = 0
        selb = sel_ref[0, 0].astype(bf16)
    else:
        kt_lo = jnp.maximum(qlo - (window - 1) - kpos0, 0) // tk

    def body(kt, carry):
        k0 = pl.multiple_of(kt * tk, tk)
        kb = k_ref[0, 0, pl.ds(k0, tk), :]
        vb = v_ref[0, 0, pl.ds(k0, tk), :]
        kidx = k0 + lax.broadcasted_iota(jnp.int32, (1, tk), 1)
        dist = qpos - (kpos0 + kidx)
        mask = dist >= 0
        if window is None:
            blk = lax.broadcasted_iota(jnp.int32, (nblk, tk), 0)
            kblk = (k0 + lax.broadcasted_iota(jnp.int32, (nblk, tk), 1)) // CMP_BLOCK
            expand = jnp.where(blk == kblk, 1.0, 0.0).astype(bf16)
            mask = mask & (jnp.dot(selb, expand, preferred_element_type=f32) > 0.5)
        else:
            mask = mask & (dist < window)
        distf = dist.astype(f32)
        for r in range(B_REP):
            rows = slice(r * tq, (r + 1) * tq)
            s = lax.dot_general(q_ref[0, 0, 0, rows, :], kb, (((1,), (1,)), ((), ())),
                                preferred_element_type=f32) * (B_HD ** -0.5)
            s = jnp.where(mask, s - slope_ref[gi * B_REP + r] * distf, NEG)
            m_old = m_s[rows, :]
            m_new = jnp.maximum(m_old, jnp.max(s, axis=1, keepdims=True))
            alpha = jnp.exp(m_old - m_new)
            p = jnp.exp(s - m_new)
            l_s[rows, :] = alpha * l_s[rows, :] + jnp.sum(p, axis=1, keepdims=True)
            acc_s[rows, :] = alpha * acc_s[rows, :] + jnp.dot(p.astype(bf16), vb, preferred_element_type=f32)
            m_s[rows, :] = m_new
        return carry

    lax.fori_loop(kt_lo, kt_hi, body, 0)
    g = g_ref[...]
    for r in range(B_REP):
        rows = slice(r * tq, (r + 1) * tq)
        o_ref[0, 0, 0, rows, :] = acc_s[rows, :] / l_s[rows, :] * _gate_col(g, gate_lane + gi * B_REP + r)


def _attn(slopes, qt, kh, vh, gsmall, sel, *, tq, tk, qpos0, kpos0, window, gate_lane, row0):
    nb, _, nqt, _, _ = qt.shape
    tkk = kh.shape[2]
    nkt = tkk // tk
    nblk = None if sel is None else sel.shape[3]
    rb0 = row0 // tq
    in_specs = [
        pl.BlockSpec(memory_space=pltpu.SMEM),
        pl.BlockSpec((1, 1, 1, B_REP * tq, B_HD), lambda b, g, i: (b, g, i, 0, 0)),
        pl.BlockSpec((1, 1, tkk, B_HD), lambda b, g, i: (b, g, 0, 0)),
        pl.BlockSpec((1, 1, tkk, B_HD), lambda b, g, i: (b, g, 0, 0)),
        pl.BlockSpec((tq, LANES), lambda b, g, i: (rb0 + b * nqt + i, 0)),
    ]
    args = [slopes, qt, kh, vh, gsmall]
    if sel is not None:
        in_specs.append(pl.BlockSpec((1, 1, tq, nblk), lambda b, g, i: (b, g, i, 0)))
        args.append(sel)
    return pl.pallas_call(
        functools.partial(_attn_kernel, tq=tq, tk=tk, nkt=nkt, nblk=nblk, qpos0=qpos0, kpos0=kpos0,
                          window=window, gate_lane=gate_lane),
        grid=(nb, B_KV, nqt),
        in_specs=in_specs,
        out_specs=pl.BlockSpec((1, 1, 1, B_REP * tq, B_HD), lambda b, g, i: (b, g, i, 0, 0)),
        out_shape=jax.ShapeDtypeStruct(qt.shape, f32),
        scratch_shapes=[pltpu.VMEM((B_REP * tq, 1), f32), pltpu.VMEM((B_REP * tq, 1), f32),
                        pltpu.VMEM((B_REP * tq, B_HD), f32)],
        compiler_params=_cparams("parallel", "parallel", "parallel"),
        name="attn_sel" if window is None else "attn_win",
    )(*args)


def _roll_lanes(x, shift):
    return x if shift == 0 else pltpu.roll(x, shift, axis=1)


def _softmax_rows(s, mask):
    s = jnp.where(mask, s, NEG)
    e = jnp.exp(s - jnp.max(s, axis=1, keepdims=True))
    return jnp.where(mask, e, 0.0), jnp.sum(e, axis=1, keepdims=True)


def _decode_kernel(slope_ref, q_ref, g_ref, kvc_ref, pages, slc_new_ref, win_ref, win_new_ref, exp_ref,
                   oc_ref, os_ref, ow_ref, kt_s, vt_s, wkt_s, wvt_s, *, tq, nblk, qpos0, win_kpos0):
    nrow = B_HEADS * tq
    q = q_ref[...]
    gs = g_ref[...]
    lane_grp = lax.broadcasted_iota(jnp.int32, (tq, B_KV * B_HD), 1) // B_HD
    qm = []
    for g in range(B_KV):
        qg = q[:, g * B_KV * B_HD:(g + 1) * B_KV * B_HD]
        for r in range(B_REP):
            qm.append(jnp.where(lane_grp == g, _roll_lanes(qg, ((g - r) % B_REP) * B_HD), 0.0))
    qm = jnp.concatenate(qm, axis=0).astype(bf16)
    row = lax.broadcasted_iota(jnp.int32, (nrow, 1), 0)
    qpos = qpos0 + row % tq
    slope = functools.reduce(lambda acc, h: jnp.where(row // tq == h, slope_ref[h], acc), range(B_HEADS),
                             jnp.zeros((nrow, 1), f32))
    scale = B_HD ** -0.5
    nt = (((1,), (1,)), ((), ()))

    def emit(o_ref, o, branch):
        for g in range(B_KV):
            acc = jnp.zeros((tq, B_KV * B_HD), f32)
            for r in range(B_REP):
                h = g * B_REP + r
                gate = jax.nn.sigmoid(gs[:, G_GATE + branch * B_HEADS + h:G_GATE + branch * B_HEADS + h + 1])
                oh = jnp.where(lane_grp == g, o[h * tq:(h + 1) * tq, :] * gate, 0.0)
                acc = acc + _roll_lanes(oh, ((r - g) % B_REP) * B_HD)
            o_ref[:, g * B_KV * B_HD:(g + 1) * B_KV * B_HD] = acc

    kvc = kvc_ref[0]
    j = lax.broadcasted_iota(jnp.int32, (1, nblk), 1)
    dist = qpos - ((j + 1) * CMP_BLOCK - 1)
    s = lax.dot_general(qm, kvc[:, :B_KV * B_HD].astype(bf16), nt, preferred_element_type=f32) * scale
    e, l = _softmax_rows(s - slope * dist.astype(f32), dist >= 0)
    p = e / l
    emit(oc_ref, jnp.dot(p.astype(bf16), kvc[:, B_KV * B_HD:].astype(bf16), preferred_element_type=f32), 0)
    imp = jnp.concatenate(
        [functools.reduce(lambda a, b: a + b, [p[(g * B_REP + r) * tq:(g * B_REP + r + 1) * tq] for r in range(B_REP)])
         for g in range(B_KV)], axis=0)
    cur = (qpos0 + lax.broadcasted_iota(jnp.int32, (B_KV * tq, 1), 0) % tq) // CMP_BLOCK
    imp = jnp.where((j == cur) | (j == 0), float(B_REP + 1), imp)
    imp = jnp.where(j > cur, -1.0, imp)
    jf = j.astype(f32)
    sel = jnp.zeros(imp.shape, f32)
    for _ in range(N_SEL):
        mx = jnp.max(imp, axis=1, keepdims=True)
        idx = jnp.min(jnp.where(imp == mx, jf, float(nblk)), axis=1, keepdims=True)
        hit = jf == idx
        sel = jnp.where(hit, 1.0, sel)
        imp = jnp.where(hit, NEG, imp)
    sel_rows = jnp.concatenate([sel[g * tq:(g + 1) * tq] for g in range(B_KV) for _ in range(B_REP)], axis=0)

    def transposed_kv(parts, kt_s, vt_s):
        for u, part in enumerate(parts):
            w = part.shape[-1]
            kt_s[:, u * w:(u + 1) * w] = part[0, :GRP_LANES, :].astype(bf16)
            vt_s[:, u * w:(u + 1) * w] = part[0, GRP_LANES:, :].astype(bf16)

    def attend(kt_s, vt_s, mask):
        s = jnp.dot(qm, kt_s[...], preferred_element_type=f32) * scale
        e, l = _softmax_rows(s - slope * dist.astype(f32), mask)
        return lax.dot_general(e.astype(bf16), vt_s[...], nt, preferred_element_type=f32) / l

    transposed_kv(list(pages) + [slc_new_ref], kt_s, vt_s)
    dist = qpos - lax.broadcasted_iota(jnp.int32, (1, kt_s.shape[1]), 1)
    picked = jnp.dot(sel_rows.astype(bf16), exp_ref[...], preferred_element_type=f32) > 0.5
    emit(os_ref, attend(kt_s, vt_s, picked & (dist >= 0)), 1)

    wb = win_ref.shape[-1]
    wkt_s[:, :wb] = win_ref[0, :GRP_LANES, :].astype(bf16)
    wvt_s[:, :wb] = win_ref[0, GRP_LANES:, :].astype(bf16)
    wkt_s[:, wb:] = win_new_ref[0, :GRP_LANES, :].astype(bf16)
    wvt_s[:, wb:] = win_new_ref[0, GRP_LANES:, :].astype(bf16)
    dist = qpos - (win_kpos0 + lax.broadcasted_iota(jnp.int32, (1, wkt_s.shape[1]), 1))
    emit(ow_ref, attend(wkt_s, wvt_s, (dist >= 0) & (dist < WINDOW)), 2)


def _decode(page_table, bufs, slopes, proj, kvc, slc_cache_t, slc_new_t, win_t, win_new_t, expand, *,
            tq, row0, qpos0, win_kpos0):
    nb, n_pages = page_table.shape
    nblk = kvc.shape[1]
    rb0 = row0 // tq
    tk = (n_pages + 1) * PAGE_SIZE
    twin = win_t.shape[-1] + win_new_t.shape[-1]
    out = pl.BlockSpec((tq, B_HEADS * B_HD), lambda b, pt: (rb0 + b, 0))
    page = (1, KV_COLS, PAGE_SIZE)

    def body(pt_ref, oc_buf, os_buf, ow_buf, slope_ref, q_ref, g_ref, kvc_ref, *refs):
        _decode_kernel(slope_ref, q_ref, g_ref, kvc_ref, refs[:n_pages], *refs[n_pages:], tq=tq, nblk=nblk,
                       qpos0=qpos0, win_kpos0=win_kpos0)

    return pl.pallas_call(
        body,
        grid_spec=pltpu.PrefetchScalarGridSpec(
            num_scalar_prefetch=1,
            grid=(nb,),
            in_specs=[pl.BlockSpec(memory_space=pl.ANY)] * 3
            + [pl.BlockSpec(memory_space=pltpu.SMEM),
               pl.BlockSpec((tq, B_HEADS * B_HD), lambda b, pt: (rb0 + b, C_BQ // (B_HEADS * B_HD))),
               pl.BlockSpec((tq, LANES), lambda b, pt: (rb0 + b, C_SMALL // LANES)),
               pl.BlockSpec((1,) + kvc.shape[1:], lambda b, pt: (b, 0, 0))]
            + [pl.BlockSpec(page, functools.partial(lambda b, pt, u: (pt[b, u], 0, 0), u=u)) for u in range(n_pages)]
            + [pl.BlockSpec((1,) + slc_new_t.shape[1:], lambda b, pt: (b, 0, 0)),
               pl.BlockSpec((1,) + win_t.shape[1:], lambda b, pt: (b, 0, 0)),
               pl.BlockSpec((1,) + win_new_t.shape[1:], lambda b, pt: (b, 0, 0)),
               pl.BlockSpec(expand.shape, lambda b, pt: (0, 0))],
            out_specs=[out, out, out],
            scratch_shapes=[pltpu.VMEM((GRP_LANES, tk), bf16), pltpu.VMEM((GRP_LANES, tk), bf16),
                            pltpu.VMEM((GRP_LANES, twin), bf16), pltpu.VMEM((GRP_LANES, twin), bf16)],
        ),
        out_shape=[jax.ShapeDtypeStruct(b.shape, f32) for b in bufs],
        input_output_aliases={1: 0, 2: 1, 3: 2},
        compiler_params=_cparams("parallel"),
        name="nsa_decode",
    )(page_table, *bufs, slopes, proj, proj, kvc, *([slc_cache_t] * n_pages), slc_new_t, win_t, win_new_t, expand)


GRP_LANES = B_KV * B_HD
_NT = (((1,), (1,)), ((), ()))


def _masked_queries(q, tq):
    lane_grp = lax.broadcasted_iota(jnp.int32, (tq, GRP_LANES), 1) // B_HD
    rows = []
    for g in range(B_KV):
        qg = q[:, g * GRP_LANES:(g + 1) * GRP_LANES]
        for r in range(B_REP):
            rows.append(jnp.where(lane_grp == g, _roll_lanes(qg, ((g - r) % B_REP) * B_HD), 0.0))
    return jnp.concatenate(rows, axis=0).astype(bf16)


def _group_columns(slope_ref, g, tq, qlo):
    row = lax.broadcasted_iota(jnp.int32, (B_REP * tq, 1), 0)
    slope = functools.reduce(lambda acc, r: jnp.where(row // tq == r, slope_ref[g * B_REP + r], acc), range(B_REP),
                             jnp.zeros((B_REP * tq, 1), f32))
    return qlo + row % tq, slope


def _emit_group(o_ref, og, gs, branch, g, tq):
    lane_grp = lax.broadcasted_iota(jnp.int32, (tq, GRP_LANES), 1) // B_HD
    acc = jnp.zeros((tq, GRP_LANES), f32)
    for r in range(B_REP):
        c = G_GATE + branch * B_HEADS + g * B_REP + r
        oh = jnp.where(lane_grp == g, og[r * tq:(r + 1) * tq, :] * jax.nn.sigmoid(gs[:, c:c + 1]), 0.0)
        acc = acc + _roll_lanes(oh, ((r - g) % B_REP) * B_HD)
    o_ref[:, g * GRP_LANES:(g + 1) * GRP_LANES] = acc


def _cmp_nat_kernel(slope_ref, q_ref, g_ref, kvc_ref, o_ref, sel_ref, *, tq, nblk):
    qlo = pl.program_id(1) * tq
    qm = _masked_queries(q_ref[...], tq)
    gs = g_ref[...]
    kvc = kvc_ref[0]
    kc = kvc[:, :GRP_LANES].astype(bf16)
    vc = kvc[:, GRP_LANES:].astype(bf16)
    j = lax.broadcasted_iota(jnp.int32, (1, nblk), 1)
    jf = j.astype(f32)
    imps = []
    for g in range(B_KV):
        qpos, slope = _group_columns(slope_ref, g, tq, qlo)
        dist = qpos - ((j + 1) * CMP_BLOCK - 1)
        s = lax.dot_general(qm[g * B_REP * tq:(g + 1) * B_REP * tq], kc, _NT, preferred_element_type=f32) * (B_HD ** -0.5)
        e, l = _softmax_rows(s - slope * dist.astype(f32), dist >= 0)
        p = e / l
        _emit_group(o_ref, jnp.dot(p.astype(bf16), vc, preferred_element_type=f32), gs, 0, g, tq)
        imps.append(functools.reduce(lambda a, b: a + b, [p[r * tq:(r + 1) * tq] for r in range(B_REP)]))
    imp = jnp.concatenate(imps, axis=0)
    cur = (qlo + lax.broadcasted_iota(jnp.int32, (B_KV * tq, 1), 0) % tq) // CMP_BLOCK
    imp = jnp.where((j == cur) | (j == 0), float(B_REP + 1), imp)
    imp = jnp.where(j > cur, -1.0, imp)
    sel = jnp.zeros(imp.shape, f32)
    for _ in range(N_SEL):
        mx = jnp.max(imp, axis=1, keepdims=True)
        idx = jnp.min(jnp.where(imp == mx, jf, float(nblk)), axis=1, keepdims=True)
        hit = jf == idx
        sel = jnp.where(hit, 1.0, sel)
        imp = jnp.where(hit, NEG, imp)
    sel_ref[0, 0] = sel


def _cmp_nat(slopes, proj, kvc, *, nb, t, tq):
    nqt = t // tq
    nblk = kvc.shape[1]
    return pl.pallas_call(
        functools.partial(_cmp_nat_kernel, tq=tq, nblk=nblk),
        grid=(nb, nqt),
        in_specs=[pl.BlockSpec(memory_space=pltpu.SMEM),
                  pl.BlockSpec((tq, B_HEADS * B_HD), lambda b, i: (b * nqt + i, C_BQ // (B_HEADS * B_HD))),
                  pl.BlockSpec((tq, LANES), lambda b, i: (b * nqt + i, C_SMALL // LANES)),
                  pl.BlockSpec((1, nblk, KV_COLS), lambda b, i: (b, 0, 0))],
        out_specs=[pl.BlockSpec((tq, B_HEADS * B_HD), lambda b, i: (b * nqt + i, 0)),
                   pl.BlockSpec((1, 1, B_KV * tq, nblk), lambda b, i: (b, i, 0, 0))],
        out_shape=[jax.ShapeDtypeStruct((proj.shape[0], B_HEADS * B_HD), f32),
                   jax.ShapeDtypeStruct((nb, nqt, B_KV * tq, nblk), f32)],
        compiler_params=_cparams("parallel", "parallel"),
        name="nsa_cmp",
    )(slopes, proj, proj, kvc)


def _sel_nat_kernel(slope_ref, q_ref, g_ref, kv_ref, sel_ref, o_ref, qm_s, m_s, l_s, acc_s, *, tq, tk, nblk):
    qlo = pl.program_id(1) * tq
    qm_s[...] = _masked_queries(q_ref[...] * (B_HD ** -0.5), tq)
    m_s[...] = jnp.full(m_s.shape, NEG, f32)
    l_s[...] = jnp.zeros(l_s.shape, f32)
    acc_s[...] = jnp.zeros(acc_s.shape, f32)
    selb = sel_ref[0, 0].astype(bf16)
    qpos = qlo + lax.broadcasted_iota(jnp.int32, (tq, 1), 0)
    grows = B_REP * tq

    def body(kt, carry):
        k0 = pl.multiple_of(kt * tk, tk)
        kb = kv_ref[pl.ds(k0, tk), :GRP_LANES]
        vb = kv_ref[pl.ds(k0, tk), GRP_LANES:]
        dist = qpos - (k0 + lax.broadcasted_iota(jnp.int32, (1, tk), 1))
        distf = dist.astype(f32)
        blk = lax.broadcasted_iota(jnp.int32, (nblk, tk), 0)
        kblk = (k0 + lax.broadcasted_iota(jnp.int32, (nblk, tk), 1)) // CMP_BLOCK
        picked = jnp.dot(selb, jnp.where(blk == kblk, 1.0, 0.0).astype(bf16), preferred_element_type=f32)
        for g in range(B_KV):
            sg = lax.dot_general(qm_s[g * grows:(g + 1) * grows, :], kb, _NT, preferred_element_type=f32)
            amask = jnp.where((picked[g * tq:(g + 1) * tq] > 0.5) & (dist >= 0), 0.0, NEG)
            ps, alphas = [], []
            for r in range(B_REP):
                rows = slice((g * B_REP + r) * tq, (g * B_REP + r + 1) * tq)
                s = sg[r * tq:(r + 1) * tq] - slope_ref[g * B_REP + r] * distf + amask
                m_old = m_s[rows, :]
                m_new = jnp.maximum(m_old, jnp.max(s, axis=1, keepdims=True))
                alpha = jnp.exp(m_old - m_new)
                p = jnp.exp(s - m_new)
                l_s[rows, :] = alpha * l_s[rows, :] + jnp.sum(p, axis=1, keepdims=True)
                m_s[rows, :] = m_new
                ps.append(p.astype(bf16))
                alphas.append(alpha)
            grp = slice(g * grows, (g + 1) * grows)
            acc_s[grp, :] = (jnp.concatenate(alphas, axis=0) * acc_s[grp, :]
                             + jnp.dot(jnp.concatenate(ps, axis=0), vb, preferred_element_type=f32))
        return carry

    lax.fori_loop(0, (qlo + tq - 1) // tk + 1, body, 0)
    gs = g_ref[...]
    for g in range(B_KV):
        rows = slice(g * grows, (g + 1) * grows)
        _emit_group(o_ref, acc_s[rows, :] / l_s[rows, :], gs, 1, g, tq)


def _win_nat_kernel(slope_ref, q_ref, g_ref, kv_ref, o_ref, *, tq, t):
    qlo = pl.program_id(1) * tq
    span = WINDOW + tq
    k0 = pl.multiple_of(jnp.clip(qlo - WINDOW, 0, t - span), LANES)
    qm = _masked_queries(q_ref[...] * (B_HD ** -0.5), tq)
    gs = g_ref[...]
    kb = kv_ref[pl.ds(k0, span), :GRP_LANES]
    vb = kv_ref[pl.ds(k0, span), GRP_LANES:]
    dist = qlo + lax.broadcasted_iota(jnp.int32, (tq, 1), 0) - (k0 + lax.broadcasted_iota(jnp.int32, (1, span), 1))
    distf = dist.astype(f32)
    amask = jnp.where((dist >= 0) & (dist < WINDOW), 0.0, NEG)
    grows = B_REP * tq
    for g in range(B_KV):
        sg = lax.dot_general(qm[g * grows:(g + 1) * grows], kb, _NT, preferred_element_type=f32)
        es, ls = [], []
        for r in range(B_REP):
            s = sg[r * tq:(r + 1) * tq] - slope_ref[g * B_REP + r] * distf + amask
            e = jnp.exp(s - jnp.max(s, axis=1, keepdims=True))
            es.append(e.astype(bf16))
            ls.append(jnp.sum(e, axis=1, keepdims=True))
        og = jnp.dot(jnp.concatenate(es, axis=0), vb, preferred_element_type=f32) / jnp.concatenate(ls, axis=0)
        _emit_group(o_ref, og, gs, 2, g, tq)


def _attn_nat(slopes, proj, kvb, sel, *, nb, t, tq, tk, branch):
    nqt = t // tq
    in_specs = [pl.BlockSpec(memory_space=pltpu.SMEM),
                pl.BlockSpec((tq, B_HEADS * B_HD), lambda b, i: (b * nqt + i, C_BQ // (B_HEADS * B_HD))),
                pl.BlockSpec((tq, LANES), lambda b, i: (b * nqt + i, C_SMALL // LANES)),
                pl.BlockSpec((t, KV_COLS), lambda b, i: (b, branch))]
    args = [slopes, proj, proj, kvb]
    if sel is None:
        body, scratch, name = functools.partial(_win_nat_kernel, tq=tq, t=t), [], "nsa_win"
    else:
        nblk = sel.shape[3]
        in_specs.append(pl.BlockSpec((1, 1, B_KV * tq, nblk), lambda b, i: (b, i, 0, 0)))
        args.append(sel)
        body = functools.partial(_sel_nat_kernel, tq=tq, tk=tk, nblk=nblk)
        scratch = [pltpu.VMEM((B_HEADS * tq, GRP_LANES), bf16), pltpu.VMEM((B_HEADS * tq, 1), f32),
                   pltpu.VMEM((B_HEADS * tq, 1), f32), pltpu.VMEM((B_HEADS * tq, GRP_LANES), f32)]
        name = "nsa_sel"
    return pl.pallas_call(
        body,
        grid=(nb, nqt),
        in_specs=in_specs,
        out_specs=pl.BlockSpec((tq, B_HEADS * B_HD), lambda b, i: (b * nqt + i, 0)),
        out_shape=jax.ShapeDtypeStruct((proj.shape[0], B_HEADS * B_HD), f32),
        scratch_shapes=scratch,
        compiler_params=_cparams("parallel", "parallel"),
        name=name,
    )(*args)


PAD_PAGES = 4


def _gather_kernel(pt_ref, *refs):
    del pt_ref
    pages, tail_ref, o_ref = refs[:-2], refs[-2], refs[-1]
    for u, page in enumerate(pages):
        o_ref[0, u] = page[0]
    o_ref[0, len(pages)] = tail_ref[0]
    for u in range(len(pages) + 1, len(pages) + PAD_PAGES):
        o_ref[0, u] = jnp.zeros(o_ref.shape[2:], f32)


def _gather_pages(page_table, cache, tail):
    nb, n_pages = page_table.shape
    page = (1, PAGE_SIZE, KV_COLS)
    return pl.pallas_call(
        _gather_kernel,
        grid_spec=pltpu.PrefetchScalarGridSpec(
            num_scalar_prefetch=1,
            grid=(nb,),
            in_specs=[pl.BlockSpec(page, functools.partial(lambda b, pt, u: (pt[b, u], 0, 0), u=u))
                      for u in range(n_pages)] + [pl.BlockSpec(page, lambda b, pt: (b, 0, 0))],
            out_specs=pl.BlockSpec((1, n_pages + PAD_PAGES, PAGE_SIZE, KV_COLS), lambda b, pt: (b, 0, 0, 0)),
        ),
        out_shape=jax.ShapeDtypeStruct((nb, n_pages + PAD_PAGES, PAGE_SIZE, KV_COLS), f32),
        compiler_params=_cparams("parallel"),
        name="gather_pages",
    )(page_table, *([cache] * n_pages), tail)


def _layer_norm(z, g, b):
    mu = jnp.mean(z, axis=1, keepdims=True)
    var = jnp.mean(jnp.square(z - mu), axis=1, keepdims=True)
    return (z - mu) * lax.rsqrt(var + LN_EPS) * g + b


def _tail_kernel(x_ref, ya_ref, oc_ref, os_ref, ow_ref, wm_ref, wa_ref, wb_ref, wo_ref, g_ref, b_ref,
                 h_ref, hb_ref):
    x = x_ref[...]
    gates = jax.nn.sigmoid(jnp.dot(x.astype(bf16), wm_ref[...], preferred_element_type=f32))
    yb = oc_ref[...] + os_ref[...] + ow_ref[...]
    ma = jnp.dot(ya_ref[...].astype(bf16), wa_ref[...], preferred_element_type=f32)
    mb = jnp.dot(yb.astype(bf16), wb_ref[...], preferred_element_type=f32)
    merged = gates[:, :D_MODEL] * ma + gates[:, D_MODEL:] * mb
    z = DN_ALPHA * x + jnp.dot(merged.astype(bf16), wo_ref[...], preferred_element_type=f32)
    h = _layer_norm(z, g_ref[...], b_ref[...])
    h_ref[...] = h
    hb_ref[...] = h.astype(bf16)


def _tail(x, ya, oc, os_, ow, wm, wa, wb, wo, g, b, tm):
    n = x.shape[0]
    row = pl.BlockSpec((tm, D_MODEL), lambda i: (i, 0))
    return pl.pallas_call(
        _tail_kernel,
        grid=(n // tm,),
        in_specs=[row] * 5 + [_full(wm.shape), _full(wa.shape), _full(wb.shape), _full(wo.shape),
                              _full(g.shape), _full(b.shape)],
        out_specs=[row, row],
        out_shape=[jax.ShapeDtypeStruct((n, D_MODEL), f32), jax.ShapeDtypeStruct((n, D_MODEL), bf16)],
        compiler_params=_cparams("parallel"),
        name="tail",
    )(x, ya, oc, os_, ow, wm, wa, wb, wo, g, b)


def _top16(x):
    kk, tb = x.shape
    ji = lax.broadcasted_iota(jnp.int32, (kk, tb), 0).astype(f32)
    rank = jnp.full((kk, tb), float(P_TOPK), f32)
    vals = []
    for k in range(P_TOPK):
        mx = jnp.max(x, axis=0, keepdims=True)
        idx = jnp.min(jnp.where(x == mx, ji, float(kk)), axis=0, keepdims=True)
        hit = ji == idx
        rank = jnp.where(hit, float(k), rank)
        vals.append(mx)
        x = jnp.where(hit, NEG, x)
    return rank, vals


_CAND_ROWS8 = ((1, 8), (2, 5), (3, 4), (4, 3))


def _route_kernel(h_ref, wqt_ref, keys_ref, ta_ref, tb_ref):
    qpt = lax.dot_general(wqt_ref[...], h_ref[...], (((1,), (1,)), ((), ())), preferred_element_type=f32)
    tb = qpt.shape[1]
    sub16 = lax.broadcasted_iota(jnp.int32, (P_TOPK, tb), 0)
    sub8 = lax.broadcasted_iota(jnp.int32, (8, tb), 0)
    for p in range(P_HEADS):
        sc, rk, vl = [], [], []
        for c in range(2):
            qs = qpt[(2 * p + c) * P_DHALF:(2 * p + c + 1) * P_DHALF, :].astype(bf16)
            s = jnp.dot(keys_ref[p, c], qs, preferred_element_type=f32)
            r, v = _top16(s)
            sc.append(s)
            rk.append(r)
            vl.append(v)
        v0, v1 = vl
        col0 = functools.reduce(lambda acc, k: jnp.where(sub16 == k, v0[k], acc), range(P_TOPK), jnp.zeros((P_TOPK, tb), f32))
        col1 = functools.reduce(lambda acc, k: jnp.where(sub16 == k, v1[k], acc), range(P_TOPK), jnp.zeros((P_TOPK, tb), f32))
        segs = [v0[0] + col1]
        for k1, keep in _CAND_ROWS8:
            segs.append(jnp.where(sub8 < keep, v0[k1] + col1[0:8], NEG))
        first = jnp.where(sub8 < 2, v0[5], jnp.where(sub8 < 4, v0[6], v0[7]))
        second = jnp.where(sub8 % 2 == 0, v1[0], v1[1])
        segs.append(jnp.where(sub8 < 6, first + second, NEG))
        segs.append(col0[8:16] + v1[0])
        cand = jnp.concatenate(segs, axis=0)
        crank, cvals = _top16(cand)
        taken = jnp.where(crank < float(P_TOPK), 1.0, 0.0)
        z = functools.reduce(lambda acc, v: acc + jnp.exp(v - cvals[0]), cvals, jnp.zeros((1, tb), f32))
        cnt = [jnp.sum(taken[0:16], axis=0, keepdims=True)]
        for i in range(len(_CAND_ROWS8)):
            cnt.append(jnp.sum(taken[16 + 8 * i:24 + 8 * i], axis=0, keepdims=True))
        t5 = taken[48:56]
        for lo in (0, 2, 4):
            cnt.append(jnp.sum(jnp.where((sub8 >= lo) & (sub8 < lo + 2), t5, 0.0), axis=0, keepdims=True))
        for i in range(8):
            cnt.append(taken[56 + i:57 + i])
        n_a = functools.reduce(lambda acc, k: jnp.where(rk[0] == float(k), cnt[k], acc), range(P_TOPK),
                               jnp.zeros((P_NKEYS, tb), f32))
        ta_ref[p, 0] = n_a
        ta_ref[p, 1] = jnp.exp(sc[0] - v0[0])
        tb_ref[p, 0] = rk[1]
        tb_ref[p, 1] = jnp.exp(sc[1] - v1[0]) / z


def _route(hb, wqt, keys, tb):
    n = hb.shape[0]
    spec = pl.BlockSpec((P_HEADS, 2, P_NKEYS, tb), lambda i: (0, 0, 0, i))
    return pl.pallas_call(
        _route_kernel,
        grid=(n // tb,),
        in_specs=[pl.BlockSpec((tb, D_MODEL), lambda i: (i, 0)), _full(wqt.shape), _full(keys.shape)],
        out_specs=[spec, spec],
        out_shape=[jax.ShapeDtypeStruct((P_HEADS, 2, P_NKEYS, n), f32),
                   jax.ShapeDtypeStruct((P_HEADS, 2, P_NKEYS, n), f32)],
        compiler_params=_cparams("parallel"),
        name="peer_route",
    )(hb, wqt, keys)


def _experts_kernel(hb_ref, h_ref, ta_ref, tb_ref, u_ref, vt_ref, g_ref, b_ref, y_ref, acc_s, ht_s, pt_s, *, te):
    j = pl.program_id(1)

    @pl.when(j == 0)
    def _():
        acc_s[...] = jnp.zeros(acc_s.shape, f32)

    ht_s[...] = lax.dot_general(u_ref[...], hb_ref[...], (((1,), (1,)), ((), ())), preferred_element_type=f32)
    for aa in range(te // P_NKEYS):
        a = j * (te // P_NKEYS) + aa
        n_rows = [ta_ref[p, 0, pl.ds(a, 1), :] for p in range(P_HEADS)]
        e0_rows = [ta_ref[p, 1, pl.ds(a, 1), :] for p in range(P_HEADS)]
        for lt in range(ht_s.shape[1] // LANES):
            ls = slice(lt * LANES, (lt + 1) * LANES)
            w = jnp.zeros((P_NKEYS, LANES), f32)
            for p in range(P_HEADS):
                w = w + jnp.where(tb_ref[p, 0, :, ls] < n_rows[p][:, ls], e0_rows[p][:, ls] * tb_ref[p, 1, :, ls], 0.0)
            hs = ht_s[aa * P_NKEYS:(aa + 1) * P_NKEYS, ls]
            act = 0.5 * hs * (1.0 + lax.erf(hs * (0.5 ** 0.5)))
            pt_s[aa * P_NKEYS:(aa + 1) * P_NKEYS, ls] = (w * act).astype(bf16)
    acc_s[...] += jnp.dot(vt_ref[...], pt_s[...], preferred_element_type=f32)

    @pl.when(j == pl.num_programs(1) - 1)
    def _():
        z = DN_ALPHA * h_ref[...] + acc_s[...].T
        y_ref[...] = _layer_norm(z, g_ref[...], b_ref[...])


def _experts(hb, h, ta, tbl, u, vt, g, b, tb, te):
    n = hb.shape[0]
    row = pl.BlockSpec((tb, D_MODEL), lambda i, j: (i, 0))
    tab = pl.BlockSpec((P_HEADS, 2, P_NKEYS, tb), lambda i, j: (0, 0, 0, i))
    return pl.pallas_call(
        functools.partial(_experts_kernel, te=te),
        grid=(n // tb, P_EXPERTS // te),
        in_specs=[row, row, tab, tab,
                  pl.BlockSpec((te, D_MODEL), lambda i, j: (j, 0)),
                  pl.BlockSpec((D_MODEL, te), lambda i, j: (0, j)),
                  pl.BlockSpec((1, D_MODEL), lambda i, j: (0, 0)),
                  pl.BlockSpec((1, D_MODEL), lambda i, j: (0, 0))],
        out_specs=row,
        out_shape=jax.ShapeDtypeStruct((n, D_MODEL), f32),
        scratch_shapes=[pltpu.VMEM((D_MODEL, tb), f32), pltpu.VMEM((te, tb), f32), pltpu.VMEM((te, tb), bf16)],
        compiler_params=_cparams("parallel", "arbitrary"),
        name="peer_experts",
    )(hb, h, ta, tbl, u, vt, g, b)


def _to_q_tiles(q2, nb, t, tq):
    q = q2.reshape(nb, t // tq, tq, B_KV, B_REP, B_HD).transpose(0, 3, 1, 4, 2, 5)
    return q.reshape(nb, B_KV, t // tq, B_REP * tq, B_HD).astype(bf16)


def _from_q_tiles(o, nb, t, tq):
    o = o.reshape(nb, B_KV, t // tq, B_REP, tq, B_HD).transpose(0, 2, 4, 1, 3, 5)
    return o.reshape(nb, t, B_HEADS * B_HD)


def _kv_heads(kv3):
    nb, tk, _ = kv3.shape
    kv = kv3.reshape(nb, tk, 2, B_KV, B_HD).transpose(2, 0, 3, 1, 4)
    return kv[0], kv[1]


def _nsa(slopes, proj, gsmall, pe2, w2, *, nb, t, tq, row0, qpos0, cmp_src, slc3, win3, win_kpos0, tk_sel, tk_win):
    qt = _to_q_tiles(proj[row0:row0 + nb * t, C_BQ:C_CMP], nb, t, tq)
    x2, cmp_rows, cmp_steps, cmp_colblk = cmp_src
    kvc = _compress(x2, pe2, w2, rows=cmp_rows, steps=cmp_steps, row0=0, colblk=cmp_colblk)
    kc, vc = _kv_heads(kvc.reshape(nb, -1, KV_COLS))
    o_cmp, sel = _cmp_topk(slopes, qt, kc, vc, gsmall, tq=tq, qpos0=qpos0, row0=row0)
    ks, vs = _kv_heads(slc3)
    o_slc = _attn(slopes, qt, ks.astype(bf16), vs.astype(bf16), gsmall, sel, tq=tq, tk=tk_sel, qpos0=qpos0,
                  kpos0=0, window=None, gate_lane=G_GATE + B_HEADS, row0=row0)
    kw, vw = _kv_heads(win3)
    o_win = _attn(slopes, qt, kw.astype(bf16), vw.astype(bf16), gsmall, None, tq=tq, tk=tk_win, qpos0=qpos0,
                  kpos0=win_kpos0, window=WINDOW, gate_lane=G_GATE + 2 * B_HEADS, row0=row0)
    return tuple(_from_q_tiles(o, nb, t, tq).reshape(nb * t, B_HEADS * B_HD) for o in (o_cmp, o_slc, o_win))


def kernel(x_prompt, x_sample, cache_cmp_kv, cache_slc_kv, cache_win_kv, state_C, state_n, state_m, page_table,
           w_in, b_in, norm_a_g, nsa_pe, nsa_w_cmp, w_br_a, w_br_b, w_merge, w_out, ln1_g, ln1_b,
           peer_wq, peer_keys, peer_u, peer_v, ln2_g, ln2_b):
    bp, tp, _ = x_prompt.shape
    bs, ts, _ = x_sample.shape
    tsp = 8
    n_p, n_s = bp * tp, bs * tsp
    past = page_table.shape[1] * PAGE_SIZE

    perm = np.concatenate([np.arange(0, 2048), np.arange(2056, 5640), np.arange(2048, 2056), np.arange(5640, 5688)])
    w_perm = jnp.pad(w_in[:, perm], ((0, 0), (0, C_END - perm.size)))
    b_perm = jnp.pad(b_in[perm], (0, C_END - perm.size))
    w_perm_b = w_perm.astype(bf16)
    slopes = jnp.asarray(2.0 ** (-8.0 * np.arange(1, B_HEADS + 1) / B_HEADS), f32)
    wc = nsa_w_cmp.reshape(2, CMP_BLOCK, 1, B_HD, 1, B_HD)
    eye2 = jnp.eye(2, dtype=f32).reshape(1, 1, 2, 1, 2, 1)
    w2 = (wc * eye2).reshape(2, CMP_BLOCK * LANES, LANES).astype(bf16)
    pe2 = jnp.tile(nsa_pe, (1, 1, 2)).reshape(2, 1, CMP_BLOCK * LANES)

    xs_pad = jnp.pad(x_sample, ((0, 0), (0, tsp - ts), (0, 0)))
    x_all = jnp.concatenate([x_prompt.reshape(n_p, D_MODEL), xs_pad.reshape(n_s, D_MODEL)], axis=0)
    xb = x_all.astype(bf16)
    proj, kvb = _proj(xb, w_perm_b, b_perm.reshape(1, C_END), 256)
    gt = _proj_t(w_perm_b[:, C_SMALL:].T, xb, b_perm[C_SMALL:].reshape(LANES, 1), 512)

    zc = jnp.zeros((bp, A_HEADS, A_DQK, A_DV), f32)
    zn = jnp.zeros((bp, A_HEADS, 1, A_DQK), f32)
    zm = jnp.zeros((bp, A_HEADS, 1, 1), f32)
    ng = norm_a_g.reshape(1, A_HEADS * A_DV)
    ya, p_c, p_n, p_m = _mlstm(proj, gt, ng, zc, zn, zm, row0=0, nb=bp, t=tp, L=256, valid=256)
    ya, s_c, s_n, s_m = _mlstm(proj, gt, ng, state_C, state_n.reshape(bs, A_HEADS, 1, A_DQK),
                               state_m.reshape(bs, A_HEADS, 1, 1), row0=n_p, nb=bs, t=tsp, L=tsp, valid=ts, y_buf=ya)

    wt_kv = w_perm_b[:, C_CMP:C_SMALL].T
    b_kv = b_perm[C_CMP:C_SMALL].reshape(C_SMALL - C_CMP, 1)
    kvt_p = _proj_kvt(wt_kv, xb, b_kv, row0=0, nb=bp, t=tp, tn=512)
    kvt_s = _proj_kvt(wt_kv, xb, b_kv, row0=n_p, nb=1, t=n_s, tn=512)
    kvt_s = [a.reshape(KV_COLS, bs, tsp).transpose(1, 0, 2) for a in kvt_s]
    to_rows = lambda a: a.reshape(a.shape[0], 2, B_KV, B_HD, a.shape[2]).transpose(0, 4, 1, 2, 3)
    new_lanes = lambda a: jnp.pad(a, ((0, 0), (0, 0), (0, LANES - tsp)))

    kvc_p = _compress(proj, pe2, w2, rows=n_p, steps=1, row0=0, colblk=C_CMP // KV_COLS)
    oc, sel_p = _cmp_nat(slopes, proj, kvc_p.reshape(bp, tp // CMP_BLOCK, KV_COLS), nb=bp, t=tp, tq=128)
    os_ = _attn_nat(slopes, proj, kvb, sel_p, nb=bp, t=tp, tq=128, tk=512, branch=1)
    ow = _attn_nat(slopes, proj, kvb, None, nb=bp, t=tp, tq=128, tk=512, branch=2)

    n_pool = cache_cmp_kv.shape[0]
    s_cmp_rows = kvt_s[0][:, :, :ts].transpose(0, 2, 1)
    all_cmp = _gather_pages(page_table, cache_cmp_kv.reshape(n_pool, PAGE_SIZE, KV_COLS),
                            jnp.pad(s_cmp_rows, ((0, 0), (0, PAGE_SIZE - ts), (0, 0))))
    tk_s = past + PAD_PAGES * PAGE_SIZE
    seqs_per_step = 8
    kvc_s = _compress(all_cmp.reshape(bs * tk_s, KV_COLS), pe2, w2, rows=seqs_per_step * tk_s,
                      steps=bs // seqs_per_step, row0=0, colblk=0)
    wb = cache_win_kv.shape[1]
    cache_t = lambda c: c.transpose(0, 2, 3, 4, 1).reshape(c.shape[0], KV_COLS, c.shape[1])
    win_t = cache_t(cache_win_kv)
    s_win_t = jnp.concatenate([win_t[:, :, ts:], kvt_s[2][:, :, :ts]], axis=2)
    tk_sel = past + PAGE_SIZE
    nblk_s = tk_s // CMP_BLOCK
    expand = jnp.asarray(np.arange(tk_sel)[None, :] // CMP_BLOCK == np.arange(nblk_s)[:, None], bf16)
    oc, os_, ow = _decode(page_table, (oc, os_, ow), slopes, proj, kvc_s.reshape(bs, nblk_s, KV_COLS),
                          cache_t(cache_slc_kv), new_lanes(kvt_s[1]), win_t, new_lanes(kvt_s[2]), expand,
                          tq=tsp, row0=n_p, qpos0=past, win_kpos0=past - wb)

    h1, h1b = _tail(x_all, ya, oc, os_, ow, w_merge.astype(bf16), w_br_a.astype(bf16), w_br_b.astype(bf16),
                    w_out.astype(bf16), ln1_g.reshape(1, D_MODEL), ln1_b.reshape(1, D_MODEL), 256)
    tab_a, tab_b = _route(h1b, peer_wq.T.astype(bf16), peer_keys.astype(bf16), 256)
    y = _experts(h1b, h1, tab_a, tab_b, peer_u.astype(bf16), peer_v.T.astype(bf16), ln2_g.reshape(1, D_MODEL),
                 ln2_b.reshape(1, D_MODEL), 512, 1024)

    y_prompt = y[:n_p].reshape(bp, tp, D_MODEL)
    y_sample = y[n_p:].reshape(bs, tsp, D_MODEL)[:, :ts]
    dt = x_prompt.dtype
    return (y_prompt, y_sample, to_rows(kvt_p[0]), to_rows(kvt_p[1]), to_rows(kvt_p[2][:, :, -min(WINDOW, tp):]),
            p_c.astype(dt), p_n.astype(dt), p_m.astype(dt),
            to_rows(kvt_s[0][:, :, :ts]), to_rows(kvt_s[1][:, :, :ts]), to_rows(s_win_t),
            s_c.astype(state_C.dtype), s_n.astype(state_C.dtype), s_m.astype(state_C.dtype))
```

```python
import functools

import jax
import jax.numpy as jnp
import numpy as np
from jax import lax
from jax.experimental import pallas as pl
from jax.experimental.pallas import tpu as pltpu

D_MODEL = 1024
A_HEADS, A_DQK, A_DV = 4, 128, 256
B_HEADS, B_KV, B_HD = 16, 4, 64
B_REP = B_HEADS // B_KV
CMP_BLOCK = 64
N_SEL = 16
WINDOW = 512
PAGE_SIZE = 128
P_HEADS, P_NKEYS, P_DHALF, P_TOPK = 8, 128, 128, 16
P_EXPERTS = P_NKEYS * P_NKEYS
DN_ALPHA = 2.0 ** 0.25
LN_EPS = 1e-5
NEG = -1e30

LANES = 128
KV_COLS = 2 * B_KV * B_HD
VMEM_LIMIT = 56 * 1024 * 1024

C_AQ, C_AK, C_AV, C_AO, C_BQ, C_CMP, C_SLC, C_WIN, C_SMALL, C_END = (
    0, 512, 1024, 2048, 3072, 4096, 4608, 5120, 5632, 5760)
G_I, G_F, G_GATE = 0, A_HEADS, 2 * A_HEADS

bf16 = jnp.bfloat16
f32 = jnp.float32
_NT_DIMS = (((1,), (1,)), ((), ()))


def _cparams(*sem, flags=None):
    return pltpu.CompilerParams(dimension_semantics=sem, vmem_limit_bytes=VMEM_LIMIT, flags=flags)


def _full(shape):
    nd = len(shape)
    return pl.BlockSpec(shape, lambda *_: (0,) * nd)


def _proj_kernel(x_ref, w_ref, b_ref, o_ref, kvb_ref):
    res = jnp.dot(x_ref[...], w_ref[...], preferred_element_type=f32) + b_ref[...]
    o_ref[...] = res
    kvb_ref[...] = res[:, C_CMP:C_SMALL].astype(bf16)


def _proj(xb, w, b, tm):
    n, k = xb.shape
    e = w.shape[1]
    return pl.pallas_call(
        _proj_kernel,
        grid=(n // tm,),
        in_specs=[pl.BlockSpec((tm, k), lambda i: (i, 0)), _full((k, e)), _full((1, e))],
        out_specs=[pl.BlockSpec((tm, e), lambda i: (i, 0)), pl.BlockSpec((tm, C_SMALL - C_CMP), lambda i: (i, 0))],
        out_shape=[jax.ShapeDtypeStruct((n, e), f32), jax.ShapeDtypeStruct((n, C_SMALL - C_CMP), bf16)],
        compiler_params=_cparams("parallel"),
        name="proj",
    )(xb, w, b)


def _proj_t_kernel(wt_ref, x_ref, b_ref, o_ref):
    o_ref[...] = lax.dot_general(wt_ref[...], x_ref[...], (((1,), (1,)), ((), ())),
                                 preferred_element_type=f32) + b_ref[...]


def _proj_t(wt, xb, bcol, tn):
    e, k = wt.shape
    n = xb.shape[0]
    return pl.pallas_call(
        _proj_t_kernel,
        grid=(n // tn,),
        in_specs=[_full((e, k)), pl.BlockSpec((tn, k), lambda i: (i, 0)), _full((e, 1))],
        out_specs=pl.BlockSpec((e, tn), lambda i: (0, i)),
        out_shape=jax.ShapeDtypeStruct((e, n), f32),
        compiler_params=_cparams("parallel"),
        name="proj_t",
    )(wt, xb, bcol)


def _proj_kvt_kernel(wt_ref, x_ref, b_ref, oc_ref, os_ref, ow_ref):
    res = lax.dot_general(wt_ref[...], x_ref[...], _NT_DIMS, preferred_element_type=f32) + b_ref[...]
    for i, o_ref in enumerate((oc_ref, os_ref, ow_ref)):
        o_ref[0] = res[i * KV_COLS:(i + 1) * KV_COLS]


def _proj_kvt(wt, xb, bcol, *, row0, nb, t, tn):
    e, k = wt.shape
    nt = t // tn
    rb0 = row0 // tn
    out = pl.BlockSpec((1, KV_COLS, tn), lambda b, i: (b, 0, i))
    return pl.pallas_call(
        _proj_kvt_kernel,
        grid=(nb, nt),
        in_specs=[_full((e, k)), pl.BlockSpec((tn, k), lambda b, i: (rb0 + b * nt + i, 0)), _full((e, 1))],
        out_specs=[out, out, out],
        out_shape=[jax.ShapeDtypeStruct((nb, KV_COLS, t), f32)] * 3,
        compiler_params=_cparams("parallel", "parallel"),
        name="proj_kvt",
    )(wt, xb, bcol)


def _mlstm_kernel(q_ref, k_ref, v_ref, ao_ref, g_ref, gt_ref, ng_ref, c0_ref, n0_ref, m0_ref,
                  y_ref, c_out, n_out, m_out, c_s, n_s, m_s, *, L, valid):
    h = pl.program_id(1)
    c = pl.program_id(2)

    @pl.when(c == 0)
    def _():
        c_s[...] = c0_ref[0, 0]
        n_s[...] = n0_ref[0, 0]
        m_s[...] = m0_ref[0, 0]

    q = q_ref[...]
    k = k_ref[...] * (A_DQK ** -0.5)
    v = v_ref[...]
    g = g_ref[...]
    gt = gt_ref[0]
    lane = lax.broadcasted_iota(jnp.int32, g.shape, 1)
    sub = lax.broadcasted_iota(jnp.int32, gt.shape, 0)
    i_col = jnp.sum(jnp.where(lane == G_I + h, g, 0.0), axis=1, keepdims=True)
    f_col = jnp.sum(jnp.where(lane == G_F + h, g, 0.0), axis=1, keepdims=True)
    i_row = jnp.sum(jnp.where(sub == G_I + h, gt, 0.0), axis=0, keepdims=True)
    f_row = jnp.sum(jnp.where(sub == G_F + h, gt, 0.0), axis=0, keepdims=True)
    lf_col = jax.nn.log_sigmoid(f_col)
    lf_row = jax.nn.log_sigmoid(f_row)
    t_col = lax.broadcasted_iota(jnp.int32, (L, 1), 0)
    s_row = lax.broadcasted_iota(jnp.int32, (1, L), 1)
    if valid < L:
        lf_col = jnp.where(t_col < valid, lf_col, 0.0)
        lf_row = jnp.where(s_row < valid, lf_row, 0.0)
        i_col = jnp.where(t_col < valid, i_col, NEG)
        i_row = jnp.where(s_row < valid, i_row, NEG)
    tt = lax.broadcasted_iota(jnp.int32, (L, L), 0)
    ss = lax.broadcasted_iota(jnp.int32, (L, L), 1)
    causal = ss <= tt
    b_col = jnp.sum(jnp.where(causal, lf_row, 0.0), axis=1, keepdims=True)
    b_row = jnp.sum(jnp.where(tt <= ss, lf_col, 0.0), axis=0, keepdims=True)
    m_prev = m_s[...]
    cmat = c_s[...]
    n_row = n_s[...]

    d_log = jnp.where(causal, b_col - b_row + i_row, NEG)
    inter = b_col + m_prev
    m_t = jnp.maximum(inter, jnp.max(d_log, axis=1, keepdims=True))
    qb = q.astype(bf16)
    qk = lax.dot_general(qb, k.astype(bf16), (((1,), (1,)), ((), ())), preferred_element_type=f32)
    smat = qk * jnp.exp(d_log - m_t)
    w_inter = jnp.exp(inter - m_t)
    vb = v.astype(bf16)
    num = (w_inter * jnp.dot(qb, cmat.astype(bf16), preferred_element_type=f32)
           + jnp.dot(smat.astype(bf16), vb, preferred_element_type=f32))
    den = w_inter * jnp.sum(q * n_row, axis=1, keepdims=True) + jnp.sum(smat, axis=1, keepdims=True)
    hid = num / jnp.maximum(jnp.abs(den), jnp.exp(-m_t))
    mu = jnp.mean(hid, axis=1, keepdims=True)
    var = jnp.mean(jnp.square(hid - mu), axis=1, keepdims=True)
    hid = (hid - mu) * lax.rsqrt(var + LN_EPS) * ng_ref[...]
    y_ref[...] = hid * jax.nn.sigmoid(ao_ref[...])

    b_end = b_col[L - 1:L, :]
    g_row = b_end - b_row + i_row
    m_new = jnp.maximum(b_end + m_prev, jnp.max(g_row, axis=1, keepdims=True))
    a = jnp.exp(b_end + m_prev - m_new)
    w_col = jnp.exp(b_end - b_col + i_col - m_new)
    kw = k * w_col
    c_new = a * cmat + lax.dot_general(kw.astype(bf16), vb, (((0,), (0,)), ((), ())),
                                       preferred_element_type=f32)
    n_new = a * n_row + jnp.sum(kw, axis=0, keepdims=True)
    c_s[...] = c_new
    n_s[...] = n_new
    m_s[...] = m_new

    @pl.when(c == pl.num_programs(2) - 1)
    def _():
        c_out[0, 0] = c_new
        n_out[0, 0] = n_new
        m_out[0, 0] = m_new


def _into(buf, kern):
    if buf is None:
        return kern, [], [], {}
    return (lambda buf_ref, *refs: kern(*refs)), [pl.BlockSpec(memory_space=pl.ANY)], [buf], {0: 0}


def _mlstm(proj, gt, norm_g, c0, n0, m0, *, row0, nb, t, L, valid, y_buf=None):
    nc = t // L
    rb0 = row0 // L
    gt = gt[:8, row0:row0 + nb * t].reshape(8, nb * nc, L).transpose(1, 0, 2)
    rows = lambda b, h, c: rb0 + b * nc + c
    st = lambda b, h, c: (b, h, 0, 0)
    kern, alias_specs, alias_args, aliases = _into(y_buf, functools.partial(_mlstm_kernel, L=L, valid=valid))
    y, c_f, n_f, m_f = pl.pallas_call(
        kern,
        grid=(nb, A_HEADS, nc),
        input_output_aliases=aliases,
        in_specs=alias_specs + [
            pl.BlockSpec((L, A_DQK), lambda b, h, c: (rows(b, h, c), C_AQ // A_DQK + h)),
            pl.BlockSpec((L, A_DQK), lambda b, h, c: (rows(b, h, c), C_AK // A_DQK + h)),
            pl.BlockSpec((L, A_DV), lambda b, h, c: (rows(b, h, c), C_AV // A_DV + h)),
            pl.BlockSpec((L, A_DV), lambda b, h, c: (rows(b, h, c), C_AO // A_DV + h)),
            pl.BlockSpec((L, LANES), lambda b, h, c: (rows(b, h, c), C_SMALL // LANES)),
            pl.BlockSpec((1, 8, L), lambda b, h, c: (b * nc + c, 0, 0)),
            pl.BlockSpec((1, A_DV), lambda b, h, c: (0, h)),
            pl.BlockSpec((1, 1, A_DQK, A_DV), st),
            pl.BlockSpec((1, 1, 1, A_DQK), st),
            pl.BlockSpec((1, 1, 1, 1), st),
        ],
        out_specs=[
            pl.BlockSpec((L, A_DV), lambda b, h, c: (rows(b, h, c), h)),
            pl.BlockSpec((1, 1, A_DQK, A_DV), st),
            pl.BlockSpec((1, 1, 1, A_DQK), st),
            pl.BlockSpec((1, 1, 1, 1), st),
        ],
        out_shape=[
            jax.ShapeDtypeStruct((proj.shape[0], A_HEADS * A_DV), f32),
            jax.ShapeDtypeStruct((nb, A_HEADS, A_DQK, A_DV), f32),
            jax.ShapeDtypeStruct((nb, A_HEADS, 1, A_DQK), f32),
            jax.ShapeDtypeStruct((nb, A_HEADS, 1, 1), f32),
        ],
        scratch_shapes=[pltpu.VMEM((A_DQK, A_DV), f32), pltpu.VMEM((1, A_DQK), f32), pltpu.VMEM((1, 1), f32)],
        compiler_params=_cparams("parallel", "parallel", "arbitrary"),
        name="mlstm",
    )(*alias_args, proj, proj, proj, proj, proj, gt, norm_g, c0, n0, m0)
    return y, c_f, n_f[:, :, 0], m_f[:, :, 0, 0]


def _compress_kernel(x_ref, pe_ref, w_ref, o_ref, xf_ref, *, nblk):
    for l in range(CMP_BLOCK):
        xf_ref[:, l * LANES:(l + 1) * LANES] = x_ref[pl.ds(l, nblk, stride=CMP_BLOCK), :]
    xf = (xf_ref[...] + pe_ref[0]).astype(bf16)
    o_ref[...] = jnp.dot(xf, w_ref[0], preferred_element_type=f32)


def _compress(x2, pe2, w2, *, rows, steps, row0, colblk):
    nblk = rows // CMP_BLOCK
    kflat = CMP_BLOCK * LANES
    rb0 = row0 // rows
    ngrp = KV_COLS // LANES
    return pl.pallas_call(
        functools.partial(_compress_kernel, nblk=nblk),
        grid=(steps, ngrp),
        in_specs=[pl.BlockSpec((rows, LANES), lambda s, p: (rb0 + s, colblk * ngrp + p)),
                  pl.BlockSpec((1, 1, kflat), lambda s, p: (p // 2, 0, 0)),
                  pl.BlockSpec((1, kflat, LANES), lambda s, p: (p // 2, 0, 0))],
        out_specs=pl.BlockSpec((nblk, LANES), lambda s, p: (s, p)),
        out_shape=jax.ShapeDtypeStruct((steps * nblk, KV_COLS), f32),
        scratch_shapes=[pltpu.VMEM((nblk, kflat), f32)],
        compiler_params=_cparams("parallel", "parallel"),
        name="compress",
    )(x2, pe2, w2)


SEQS_PER_STEP = 8


def _compress_pages_kernel(pt_ref, *refs):
    del pt_ref
    pages = refs[:SEQS_PER_STEP]
    pe_ref, w_ref, o_ref = refs[SEQS_PER_STEP:SEQS_PER_STEP + 3]
    tm = refs[SEQS_PER_STEP + 3:SEQS_PER_STEP + 3 + KV_COLS // LANES]
    xf_s = refs[-1]
    j = pl.program_id(1)
    blocks = SEQS_PER_STEP * PAGE_SIZE // CMP_BLOCK
    for u, page in enumerate(pages):
        rows = page[0].T
        for p, tm_p in enumerate(tm):
            tm_p[u * PAGE_SIZE:(u + 1) * PAGE_SIZE, :] = rows[:, p * LANES:(p + 1) * LANES]
    dst = pl.ds(pl.multiple_of(j * blocks, blocks), blocks)
    for p, tm_p in enumerate(tm):
        for l in range(CMP_BLOCK):
            piece = tm_p[pl.ds(l, blocks, stride=CMP_BLOCK), :] + pe_ref[p // 2, :, l * LANES:(l + 1) * LANES]
            xf_s[p, dst, l * LANES:(l + 1) * LANES] = piece.astype(bf16)

    @pl.when(j == pl.num_programs(1) - 1)
    def _():
        for p in range(len(tm)):
            o_ref[0, :, p * LANES:(p + 1) * LANES] = jnp.dot(xf_s[p], w_ref[p // 2], preferred_element_type=f32)


def _compress_pages(page_table, cache_t, pe2, w2):
    nb, n_pages = page_table.shape
    groups = nb // SEQS_PER_STEP
    kflat = CMP_BLOCK * LANES
    rows = SEQS_PER_STEP * n_pages * PAGE_SIZE // CMP_BLOCK
    ngrp = KV_COLS // LANES
    out = pl.pallas_call(
        _compress_pages_kernel,
        grid_spec=pltpu.PrefetchScalarGridSpec(
            num_scalar_prefetch=1,
            grid=(groups, n_pages),
            in_specs=[pl.BlockSpec((1, KV_COLS, PAGE_SIZE),
                                   functools.partial(lambda s, j, pt, u: (pt[s * SEQS_PER_STEP + u, j], 0, 0), u=u))
                      for u in range(SEQS_PER_STEP)]
            + [pl.BlockSpec((2, 1, kflat), lambda s, j, pt: (0, 0, 0)),
               pl.BlockSpec((2, kflat, LANES), lambda s, j, pt: (0, 0, 0))],
            out_specs=pl.BlockSpec((1, rows, KV_COLS), lambda s, j, pt: (s, 0, 0)),
            scratch_shapes=[pltpu.VMEM((SEQS_PER_STEP * PAGE_SIZE, LANES), f32) for _ in range(ngrp)]
            + [pltpu.VMEM((ngrp, rows, kflat), bf16)],
        ),
        out_shape=jax.ShapeDtypeStruct((groups, rows, KV_COLS), f32),
        compiler_params=_cparams("parallel", "arbitrary"),
        name="compress_pages",
    )(page_table, *([cache_t] * SEQS_PER_STEP), pe2, w2)
    out = out.reshape(groups, n_pages, SEQS_PER_STEP, PAGE_SIZE // CMP_BLOCK, KV_COLS).transpose(0, 2, 1, 3, 4)
    return out.reshape(nb, n_pages * PAGE_SIZE // CMP_BLOCK, KV_COLS)


def _gate_col(g, lane_idx):
    lane = lax.broadcasted_iota(jnp.int32, g.shape, 1)
    return jax.nn.sigmoid(jnp.sum(jnp.where(lane == lane_idx, g, 0.0), axis=1, keepdims=True))


def _cmp_topk_kernel(slope_ref, q_ref, kc_ref, vc_ref, g_ref, o_ref, sel_ref, *, tq, nblk, qpos0):
    gi = pl.program_id(1)
    i = pl.program_id(2)
    qpos = qpos0 + i * tq + lax.broadcasted_iota(jnp.int32, (tq, 1), 0)
    j = lax.broadcasted_iota(jnp.int32, (1, nblk), 1)
    dist = qpos - ((j + 1) * CMP_BLOCK - 1)
    valid = dist >= 0
    distf = dist.astype(f32)
    kc = kc_ref[0, 0].astype(bf16)
    vc = vc_ref[0, 0].astype(bf16)
    g = g_ref[...]
    imp = jnp.zeros((tq, nblk), f32)
    for r in range(B_REP):
        qr = q_ref[0, 0, 0, r * tq:(r + 1) * tq, :]
        s = lax.dot_general(qr, kc, (((1,), (1,)), ((), ())), preferred_element_type=f32) * (B_HD ** -0.5)
        s = s - slope_ref[gi * B_REP + r] * distf
        s = jnp.where(valid, s, NEG)
        e = jnp.exp(s - jnp.max(s, axis=1, keepdims=True))
        p = jnp.where(valid, e / jnp.sum(e, axis=1, keepdims=True), 0.0)
        imp = imp + p
        o = jnp.dot(p.astype(bf16), vc, preferred_element_type=f32)
        o_ref[0, 0, 0, r * tq:(r + 1) * tq, :] = o * _gate_col(g, G_GATE + gi * B_REP + r)
    cur = qpos // CMP_BLOCK
    imp = jnp.where((j == cur) | (j == 0), float(B_REP + 1), imp)
    imp = jnp.where(j > cur, -1.0, imp)
    jf = j.astype(f32)
    sel = jnp.zeros((tq, nblk), f32)
    for _ in range(N_SEL):
        mx = jnp.max(imp, axis=1, keepdims=True)
        idx = jnp.min(jnp.where(imp == mx, jf, float(nblk)), axis=1, keepdims=True)
        hit = jf == idx
        sel = jnp.where(hit, 1.0, sel)
        imp = jnp.where(hit, NEG, imp)
    sel_ref[0, 0] = sel


def _cmp_topk(slopes, qt, kc, vc, gsmall, *, tq, qpos0, row0):
    nb, _, nqt, _, _ = qt.shape
    nblk = kc.shape[2]
    rb0 = row0 // tq
    return pl.pallas_call(
        functools.partial(_cmp_topk_kernel, tq=tq, nblk=nblk, qpos0=qpos0),
        grid=(nb, B_KV, nqt),
        in_specs=[
            pl.BlockSpec(memory_space=pltpu.SMEM),
            pl.BlockSpec((1, 1, 1, B_REP * tq, B_HD), lambda b, g, i: (b, g, i, 0, 0)),
            pl.BlockSpec((1, 1, nblk, B_HD), lambda b, g, i: (b, g, 0, 0)),
            pl.BlockSpec((1, 1, nblk, B_HD), lambda b, g, i: (b, g, 0, 0)),
            pl.BlockSpec((tq, LANES), lambda b, g, i: (rb0 + b * nqt + i, 0)),
        ],
        out_specs=[
            pl.BlockSpec((1, 1, 1, B_REP * tq, B_HD), lambda b, g, i: (b, g, i, 0, 0)),
            pl.BlockSpec((1, 1, tq, nblk), lambda b, g, i: (b, g, i, 0)),
        ],
        out_shape=[
            jax.ShapeDtypeStruct(qt.shape, f32),
            jax.ShapeDtypeStruct((nb, B_KV, nqt * tq, nblk), f32),
        ],
        compiler_params=_cparams("parallel", "parallel", "parallel"),
        name="cmp_topk",
    )(slopes, qt, kc, vc, gsmall)


def _attn_kernel(slope_ref, q_ref, k_ref, v_ref, g_ref, *rest, tq, tk, nkt, nblk, qpos0, kpos0, window,
                 gate_lane):
    if window is None:
        sel_ref, o_ref, m_s, l_s, acc_s = rest
    else:
        o_ref, m_s, l_s, acc_s = rest
    gi = pl.program_id(1)
    i = pl.program_id(2)
    qlo = qpos0 + i * tq
    qpos = qlo + lax.broadcasted_iota(jnp.int32, (tq, 1), 0)
    m_s[...] = jnp.full(m_s.shape, NEG, f32)
    l_s[...] = jnp.zeros(l_s.shape, f32)
    acc_s[...] = jnp.zeros(acc_s.shape, f32)
    kt_hi = jnp.minimum((qlo + tq - 1 - kpos0) // tk + 1, nkt)
    if window is None:
        kt_lo = 0
        selb = sel_ref[0, 0].astype(bf16)
    else:
        kt_lo = jnp.maximum(qlo - (window - 1) - kpos0, 0) // tk

    def body(kt, carry):
        k0 = pl.multiple_of(kt * tk, tk)
        kb = k_ref[0, 0, pl.ds(k0, tk), :]
        vb = v_ref[0, 0, pl.ds(k0, tk), :]
        kidx = k0 + lax.broadcasted_iota(jnp.int32, (1, tk), 1)
        dist = qpos - (kpos0 + kidx)
        mask = dist >= 0
        if window is None:
            blk = lax.broadcasted_iota(jnp.int32, (nblk, tk), 0)
            kblk = (k0 + lax.broadcasted_iota(jnp.int32, (nblk, tk), 1)) // CMP_BLOCK
            expand = jnp.where(blk == kblk, 1.0, 0.0).astype(bf16)
            mask = mask & (jnp.dot(selb, expand, preferred_element_type=f32) > 0.5)
        else:
            mask = mask & (dist < window)
        distf = dist.astype(f32)
        for r in range(B_REP):
            rows = slice(r * tq, (r + 1) * tq)
            s = lax.dot_general(q_ref[0, 0, 0, rows, :], kb, (((1,), (1,)), ((), ())),
                                preferred_element_type=f32) * (B_HD ** -0.5)
            s = jnp.where(mask, s - slope_ref[gi * B_REP + r] * distf, NEG)
            m_old = m_s[rows, :]
            m_new = jnp.maximum(m_old, jnp.max(s, axis=1, keepdims=True))
            alpha = jnp.exp(m_old - m_new)
            p = jnp.exp(s - m_new)
            l_s[rows, :] = alpha * l_s[rows, :] + jnp.sum(p, axis=1, keepdims=True)
            acc_s[rows, :] = alpha * acc_s[rows, :] + jnp.dot(p.astype(bf16), vb, preferred_element_type=f32)
            m_s[rows, :] = m_new
        return carry

    lax.fori_loop(kt_lo, kt_hi, body, 0)
    g = g_ref[...]
    for r in range(B_REP):
        rows = slice(r * tq, (r + 1) * tq)
        o_ref[0, 0, 0, rows, :] = acc_s[rows, :] / l_s[rows, :] * _gate_col(g, gate_lane + gi * B_REP + r)


def _attn(slopes, qt, kh, vh, gsmall, sel, *, tq, tk, qpos0, kpos0, window, gate_lane, row0):
    nb, _, nqt, _, _ = qt.shape
    tkk = kh.shape[2]
    nkt = tkk // tk
    nblk = None if sel is None else sel.shape[3]
    rb0 = row0 // tq
    in_specs = [
        pl.BlockSpec(memory_space=pltpu.SMEM),
        pl.BlockSpec((1, 1, 1, B_REP * tq, B_HD), lambda b, g, i: (b, g, i, 0, 0)),
        pl.BlockSpec((1, 1, tkk, B_HD), lambda b, g, i: (b, g, 0, 0)),
        pl.BlockSpec((1, 1, tkk, B_HD), lambda b, g, i: (b, g, 0, 0)),
        pl.BlockSpec((tq, LANES), lambda b, g, i: (rb0 + b * nqt + i, 0)),
    ]
    args = [slopes, qt, kh, vh, gsmall]
    if sel is not None:
        in_specs.append(pl.BlockSpec((1, 1, tq, nblk), lambda b, g, i: (b, g, i, 0)))
        args.append(sel)
    return pl.pallas_call(
        functools.partial(_attn_kernel, tq=tq, tk=tk, nkt=nkt, nblk=nblk, qpos0=qpos0, kpos0=kpos0,
                          window=window, gate_lane=gate_lane),
        grid=(nb, B_KV, nqt),
        in_specs=in_specs,
        out_specs=pl.BlockSpec((1, 1, 1, B_REP * tq, B_HD), lambda b, g, i: (b, g, i, 0, 0)),
        out_shape=jax.ShapeDtypeStruct(qt.shape, f32),
        scratch_shapes=[pltpu.VMEM((B_REP * tq, 1), f32), pltpu.VMEM((B_REP * tq, 1), f32),
                        pltpu.VMEM((B_REP * tq, B_HD), f32)],
        compiler_params=_cparams("parallel", "parallel", "parallel"),
        name="attn_sel" if window is None else "attn_win",
    )(*args)


def _roll_lanes(x, shift):
    return x if shift == 0 else pltpu.roll(x, shift, axis=1)


def _softmax_rows(s, mask):
    s = jnp.where(mask, s, NEG)
    e = jnp.exp(s - jnp.max(s, axis=1, keepdims=True))
    return jnp.where(mask, e, 0.0), jnp.sum(e, axis=1, keepdims=True)


def _decode_kernel(slope_ref, q_ref, g_ref, kvc_ref, pages, slc_new_ref, win_ref, win_new_ref, exp_ref,
                   oc_ref, os_ref, ow_ref, kt_s, vt_s, wkt_s, wvt_s, *, tq, nblk, qpos0, win_kpos0):
    nrow = B_HEADS * tq
    q = q_ref[...]
    gs = g_ref[...]
    lane_grp = lax.broadcasted_iota(jnp.int32, (tq, B_KV * B_HD), 1) // B_HD
    qm = []
    for g in range(B_KV):
        qg = q[:, g * B_KV * B_HD:(g + 1) * B_KV * B_HD]
        for r in range(B_REP):
            qm.append(jnp.where(lane_grp == g, _roll_lanes(qg, ((g - r) % B_REP) * B_HD), 0.0))
    qm = jnp.concatenate(qm, axis=0).astype(bf16)
    row = lax.broadcasted_iota(jnp.int32, (nrow, 1), 0)
    qpos = qpos0 + row % tq
    slope = functools.reduce(lambda acc, h: jnp.where(row // tq == h, slope_ref[h], acc), range(B_HEADS),
                             jnp.zeros((nrow, 1), f32))
    scale = B_HD ** -0.5
    nt = (((1,), (1,)), ((), ()))

    def emit(o_ref, o, branch):
        for g in range(B_KV):
            acc = jnp.zeros((tq, B_KV * B_HD), f32)
            for r in range(B_REP):
                h = g * B_REP + r
                gate = jax.nn.sigmoid(gs[:, G_GATE + branch * B_HEADS + h:G_GATE + branch * B_HEADS + h + 1])
                oh = jnp.where(lane_grp == g, o[h * tq:(h + 1) * tq, :] * gate, 0.0)
                acc = acc + _roll_lanes(oh, ((r - g) % B_REP) * B_HD)
            o_ref[:, g * B_KV * B_HD:(g + 1) * B_KV * B_HD] = acc

    kvc = kvc_ref[0]
    j = lax.broadcasted_iota(jnp.int32, (1, nblk), 1)
    dist = qpos - ((j + 1) * CMP_BLOCK - 1)
    s = lax.dot_general(qm, kvc[:, :B_KV * B_HD].astype(bf16), nt, preferred_element_type=f32) * scale
    e, l = _softmax_rows(s - slope * dist.astype(f32), dist >= 0)
    p = e / l
    emit(oc_ref, jnp.dot(p.astype(bf16), kvc[:, B_KV * B_HD:].astype(bf16), preferred_element_type=f32), 0)
    imp = jnp.concatenate(
        [functools.reduce(lambda a, b: a + b, [p[(g * B_REP + r) * tq:(g * B_REP + r + 1) * tq] for r in range(B_REP)])
         for g in range(B_KV)], axis=0)
    cur = (qpos0 + lax.broadcasted_iota(jnp.int32, (B_KV * tq, 1), 0) % tq) // CMP_BLOCK
    imp = jnp.where((j == cur) | (j == 0), float(B_REP + 1), imp)
    imp = jnp.where(j > cur, -1.0, imp)
    jf = j.astype(f32)
    sel = jnp.zeros(imp.shape, f32)
    for _ in range(N_SEL):
        mx = jnp.max(imp, axis=1, keepdims=True)
        idx = jnp.min(jnp.where(imp == mx, jf, float(nblk)), axis=1, keepdims=True)
        hit = jf == idx
        sel = jnp.where(hit, 1.0, sel)
        imp = jnp.where(hit, NEG, imp)
    sel_rows = jnp.concatenate([sel[g * tq:(g + 1) * tq] for g in range(B_KV) for _ in range(B_REP)], axis=0)

    def transposed_kv(parts, kt_s, vt_s):
        for u, part in enumerate(parts):
            w = part.shape[-1]
            kt_s[:, u * w:(u + 1) * w] = part[0, :GRP_LANES, :].astype(bf16)
            vt_s[:, u * w:(u + 1) * w] = part[0, GRP_LANES:, :].astype(bf16)

    def attend(kt_s, vt_s, mask):
        s = jnp.dot(qm, kt_s[...], preferred_element_type=f32) * scale
        e, l = _softmax_rows(s - slope * dist.astype(f32), mask)
        return lax.dot_general(e.astype(bf16), vt_s[...], nt, preferred_element_type=f32) / l

    transposed_kv(list(pages) + [slc_new_ref], kt_s, vt_s)
    dist = qpos - lax.broadcasted_iota(jnp.int32, (1, kt_s.shape[1]), 1)
    picked = jnp.dot(sel_rows.astype(bf16), exp_ref[...], preferred_element_type=f32) > 0.5
    emit(os_ref, attend(kt_s, vt_s, picked & (dist >= 0)), 1)

    wb = win_ref.shape[-1]
    wkt_s[:, :wb] = win_ref[0, :GRP_LANES, :].astype(bf16)
    wvt_s[:, :wb] = win_ref[0, GRP_LANES:, :].astype(bf16)
    wkt_s[:, wb:] = win_new_ref[0, :GRP_LANES, :].astype(bf16)
    wvt_s[:, wb:] = win_new_ref[0, GRP_LANES:, :].astype(bf16)
    dist = qpos - (win_kpos0 + lax.broadcasted_iota(jnp.int32, (1, wkt_s.shape[1]), 1))
    emit(ow_ref, attend(wkt_s, wvt_s, (dist >= 0) & (dist < WINDOW)), 2)


def _decode(page_table, bufs, slopes, proj, kvc, slc_cache_t, slc_new_t, win_t, win_new_t, expand, *,
            tq, row0, qpos0, win_kpos0):
    nb, n_pages = page_table.shape
    nblk = kvc.shape[1]
    rb0 = row0 // tq
    tk = (n_pages + 1) * PAGE_SIZE
    twin = win_t.shape[-1] + win_new_t.shape[-1]
    out = pl.BlockSpec((tq, B_HEADS * B_HD), lambda b, pt: (rb0 + b, 0))
    page = (1, KV_COLS, PAGE_SIZE)

    def body(pt_ref, oc_buf, os_buf, ow_buf, slope_ref, q_ref, g_ref, kvc_ref, *refs):
        _decode_kernel(slope_ref, q_ref, g_ref, kvc_ref, refs[:n_pages], *refs[n_pages:], tq=tq, nblk=nblk,
                       qpos0=qpos0, win_kpos0=win_kpos0)

    return pl.pallas_call(
        body,
        grid_spec=pltpu.PrefetchScalarGridSpec(
            num_scalar_prefetch=1,
            grid=(nb,),
            in_specs=[pl.BlockSpec(memory_space=pl.ANY)] * 3
            + [pl.BlockSpec(memory_space=pltpu.SMEM),
               pl.BlockSpec((tq, B_HEADS * B_HD), lambda b, pt: (rb0 + b, C_BQ // (B_HEADS * B_HD))),
               pl.BlockSpec((tq, LANES), lambda b, pt: (rb0 + b, C_SMALL // LANES)),
               pl.BlockSpec((1,) + kvc.shape[1:], lambda b, pt: (b, 0, 0))]
            + [pl.BlockSpec(page, functools.partial(lambda b, pt, u: (pt[b, u], 0, 0), u=u)) for u in range(n_pages)]
            + [pl.BlockSpec((1,) + slc_new_t.shape[1:], lambda b, pt: (b, 0, 0)),
               pl.BlockSpec((1,) + win_t.shape[1:], lambda b, pt: (b, 0, 0)),
               pl.BlockSpec((1,) + win_new_t.shape[1:], lambda b, pt: (b, 0, 0)),
               pl.BlockSpec(expand.shape, lambda b, pt: (0, 0))],
            out_specs=[out, out, out],
            scratch_shapes=[pltpu.VMEM((GRP_LANES, tk), bf16), pltpu.VMEM((GRP_LANES, tk), bf16),
                            pltpu.VMEM((GRP_LANES, twin), bf16), pltpu.VMEM((GRP_LANES, twin), bf16)],
        ),
        out_shape=[jax.ShapeDtypeStruct(b.shape, f32) for b in bufs],
        input_output_aliases={1: 0, 2: 1, 3: 2},
        compiler_params=_cparams("parallel"),
        name="nsa_decode",
    )(page_table, *bufs, slopes, proj, proj, kvc, *([slc_cache_t] * n_pages), slc_new_t, win_t, win_new_t, expand)


GRP_LANES = B_KV * B_HD
_NT = (((1,), (1,)), ((), ()))


def _masked_queries(q, tq):
    lane_grp = lax.broadcasted_iota(jnp.int32, (tq, GRP_LANES), 1) // B_HD
    rows = []
    for g in range(B_KV):
        qg = q[:, g * GRP_LANES:(g + 1) * GRP_LANES]
        for r in range(B_REP):
            rows.append(jnp.where(lane_grp == g, _roll_lanes(qg, ((g - r) % B_REP) * B_HD), 0.0))
    return jnp.concatenate(rows, axis=0).astype(bf16)


def _group_columns(slope_ref, g, tq, qlo):
    row = lax.broadcasted_iota(jnp.int32, (B_REP * tq, 1), 0)
    slope = functools.reduce(lambda acc, r: jnp.where(row // tq == r, slope_ref[g * B_REP + r], acc), range(B_REP),
                             jnp.zeros((B_REP * tq, 1), f32))
    return qlo + row % tq, slope


def _emit_group(o_ref, og, gs, branch, g, tq):
    lane_grp = lax.broadcasted_iota(jnp.int32, (tq, GRP_LANES), 1) // B_HD
    acc = jnp.zeros((tq, GRP_LANES), f32)
    for r in range(B_REP):
        c = G_GATE + branch * B_HEADS + g * B_REP + r
        oh = jnp.where(lane_grp == g, og[r * tq:(r + 1) * tq, :] * jax.nn.sigmoid(gs[:, c:c + 1]), 0.0)
        acc = acc + _roll_lanes(oh, ((r - g) % B_REP) * B_HD)
    o_ref[:, g * GRP_LANES:(g + 1) * GRP_LANES] = acc


def _cmp_nat_kernel(slope_ref, q_ref, g_ref, kvc_ref, o_ref, sel_ref, *, tq, nblk):
    qlo = pl.program_id(1) * tq
    qm = _masked_queries(q_ref[...], tq)
    gs = g_ref[...]
    kvc = kvc_ref[0]
    kc = kvc[:, :GRP_LANES].astype(bf16)
    vc = kvc[:, GRP_LANES:].astype(bf16)
    j = lax.broadcasted_iota(jnp.int32, (1, nblk), 1)
    jf = j.astype(f32)
    imps = []
    for g in range(B_KV):
        qpos, slope = _group_columns(slope_ref, g, tq, qlo)
        dist = qpos - ((j + 1) * CMP_BLOCK - 1)
        s = lax.dot_general(qm[g * B_REP * tq:(g + 1) * B_REP * tq], kc, _NT, preferred_element_type=f32) * (B_HD ** -0.5)
        e, l = _softmax_rows(s - slope * dist.astype(f32), dist >= 0)
        p = e / l
        _emit_group(o_ref, jnp.dot(p.astype(bf16), vc, preferred_element_type=f32), gs, 0, g, tq)
        imps.append(functools.reduce(lambda a, b: a + b, [p[r * tq:(r + 1) * tq] for r in range(B_REP)]))
    imp = jnp.concatenate(imps, axis=0)
    cur = (qlo + lax.broadcasted_iota(jnp.int32, (B_KV * tq, 1), 0) % tq) // CMP_BLOCK
    imp = jnp.where((j == cur) | (j == 0), float(B_REP + 1), imp)
    imp = jnp.where(j > cur, -1.0, imp)
    sel = jnp.zeros(imp.shape, f32)
    for _ in range(N_SEL):
        mx = jnp.max(imp, axis=1, keepdims=True)
        idx = jnp.min(jnp.where(imp == mx, jf, float(nblk)), axis=1, keepdims=True)
        hit = jf == idx
        sel = jnp.where(hit, 1.0, sel)
        imp = jnp.where(hit, NEG, imp)
    sel_ref[0, 0] = sel


def _cmp_nat(slopes, proj, kvc, *, nb, t, tq):
    nqt = t // tq
    nblk = kvc.shape[1]
    return pl.pallas_call(
        functools.partial(_cmp_nat_kernel, tq=tq, nblk=nblk),
        grid=(nb, nqt),
        in_specs=[pl.BlockSpec(memory_space=pltpu.SMEM),
                  pl.BlockSpec((tq, B_HEADS * B_HD), lambda b, i: (b * nqt + i, C_BQ // (B_HEADS * B_HD))),
                  pl.BlockSpec((tq, LANES), lambda b, i: (b * nqt + i, C_SMALL // LANES)),
                  pl.BlockSpec((1, nblk, KV_COLS), lambda b, i: (b, 0, 0))],
        out_specs=[pl.BlockSpec((tq, B_HEADS * B_HD), lambda b, i: (b * nqt + i, 0)),
                   pl.BlockSpec((1, 1, B_KV * tq, nblk), lambda b, i: (b, i, 0, 0))],
        out_shape=[jax.ShapeDtypeStruct((proj.shape[0], B_HEADS * B_HD), f32),
                   jax.ShapeDtypeStruct((nb, nqt, B_KV * tq, nblk), f32)],
        compiler_params=_cparams("parallel", "parallel"),
        name="nsa_cmp",
    )(slopes, proj, proj, kvc)


def _sel_nat_kernel(slope_ref, q_ref, g_ref, kv_ref, sel_ref, o_ref, qm_s, m_s, l_s, acc_s, *, tq, tk, nblk):
    qlo = pl.program_id(1) * tq
    qm_s[...] = _masked_queries(q_ref[...] * (B_HD ** -0.5), tq)
    m_s[...] = jnp.full(m_s.shape, NEG, f32)
    l_s[...] = jnp.zeros(l_s.shape, f32)
    acc_s[...] = jnp.zeros(acc_s.shape, f32)
    selb = sel_ref[0, 0].astype(bf16)
    qpos = qlo + lax.broadcasted_iota(jnp.int32, (tq, 1), 0)
    grows = B_REP * tq

    def body(kt, carry):
        k0 = pl.multiple_of(kt * tk, tk)
        kb = kv_ref[pl.ds(k0, tk), :GRP_LANES]
        vb = kv_ref[pl.ds(k0, tk), GRP_LANES:]
        dist = qpos - (k0 + lax.broadcasted_iota(jnp.int32, (1, tk), 1))
        distf = dist.astype(f32)
        blk = lax.broadcasted_iota(jnp.int32, (nblk, tk), 0)
        kblk = (k0 + lax.broadcasted_iota(jnp.int32, (nblk, tk), 1)) // CMP_BLOCK
        picked = jnp.dot(selb, jnp.where(blk == kblk, 1.0, 0.0).astype(bf16), preferred_element_type=f32)
        for g in range(B_KV):
            sg = lax.dot_general(qm_s[g * grows:(g + 1) * grows, :], kb, _NT, preferred_element_type=f32)
            amask = jnp.where((picked[g * tq:(g + 1) * tq] > 0.5) & (dist >= 0), 0.0, NEG)
            ps, alphas = [], []
            for r in range(B_REP):
                rows = slice((g * B_REP + r) * tq, (g * B_REP + r + 1) * tq)
                s = sg[r * tq:(r + 1) * tq] - slope_ref[g * B_REP + r] * distf + amask
                m_old = m_s[rows, :]
                m_new = jnp.maximum(m_old, jnp.max(s, axis=1, keepdims=True))
                alpha = jnp.exp(m_old - m_new)
                p = jnp.exp(s - m_new)
                l_s[rows, :] = alpha * l_s[rows, :] + jnp.sum(p, axis=1, keepdims=True)
                m_s[rows, :] = m_new
                ps.append(p.astype(bf16))
                alphas.append(alpha)
            grp = slice(g * grows, (g + 1) * grows)
            acc_s[grp, :] = (jnp.concatenate(alphas, axis=0) * acc_s[grp, :]
                             + jnp.dot(jnp.concatenate(ps, axis=0), vb, preferred_element_type=f32))
        return carry

    lax.fori_loop(0, (qlo + tq - 1) // tk + 1, body, 0)
    gs = g_ref[...]
    for g in range(B_KV):
        rows = slice(g * grows, (g + 1) * grows)
        _emit_group(o_ref, acc_s[rows, :] / l_s[rows, :], gs, 1, g, tq)


def _win_nat_kernel(slope_ref, q_ref, g_ref, kv_ref, o_ref, *, tq, t):
    qlo = pl.program_id(1) * tq
    span = WINDOW + tq
    k0 = pl.multiple_of(jnp.clip(qlo - WINDOW, 0, t - span), LANES)
    qm = _masked_queries(q_ref[...] * (B_HD ** -0.5), tq)
    gs = g_ref[...]
    kb = kv_ref[pl.ds(k0, span), :GRP_LANES]
    vb = kv_ref[pl.ds(k0, span), GRP_LANES:]
    dist = qlo + lax.broadcasted_iota(jnp.int32, (tq, 1), 0) - (k0 + lax.broadcasted_iota(jnp.int32, (1, span), 1))
    distf = dist.astype(f32)
    amask = jnp.where((dist >= 0) & (dist < WINDOW), 0.0, NEG)
    grows = B_REP * tq
    for g in range(B_KV):
        sg = lax.dot_general(qm[g * grows:(g + 1) * grows], kb, _NT, preferred_element_type=f32)
        es, ls = [], []
        for r in range(B_REP):
            s = sg[r * tq:(r + 1) * tq] - slope_ref[g * B_REP + r] * distf + amask
            e = jnp.exp(s - jnp.max(s, axis=1, keepdims=True))
            es.append(e.astype(bf16))
            ls.append(jnp.sum(e, axis=1, keepdims=True))
        og = jnp.dot(jnp.concatenate(es, axis=0), vb, preferred_element_type=f32) / jnp.concatenate(ls, axis=0)
        _emit_group(o_ref, og, gs, 2, g, tq)


def _attn_nat(slopes, proj, kvb, sel, *, nb, t, tq, tk, branch):
    nqt = t // tq
    in_specs = [pl.BlockSpec(memory_space=pltpu.SMEM),
                pl.BlockSpec((tq, B_HEADS * B_HD), lambda b, i: (b * nqt + i, C_BQ // (B_HEADS * B_HD))),
                pl.BlockSpec((tq, LANES), lambda b, i: (b * nqt + i, C_SMALL // LANES)),
                pl.BlockSpec((t, KV_COLS), lambda b, i: (b, branch))]
    args = [slopes, proj, proj, kvb]
    if sel is None:
        body, scratch, name = functools.partial(_win_nat_kernel, tq=tq, t=t), [], "nsa_win"
    else:
        nblk = sel.shape[3]
        in_specs.append(pl.BlockSpec((1, 1, B_KV * tq, nblk), lambda b, i: (b, i, 0, 0)))
        args.append(sel)
        body = functools.partial(_sel_nat_kernel, tq=tq, tk=tk, nblk=nblk)
        scratch = [pltpu.VMEM((B_HEADS * tq, GRP_LANES), bf16), pltpu.VMEM((B_HEADS * tq, 1), f32),
                   pltpu.VMEM((B_HEADS * tq, 1), f32), pltpu.VMEM((B_HEADS * tq, GRP_LANES), f32)]
        name = "nsa_sel"
    return pl.pallas_call(
        body,
        grid=(nb, nqt),
        in_specs=in_specs,
        out_specs=pl.BlockSpec((tq, B_HEADS * B_HD), lambda b, i: (b * nqt + i, 0)),
        out_shape=jax.ShapeDtypeStruct((proj.shape[0], B_HEADS * B_HD), f32),
        scratch_shapes=scratch,
        compiler_params=_cparams("parallel", "parallel"),
        name=name,
    )(*args)


PAD_PAGES = 4


def _gather_kernel(pt_ref, *refs):
    del pt_ref
    pages, tail_ref, o_ref = refs[:-2], refs[-2], refs[-1]
    for u, page in enumerate(pages):
        o_ref[0, u] = page[0]
    o_ref[0, len(pages)] = tail_ref[0]
    for u in range(len(pages) + 1, len(pages) + PAD_PAGES):
        o_ref[0, u] = jnp.zeros(o_ref.shape[2:], f32)


def _gather_pages(page_table, cache, tail):
    nb, n_pages = page_table.shape
    page = (1, PAGE_SIZE, KV_COLS)
    return pl.pallas_call(
        _gather_kernel,
        grid_spec=pltpu.PrefetchScalarGridSpec(
            num_scalar_prefetch=1,
            grid=(nb,),
            in_specs=[pl.BlockSpec(page, functools.partial(lambda b, pt, u: (pt[b, u], 0, 0), u=u))
                      for u in range(n_pages)] + [pl.BlockSpec(page, lambda b, pt: (b, 0, 0))],
            out_specs=pl.BlockSpec((1, n_pages + PAD_PAGES, PAGE_SIZE, KV_COLS), lambda b, pt: (b, 0, 0, 0)),
        ),
        out_shape=jax.ShapeDtypeStruct((nb, n_pages + PAD_PAGES, PAGE_SIZE, KV_COLS), f32),
        compiler_params=_cparams("parallel"),
        name="gather_pages",
    )(page_table, *([cache] * n_pages), tail)


def _layer_norm(z, g, b):
    mu = jnp.mean(z, axis=1, keepdims=True)
    var = jnp.mean(jnp.square(z - mu), axis=1, keepdims=True)
    return (z - mu) * lax.rsqrt(var + LN_EPS) * g + b


def _tail_kernel(x_ref, ya_ref, oc_ref, os_ref, ow_ref, wm_ref, wa_ref, wb_ref, wo_ref, g_ref, b_ref,
                 h_ref, hb_ref):
    x = x_ref[...]
    gates = jax.nn.sigmoid(jnp.dot(x.astype(bf16), wm_ref[...], preferred_element_type=f32))
    yb = oc_ref[...] + os_ref[...] + ow_ref[...]
    ma = jnp.dot(ya_ref[...].astype(bf16), wa_ref[...], preferred_element_type=f32)
    mb = jnp.dot(yb.astype(bf16), wb_ref[...], preferred_element_type=f32)
    merged = gates[:, :D_MODEL] * ma + gates[:, D_MODEL:] * mb
    z = DN_ALPHA * x + jnp.dot(merged.astype(bf16), wo_ref[...], preferred_element_type=f32)
    h = _layer_norm(z, g_ref[...], b_ref[...])
    h_ref[...] = h
    hb_ref[...] = h.astype(bf16)


def _tail(x, ya, oc, os_, ow, wm, wa, wb, wo, g, b, tm):
    n = x.shape[0]
    row = pl.BlockSpec((tm, D_MODEL), lambda i: (i, 0))
    return pl.pallas_call(
        _tail_kernel,
        grid=(n // tm,),
        in_specs=[row] * 5 + [_full(wm.shape), _full(wa.shape), _full(wb.shape), _full(wo.shape),
                              _full(g.shape), _full(b.shape)],
        out_specs=[row, row],
        out_shape=[jax.ShapeDtypeStruct((n, D_MODEL), f32), jax.ShapeDtypeStruct((n, D_MODEL), bf16)],
        compiler_params=_cparams("parallel"),
        name="tail",
    )(x, ya, oc, os_, ow, wm, wa, wb, wo, g, b)


def _top16(x):
    kk, tb = x.shape
    ji = lax.broadcasted_iota(jnp.int32, (kk, tb), 0).astype(f32)
    rank = jnp.full((kk, tb), float(P_TOPK), f32)
    vals = []
    for k in range(P_TOPK):
        mx = jnp.max(x, axis=0, keepdims=True)
        idx = jnp.min(jnp.where(x == mx, ji, float(kk)), axis=0, keepdims=True)
        hit = ji == idx
        rank = jnp.where(hit, float(k), rank)
        vals.append(mx)
        x = jnp.where(hit, NEG, x)
    return rank, vals


_CAND_ROWS8 = ((1, 8), (2, 5), (3, 4), (4, 3))


def _route_kernel(h_ref, wqt_ref, keys_ref, ta_ref, tb_ref):
    qpt = lax.dot_general(wqt_ref[...], h_ref[...], (((1,), (1,)), ((), ())), preferred_element_type=f32)
    tb = qpt.shape[1]
    sub16 = lax.broadcasted_iota(jnp.int32, (P_TOPK, tb), 0)
    sub8 = lax.broadcasted_iota(jnp.int32, (8, tb), 0)
    for p in range(P_HEADS):
        sc, rk, vl = [], [], []
        for c in range(2):
            qs = qpt[(2 * p + c) * P_DHALF:(2 * p + c + 1) * P_DHALF, :].astype(bf16)
            s = jnp.dot(keys_ref[p, c], qs, preferred_element_type=f32)
            r, v = _top16(s)
            sc.append(s)
            rk.append(r)
            vl.append(v)
        v0, v1 = vl
        col0 = functools.reduce(lambda acc, k: jnp.where(sub16 == k, v0[k], acc), range(P_TOPK), jnp.zeros((P_TOPK, tb), f32))
        col1 = functools.reduce(lambda acc, k: jnp.where(sub16 == k, v1[k], acc), range(P_TOPK), jnp.zeros((P_TOPK, tb), f32))
        segs = [v0[0] + col1]
        for k1, keep in _CAND_ROWS8:
            segs.append(jnp.where(sub8 < keep, v0[k1] + col1[0:8], NEG))
        first = jnp.where(sub8 < 2, v0[5], jnp.where(sub8 < 4, v0[6], v0[7]))
        second = jnp.where(sub8 % 2 == 0, v1[0], v1[1])
        segs.append(jnp.where(sub8 < 6, first + second, NEG))
        segs.append(col0[8:16] + v1[0])
        cand = jnp.concatenate(segs, axis=0)
        crank, cvals = _top16(cand)
        taken = jnp.where(crank < float(P_TOPK), 1.0, 0.0)
        z = functools.reduce(lambda acc, v: acc + jnp.exp(v - cvals[0]), cvals, jnp.zeros((1, tb), f32))
        cnt = [jnp.sum(taken[0:16], axis=0, keepdims=True)]
        for i in range(len(_CAND_ROWS8)):
            cnt.append(jnp.sum(taken[16 + 8 * i:24 + 8 * i], axis=0, keepdims=True))
        t5 = taken[48:56]
        for lo in (0, 2, 4):
            cnt.append(jnp.sum(jnp.where((sub8 >= lo) & (sub8 < lo + 2), t5, 0.0), axis=0, keepdims=True))
        for i in range(8):
            cnt.append(taken[56 + i:57 + i])
        n_a = functools.reduce(lambda acc, k: jnp.where(rk[0] == float(k), cnt[k], acc), range(P_TOPK),
                               jnp.zeros((P_NKEYS, tb), f32))
        ta_ref[p, 0] = n_a
        ta_ref[p, 1] = jnp.exp(sc[0] - v0[0])
        tb_ref[p, 0] = rk[1]
        tb_ref[p, 1] = jnp.exp(sc[1] - v1[0]) / z


def _route(hb, wqt, keys, tb):
    n = hb.shape[0]
    spec = pl.BlockSpec((P_HEADS, 2, P_NKEYS, tb), lambda i: (0, 0, 0, i))
    return pl.pallas_call(
        _route_kernel,
        grid=(n // tb,),
        in_specs=[pl.BlockSpec((tb, D_MODEL), lambda i: (i, 0)), _full(wqt.shape), _full(keys.shape)],
        out_specs=[spec, spec],
        out_shape=[jax.ShapeDtypeStruct((P_HEADS, 2, P_NKEYS, n), f32),
                   jax.ShapeDtypeStruct((P_HEADS, 2, P_NKEYS, n), f32)],
        compiler_params=_cparams("parallel"),
        name="peer_route",
    )(hb, wqt, keys)


GATE_ROWS = 64


def _experts_kernel(hb_ref, h_ref, ta_ref, tb_ref, u_ref, vt_ref, g_ref, b_ref, y_ref, acc_s, ht_s, pt_s, *, te):
    j = pl.program_id(1)

    @pl.when(j == 0)
    def _():
        acc_s[...] = jnp.zeros(acc_s.shape, f32)

    ht_s[...] = lax.dot_general(u_ref[...], hb_ref[...], _NT_DIMS, preferred_element_type=f32)
    for aa in range(te // P_NKEYS):
        a = j * (te // P_NKEYS) + aa
        n_rows = [ta_ref[p, 0, pl.ds(a, 1), :] for p in range(P_HEADS)]
        e0_rows = [ta_ref[p, 1, pl.ds(a, 1), :] for p in range(P_HEADS)]
        for lt in range(ht_s.shape[1] // LANES):
            ls = slice(lt * LANES, (lt + 1) * LANES)
            for half in range(2):
                bs_ = slice(half * GATE_ROWS, (half + 1) * GATE_ROWS)
                ex = slice(aa * P_NKEYS + half * GATE_ROWS, aa * P_NKEYS + (half + 1) * GATE_ROWS)
                w = jnp.zeros((GATE_ROWS, LANES), f32)
                for p in range(P_HEADS):
                    w = w + jnp.where(tb_ref[p, 0, bs_, ls] < n_rows[p][:, ls], e0_rows[p][:, ls] * tb_ref[p, 1, bs_, ls],
                                      0.0)
                hs = ht_s[ex, ls]
                act = 0.5 * hs * (1.0 + lax.erf(hs * (0.5 ** 0.5)))
                pt_s[ex, ls] = (w * act).astype(bf16)
    acc_s[...] += jnp.dot(vt_ref[...], pt_s[...], preferred_element_type=f32)

    @pl.when(j == pl.num_programs(1) - 1)
    def _():
        z = DN_ALPHA * h_ref[...] + acc_s[...].T
        y_ref[...] = _layer_norm(z, g_ref[...], b_ref[...])


def _experts(hb, h, ta, tbl, u, vt, g, b, tb, te):
    n = hb.shape[0]
    row = pl.BlockSpec((tb, D_MODEL), lambda i, j: (i, 0))
    tab = pl.BlockSpec((P_HEADS, 2, P_NKEYS, tb), lambda i, j: (0, 0, 0, i))
    return pl.pallas_call(
        functools.partial(_experts_kernel, te=te),
        grid=(n // tb, P_EXPERTS // te),
        in_specs=[row, row, tab, tab,
                  pl.BlockSpec((te, D_MODEL), lambda i, j: (j, 0)),
                  pl.BlockSpec((D_MODEL, te), lambda i, j: (0, j)),
                  pl.BlockSpec((1, D_MODEL), lambda i, j: (0, 0)),
                  pl.BlockSpec((1, D_MODEL), lambda i, j: (0, 0))],
        out_specs=row,
        out_shape=jax.ShapeDtypeStruct((n, D_MODEL), f32),
        scratch_shapes=[pltpu.VMEM((D_MODEL, tb), f32), pltpu.VMEM((te, tb), f32), pltpu.VMEM((te, tb), bf16)],
        compiler_params=_cparams("parallel", "arbitrary"),
        name="peer_experts",
    )(hb, h, ta, tbl, u, vt, g, b)


def _to_q_tiles(q2, nb, t, tq):
    q = q2.reshape(nb, t // tq, tq, B_KV, B_REP, B_HD).transpose(0, 3, 1, 4, 2, 5)
    return q.reshape(nb, B_KV, t // tq, B_REP * tq, B_HD).astype(bf16)


def _from_q_tiles(o, nb, t, tq):
    o = o.reshape(nb, B_KV, t // tq, B_REP, tq, B_HD).transpose(0, 2, 4, 1, 3, 5)
    return o.reshape(nb, t, B_HEADS * B_HD)


def _kv_heads(kv3):
    nb, tk, _ = kv3.shape
    kv = kv3.reshape(nb, tk, 2, B_KV, B_HD).transpose(2, 0, 3, 1, 4)
    return kv[0], kv[1]


def _nsa(slopes, proj, gsmall, pe2, w2, *, nb, t, tq, row0, qpos0, cmp_src, slc3, win3, win_kpos0, tk_sel, tk_win):
    qt = _to_q_tiles(proj[row0:row0 + nb * t, C_BQ:C_CMP], nb, t, tq)
    x2, cmp_rows, cmp_steps, cmp_colblk = cmp_src
    kvc = _compress(x2, pe2, w2, rows=cmp_rows, steps=cmp_steps, row0=0, colblk=cmp_colblk)
    kc, vc = _kv_heads(kvc.reshape(nb, -1, KV_COLS))
    o_cmp, sel = _cmp_topk(slopes, qt, kc, vc, gsmall, tq=tq, qpos0=qpos0, row0=row0)
    ks, vs = _kv_heads(slc3)
    o_slc = _attn(slopes, qt, ks.astype(bf16), vs.astype(bf16), gsmall, sel, tq=tq, tk=tk_sel, qpos0=qpos0,
                  kpos0=0, window=None, gate_lane=G_GATE + B_HEADS, row0=row0)
    kw, vw = _kv_heads(win3)
    o_win = _attn(slopes, qt, kw.astype(bf16), vw.astype(bf16), gsmall, None, tq=tq, tk=tk_win, qpos0=qpos0,
                  kpos0=win_kpos0, window=WINDOW, gate_lane=G_GATE + 2 * B_HEADS, row0=row0)
    return tuple(_from_q_tiles(o, nb, t, tq).reshape(nb * t, B_HEADS * B_HD) for o in (o_cmp, o_slc, o_win))


def kernel(x_prompt, x_sample, cache_cmp_kv, cache_slc_kv, cache_win_kv, state_C, state_n, state_m, page_table,
           w_in, b_in, norm_a_g, nsa_pe, nsa_w_cmp, w_br_a, w_br_b, w_merge, w_out, ln1_g, ln1_b,
           peer_wq, peer_keys, peer_u, peer_v, ln2_g, ln2_b):
    bp, tp, _ = x_prompt.shape
    bs, ts, _ = x_sample.shape
    tsp = 8
    n_p, n_s = bp * tp, bs * tsp
    past = page_table.shape[1] * PAGE_SIZE

    perm = np.concatenate([np.arange(0, 2048), np.arange(2056, 5640), np.arange(2048, 2056), np.arange(5640, 5688)])
    w_perm = jnp.pad(w_in[:, perm], ((0, 0), (0, C_END - perm.size)))
    b_perm = jnp.pad(b_in[perm], (0, C_END - perm.size))
    w_perm_b = w_perm.astype(bf16)
    slopes = jnp.asarray(2.0 ** (-8.0 * np.arange(1, B_HEADS + 1) / B_HEADS), f32)
    wc = nsa_w_cmp.reshape(2, CMP_BLOCK, 1, B_HD, 1, B_HD)
    eye2 = jnp.eye(2, dtype=f32).reshape(1, 1, 2, 1, 2, 1)
    w2 = (wc * eye2).reshape(2, CMP_BLOCK * LANES, LANES).astype(bf16)
    pe2 = jnp.tile(nsa_pe, (1, 1, 2)).reshape(2, 1, CMP_BLOCK * LANES)

    xs_pad = jnp.pad(x_sample, ((0, 0), (0, tsp - ts), (0, 0)))
    x_all = jnp.concatenate([x_prompt.reshape(n_p, D_MODEL), xs_pad.reshape(n_s, D_MODEL)], axis=0)
    xb = x_all.astype(bf16)
    proj, kvb = _proj(xb, w_perm_b, b_perm.reshape(1, C_END), 256)
    gt = _proj_t(w_perm_b[:, C_SMALL:].T, xb, b_perm[C_SMALL:].reshape(LANES, 1), 512)

    zc = jnp.zeros((bp, A_HEADS, A_DQK, A_DV), f32)
    zn = jnp.zeros((bp, A_HEADS, 1, A_DQK), f32)
    zm = jnp.zeros((bp, A_HEADS, 1, 1), f32)
    ng = norm_a_g.reshape(1, A_HEADS * A_DV)
    ya, p_c, p_n, p_m = _mlstm(proj, gt, ng, zc, zn, zm, row0=0, nb=bp, t=tp, L=256, valid=256)
    ya, s_c, s_n, s_m = _mlstm(proj, gt, ng, state_C, state_n.reshape(bs, A_HEADS, 1, A_DQK),
                               state_m.reshape(bs, A_HEADS, 1, 1), row0=n_p, nb=bs, t=tsp, L=tsp, valid=ts, y_buf=ya)

    wt_kv = w_perm_b[:, C_CMP:C_SMALL].T
    b_kv = b_perm[C_CMP:C_SMALL].reshape(C_SMALL - C_CMP, 1)
    kvt_p = _proj_kvt(wt_kv, xb, b_kv, row0=0, nb=bp, t=tp, tn=512)
    kvt_s = _proj_kvt(wt_kv, xb, b_kv, row0=n_p, nb=1, t=n_s, tn=512)
    kvt_s = [a.reshape(KV_COLS, bs, tsp).transpose(1, 0, 2) for a in kvt_s]
    to_rows = lambda a: a.reshape(a.shape[0], 2, B_KV, B_HD, a.shape[2]).transpose(0, 4, 1, 2, 3)
    new_lanes = lambda a: jnp.pad(a, ((0, 0), (0, 0), (0, LANES - tsp)))

    kvc_p = _compress(proj, pe2, w2, rows=n_p, steps=1, row0=0, colblk=C_CMP // KV_COLS)
    oc, sel_p = _cmp_nat(slopes, proj, kvc_p.reshape(bp, tp // CMP_BLOCK, KV_COLS), nb=bp, t=tp, tq=128)
    os_ = _attn_nat(slopes, proj, kvb, sel_p, nb=bp, t=tp, tq=128, tk=512, branch=1)
    ow = _attn_nat(slopes, proj, kvb, None, nb=bp, t=tp, tq=128, tk=512, branch=2)

    tk_s = past + PAD_PAGES * PAGE_SIZE
    wb = cache_win_kv.shape[1]
    cache_t = lambda c: c.transpose(0, 2, 3, 4, 1).reshape(c.shape[0], KV_COLS, c.shape[1])
    kvc_s = _compress_pages(page_table, cache_t(cache_cmp_kv), pe2, w2)
    kvc_s = jnp.pad(kvc_s, ((0, 0), (0, tk_s // CMP_BLOCK - kvc_s.shape[1]), (0, 0)))
    win_t = cache_t(cache_win_kv)
    s_win_t = jnp.concatenate([win_t[:, :, ts:], kvt_s[2][:, :, :ts]], axis=2)
    tk_sel = past + PAGE_SIZE
    nblk_s = tk_s // CMP_BLOCK
    expand = jnp.asarray(np.arange(tk_sel)[None, :] // CMP_BLOCK == np.arange(nblk_s)[:, None], bf16)
    oc, os_, ow = _decode(page_table, (oc, os_, ow), slopes, proj, kvc_s.reshape(bs, nblk_s, KV_COLS),
                          cache_t(cache_slc_kv), new_lanes(kvt_s[1]), win_t, new_lanes(kvt_s[2]), expand,
                          tq=tsp, row0=n_p, qpos0=past, win_kpos0=past - wb)

    h1, h1b = _tail(x_all, ya, oc, os_, ow, w_merge.astype(bf16), w_br_a.astype(bf16), w_br_b.astype(bf16),
                    w_out.astype(bf16), ln1_g.reshape(1, D_MODEL), ln1_b.reshape(1, D_MODEL), 256)
    tab_a, tab_b = _route(h1b, peer_wq.T.astype(bf16), peer_keys.astype(bf16), 256)
    y = _experts(h1b, h1, tab_a, tab_b, peer_u.astype(bf16), peer_v.T.astype(bf16), ln2_g.reshape(1, D_MODEL),
                 ln2_b.reshape(1, D_MODEL), 512, 1024)

    y_prompt = y[:n_p].reshape(bp, tp, D_MODEL)
    y_sample = y[n_p:].reshape(bs, tsp, D_MODEL)[:, :ts]
    dt = x_prompt.dtype
    return (y_prompt, y_sample, to_rows(kvt_p[0]), to_rows(kvt_p[1]), to_rows(kvt_p[2][:, :, -min(WINDOW, tp):]),
            p_c.astype(dt), p_n.astype(dt), p_m.astype(dt),
            to_rows(kvt_s[0][:, :, :ts]), to_rows(kvt_s[1][:, :, :ts]), to_rows(s_win_t),
            s_c.astype(state_C.dtype), s_n.astype(state_C.dtype), s_m.astype(state_C.dtype))
```

```python
import functools

import jax
import jax.numpy as jnp
import numpy as np
from jax import lax
from jax.experimental import pallas as pl
from jax.experimental.pallas import tpu as pltpu

D_MODEL = 1024
A_HEADS, A_DQK, A_DV = 4, 128, 256
B_HEADS, B_KV, B_HD = 16, 4, 64
B_REP = B_HEADS // B_KV
CMP_BLOCK = 64
N_SEL = 16
WINDOW = 512
PAGE_SIZE = 128
P_HEADS, P_NKEYS, P_DHALF, P_TOPK = 8, 128, 128, 16
P_EXPERTS = P_NKEYS * P_NKEYS
DN_ALPHA = 2.0 ** 0.25
LN_EPS = 1e-5
NEG = -1e30

LANES = 128
KV_COLS = 2 * B_KV * B_HD
VMEM_LIMIT = 56 * 1024 * 1024

C_AQ, C_AK, C_AV, C_AO, C_BQ, C_CMP, C_SLC, C_WIN, C_SMALL, C_END = (
    0, 512, 1024, 2048, 3072, 4096, 4608, 5120, 5632, 5760)
G_I, G_F, G_GATE = 0, A_HEADS, 2 * A_HEADS

bf16 = jnp.bfloat16
f32 = jnp.float32
_NT_DIMS = (((1,), (1,)), ((), ()))


def _cparams(*sem, flags=None):
    return pltpu.CompilerParams(dimension_semantics=sem, vmem_limit_bytes=VMEM_LIMIT, flags=flags)


def _full(shape):
    nd = len(shape)
    return pl.BlockSpec(shape, lambda *_: (0,) * nd)


def _proj_kernel(x_ref, w_ref, b_ref, o_ref, kvb_ref):
    res = jnp.dot(x_ref[...], w_ref[...], preferred_element_type=f32) + b_ref[...]
    o_ref[...] = res
    kvb_ref[...] = res[:, C_CMP:C_SMALL].astype(bf16)


def _proj(xb, w, b, tm):
    n, k = xb.shape
    e = w.shape[1]
    return pl.pallas_call(
        _proj_kernel,
        grid=(n // tm,),
        in_specs=[pl.BlockSpec((tm, k), lambda i: (i, 0)), _full((k, e)), _full((1, e))],
        out_specs=[pl.BlockSpec((tm, e), lambda i: (i, 0)), pl.BlockSpec((tm, C_SMALL - C_CMP), lambda i: (i, 0))],
        out_shape=[jax.ShapeDtypeStruct((n, e), f32), jax.ShapeDtypeStruct((n, C_SMALL - C_CMP), bf16)],
        compiler_params=_cparams("parallel"),
        name="proj",
    )(xb, w, b)


def _proj_t_kernel(wt_ref, x_ref, b_ref, o_ref):
    o_ref[...] = lax.dot_general(wt_ref[...], x_ref[...], (((1,), (1,)), ((), ())),
                                 preferred_element_type=f32) + b_ref[...]


def _proj_t(wt, xb, bcol, tn):
    e, k = wt.shape
    n = xb.shape[0]
    return pl.pallas_call(
        _proj_t_kernel,
        grid=(n // tn,),
        in_specs=[_full((e, k)), pl.BlockSpec((tn, k), lambda i: (i, 0)), _full((e, 1))],
        out_specs=pl.BlockSpec((e, tn), lambda i: (0, i)),
        out_shape=jax.ShapeDtypeStruct((e, n), f32),
        compiler_params=_cparams("parallel"),
        name="proj_t",
    )(wt, xb, bcol)


def _proj_kvt_kernel(wt_ref, x_ref, b_ref, oc_ref, os_ref, ow_ref):
    res = lax.dot_general(wt_ref[...], x_ref[...], _NT_DIMS, preferred_element_type=f32) + b_ref[...]
    for i, o_ref in enumerate((oc_ref, os_ref, ow_ref)):
        o_ref[0] = res[i * KV_COLS:(i + 1) * KV_COLS]


def _proj_kvt(wt, xb, bcol, *, row0, nb, t, tn):
    e, k = wt.shape
    nt = t // tn
    rb0 = row0 // tn
    out = pl.BlockSpec((1, KV_COLS, tn), lambda b, i: (b, 0, i))
    return pl.pallas_call(
        _proj_kvt_kernel,
        grid=(nb, nt),
        in_specs=[_full((e, k)), pl.BlockSpec((tn, k), lambda b, i: (rb0 + b * nt + i, 0)), _full((e, 1))],
        out_specs=[out, out, out],
        out_shape=[jax.ShapeDtypeStruct((nb, KV_COLS, t), f32)] * 3,
        compiler_params=_cparams("parallel", "parallel"),
        name="proj_kvt",
    )(wt, xb, bcol)


def _mlstm_kernel(q_ref, k_ref, v_ref, ao_ref, g_ref, gt_ref, ng_ref, c0_ref, n0_ref, m0_ref,
                  y_ref, c_out, n_out, m_out, c_s, n_s, m_s, *, L, valid):
    c = pl.program_id(1)

    @pl.when(c == 0)
    def _():
        c_s[...] = c0_ref[0]
        n_s[...] = n0_ref[0]
        m_s[...] = m0_ref[0]

    g = g_ref[...]
    gt = gt_ref[0]
    t_col = lax.broadcasted_iota(jnp.int32, (L, 1), 0)
    s_row = lax.broadcasted_iota(jnp.int32, (1, L), 1)
    tt = lax.broadcasted_iota(jnp.int32, (L, L), 0)
    ss = lax.broadcasted_iota(jnp.int32, (L, L), 1)
    causal = ss <= tt
    for h in range(A_HEADS):
        q = q_ref[:, h * A_DQK:(h + 1) * A_DQK]
        k = k_ref[:, h * A_DQK:(h + 1) * A_DQK] * (A_DQK ** -0.5)
        v = v_ref[:, h * A_DV:(h + 1) * A_DV]
        i_col, f_col = g[:, G_I + h:G_I + h + 1], g[:, G_F + h:G_F + h + 1]
        i_row, f_row = gt[G_I + h:G_I + h + 1, :], gt[G_F + h:G_F + h + 1, :]
        lf_col = jax.nn.log_sigmoid(f_col)
        lf_row = jax.nn.log_sigmoid(f_row)
        if valid < L:
            lf_col = jnp.where(t_col < valid, lf_col, 0.0)
            lf_row = jnp.where(s_row < valid, lf_row, 0.0)
            i_col = jnp.where(t_col < valid, i_col, NEG)
            i_row = jnp.where(s_row < valid, i_row, NEG)
        b_col = jnp.sum(jnp.where(causal, lf_row, 0.0), axis=1, keepdims=True)
        b_row = jnp.sum(jnp.where(tt <= ss, lf_col, 0.0), axis=0, keepdims=True)
        m_prev = m_s[h]
        cmat = c_s[h]
        n_row = n_s[h]

        d_log = jnp.where(causal, b_col - b_row + i_row, NEG)
        inter = b_col + m_prev
        m_t = jnp.maximum(inter, jnp.max(d_log, axis=1, keepdims=True))
        qb = q.astype(bf16)
        qk = lax.dot_general(qb, k.astype(bf16), _NT_DIMS, preferred_element_type=f32)
        smat = qk * jnp.exp(d_log - m_t)
        w_inter = jnp.exp(inter - m_t)
        vb = v.astype(bf16)
        num = (w_inter * jnp.dot(qb, cmat.astype(bf16), preferred_element_type=f32)
               + jnp.dot(smat.astype(bf16), vb, preferred_element_type=f32))
        den = w_inter * jnp.sum(q * n_row, axis=1, keepdims=True) + jnp.sum(smat, axis=1, keepdims=True)
        hid = num / jnp.maximum(jnp.abs(den), jnp.exp(-m_t))
        mu = jnp.mean(hid, axis=1, keepdims=True)
        var = jnp.mean(jnp.square(hid - mu), axis=1, keepdims=True)
        hid = (hid - mu) * lax.rsqrt(var + LN_EPS) * ng_ref[:, h * A_DV:(h + 1) * A_DV]
        y_ref[:, h * A_DV:(h + 1) * A_DV] = hid * jax.nn.sigmoid(ao_ref[:, h * A_DV:(h + 1) * A_DV])

        b_end = b_col[L - 1:L, :]
        g_row = b_end - b_row + i_row
        m_new = jnp.maximum(b_end + m_prev, jnp.max(g_row, axis=1, keepdims=True))
        a = jnp.exp(b_end + m_prev - m_new)
        w_col = jnp.exp(b_end - b_col + i_col - m_new)
        kw = k * w_col
        c_s[h] = a * cmat + lax.dot_general(kw.astype(bf16), vb, (((0,), (0,)), ((), ())),
                                            preferred_element_type=f32)
        n_s[h] = a * n_row + jnp.sum(kw, axis=0, keepdims=True)
        m_s[h] = m_new

    @pl.when(c == pl.num_programs(1) - 1)
    def _():
        c_out[0] = c_s[...]
        n_out[0] = n_s[...]
        m_out[0] = m_s[...]


def _into(buf, kern):
    if buf is None:
        return kern, [], [], {}
    return (lambda buf_ref, *refs: kern(*refs)), [pl.BlockSpec(memory_space=pl.ANY)], [buf], {0: 0}


def _mlstm(proj, gt, norm_g, c0, n0, m0, *, row0, nb, t, L, valid, y_buf=None):
    nc = t // L
    rb0 = row0 // L
    gt = gt[:8, row0:row0 + nb * t].reshape(8, nb * nc, L).transpose(1, 0, 2)
    rows = lambda b, c: rb0 + b * nc + c
    st = lambda b, c: (b, 0, 0, 0)
    qk_w, v_w = A_HEADS * A_DQK, A_HEADS * A_DV
    kern, alias_specs, alias_args, aliases = _into(y_buf, functools.partial(_mlstm_kernel, L=L, valid=valid))
    y, c_f, n_f, m_f = pl.pallas_call(
        kern,
        grid=(nb, nc),
        input_output_aliases=aliases,
        in_specs=alias_specs + [
            pl.BlockSpec((L, qk_w), lambda b, c: (rows(b, c), C_AQ // qk_w)),
            pl.BlockSpec((L, qk_w), lambda b, c: (rows(b, c), C_AK // qk_w)),
            pl.BlockSpec((L, v_w), lambda b, c: (rows(b, c), C_AV // v_w)),
            pl.BlockSpec((L, v_w), lambda b, c: (rows(b, c), C_AO // v_w)),
            pl.BlockSpec((L, LANES), lambda b, c: (rows(b, c), C_SMALL // LANES)),
            pl.BlockSpec((1, 8, L), lambda b, c: (b * nc + c, 0, 0)),
            pl.BlockSpec((1, v_w), lambda b, c: (0, 0)),
            pl.BlockSpec((1, A_HEADS, A_DQK, A_DV), st),
            pl.BlockSpec((1, A_HEADS, 1, A_DQK), st),
            pl.BlockSpec((1, A_HEADS, 1, 1), st),
        ],
        out_specs=[
            pl.BlockSpec((L, v_w), lambda b, c: (rows(b, c), 0)),
            pl.BlockSpec((1, A_HEADS, A_DQK, A_DV), st),
            pl.BlockSpec((1, A_HEADS, 1, A_DQK), st),
            pl.BlockSpec((1, A_HEADS, 1, 1), st),
        ],
        out_shape=[
            jax.ShapeDtypeStruct((proj.shape[0], A_HEADS * A_DV), f32),
            jax.ShapeDtypeStruct((nb, A_HEADS, A_DQK, A_DV), f32),
            jax.ShapeDtypeStruct((nb, A_HEADS, 1, A_DQK), f32),
            jax.ShapeDtypeStruct((nb, A_HEADS, 1, 1), f32),
        ],
        scratch_shapes=[pltpu.VMEM((A_HEADS, A_DQK, A_DV), f32), pltpu.VMEM((A_HEADS, 1, A_DQK), f32),
                        pltpu.VMEM((A_HEADS, 1, 1), f32)],
        compiler_params=_cparams("parallel", "arbitrary"),
        name="mlstm",
    )(*alias_args, proj, proj, proj, proj, proj, gt, norm_g, c0, n0, m0)
    return y, c_f, n_f[:, :, 0], m_f[:, :, 0, 0]


def _compress_kernel(x_ref, pe_ref, w_ref, o_ref, xf_ref, *, nblk):
    for l in range(CMP_BLOCK):
        xf_ref[:, l * LANES:(l + 1) * LANES] = x_ref[pl.ds(l, nblk, stride=CMP_BLOCK), :]
    xf = (xf_ref[...] + pe_ref[0]).astype(bf16)
    o_ref[...] = jnp.dot(xf, w_ref[0], preferred_element_type=f32)


def _compress(x2, pe2, w2, *, rows, steps, row0, colblk):
    nblk = rows // CMP_BLOCK
    kflat = CMP_BLOCK * LANES
    rb0 = row0 // rows
    ngrp = KV_COLS // LANES
    return pl.pallas_call(
        functools.partial(_compress_kernel, nblk=nblk),
        grid=(steps, ngrp),
        in_specs=[pl.BlockSpec((rows, LANES), lambda s, p: (rb0 + s, colblk * ngrp + p)),
                  pl.BlockSpec((1, 1, kflat), lambda s, p: (p // 2, 0, 0)),
                  pl.BlockSpec((1, kflat, LANES), lambda s, p: (p // 2, 0, 0))],
        out_specs=pl.BlockSpec((nblk, LANES), lambda s, p: (s, p)),
        out_shape=jax.ShapeDtypeStruct((steps * nblk, KV_COLS), f32),
        scratch_shapes=[pltpu.VMEM((nblk, kflat), f32)],
        compiler_params=_cparams("parallel", "parallel"),
        name="compress",
    )(x2, pe2, w2)


SEQS_PER_STEP = 8


def _compress_pages_kernel(pt_ref, *refs):
    del pt_ref
    pages = refs[:SEQS_PER_STEP]
    pe_ref, w_ref, o_ref = refs[SEQS_PER_STEP:SEQS_PER_STEP + 3]
    tm = refs[SEQS_PER_STEP + 3:SEQS_PER_STEP + 3 + KV_COLS // LANES]
    xf_s = refs[-1]
    j = pl.program_id(1)
    blocks = SEQS_PER_STEP * PAGE_SIZE // CMP_BLOCK
    for u, page in enumerate(pages):
        rows = page[0].T
        for p, tm_p in enumerate(tm):
            tm_p[u * PAGE_SIZE:(u + 1) * PAGE_SIZE, :] = rows[:, p * LANES:(p + 1) * LANES]
    dst = pl.ds(pl.multiple_of(j * blocks, blocks), blocks)
    for p, tm_p in enumerate(tm):
        for l in range(CMP_BLOCK):
            piece = tm_p[pl.ds(l, blocks, stride=CMP_BLOCK), :] + pe_ref[p // 2, :, l * LANES:(l + 1) * LANES]
            xf_s[p, dst, l * LANES:(l + 1) * LANES] = piece.astype(bf16)

    @pl.when(j == pl.num_programs(1) - 1)
    def _():
        for p in range(len(tm)):
            o_ref[0, :, p * LANES:(p + 1) * LANES] = jnp.dot(xf_s[p], w_ref[p // 2], preferred_element_type=f32)


def _compress_pages(page_table, cache_t, pe2, w2):
    nb, n_pages = page_table.shape
    groups = nb // SEQS_PER_STEP
    kflat = CMP_BLOCK * LANES
    rows = SEQS_PER_STEP * n_pages * PAGE_SIZE // CMP_BLOCK
    ngrp = KV_COLS // LANES
    out = pl.pallas_call(
        _compress_pages_kernel,
        grid_spec=pltpu.PrefetchScalarGridSpec(
            num_scalar_prefetch=1,
            grid=(groups, n_pages),
            in_specs=[pl.BlockSpec((1, KV_COLS, PAGE_SIZE),
                                   functools.partial(lambda s, j, pt, u: (pt[s * SEQS_PER_STEP + u, j], 0, 0), u=u))
                      for u in range(SEQS_PER_STEP)]
            + [pl.BlockSpec((2, 1, kflat), lambda s, j, pt: (0, 0, 0)),
               pl.BlockSpec((2, kflat, LANES), lambda s, j, pt: (0, 0, 0))],
            out_specs=pl.BlockSpec((1, rows, KV_COLS), lambda s, j, pt: (s, 0, 0)),
            scratch_shapes=[pltpu.VMEM((SEQS_PER_STEP * PAGE_SIZE, LANES), f32) for _ in range(ngrp)]
            + [pltpu.VMEM((ngrp, rows, kflat), bf16)],
        ),
        out_shape=jax.ShapeDtypeStruct((groups, rows, KV_COLS), f32),
        compiler_params=_cparams("parallel", "arbitrary"),
        name="compress_pages",
    )(page_table, *([cache_t] * SEQS_PER_STEP), pe2, w2)
    out = out.reshape(groups, n_pages, SEQS_PER_STEP, PAGE_SIZE // CMP_BLOCK, KV_COLS).transpose(0, 2, 1, 3, 4)
    return out.reshape(nb, n_pages * PAGE_SIZE // CMP_BLOCK, KV_COLS)


def _gate_col(g, lane_idx):
    lane = lax.broadcasted_iota(jnp.int32, g.shape, 1)
    return jax.nn.sigmoid(jnp.sum(jnp.where(lane == lane_idx, g, 0.0), axis=1, keepdims=True))


def _cmp_topk_kernel(slope_ref, q_ref, kc_ref, vc_ref, g_ref, o_ref, sel_ref, *, tq, nblk, qpos0):
    gi = pl.program_id(1)
    i = pl.program_id(2)
    qpos = qpos0 + i * tq + lax.broadcasted_iota(jnp.int32, (tq, 1), 0)
    j = lax.broadcasted_iota(jnp.int32, (1, nblk), 1)
    dist = qpos - ((j + 1) * CMP_BLOCK - 1)
    valid = dist >= 0
    distf = dist.astype(f32)
    kc = kc_ref[0, 0].astype(bf16)
    vc = vc_ref[0, 0].astype(bf16)
    g = g_ref[...]
    imp = jnp.zeros((tq, nblk), f32)
    for r in range(B_REP):
        qr = q_ref[0, 0, 0, r * tq:(r + 1) * tq, :]
        s = lax.dot_general(qr, kc, (((1,), (1,)), ((), ())), preferred_element_type=f32) * (B_HD ** -0.5)
        s = s - slope_ref[gi * B_REP + r] * distf
        s = jnp.where(valid, s, NEG)
        e = jnp.exp(s - jnp.max(s, axis=1, keepdims=True))
        p = jnp.where(valid, e / jnp.sum(e, axis=1, keepdims=True), 0.0)
        imp = imp + p
        o = jnp.dot(p.astype(bf16), vc, preferred_element_type=f32)
        o_ref[0, 0, 0, r * tq:(r + 1) * tq, :] = o * _gate_col(g, G_GATE + gi * B_REP + r)
    cur = qpos // CMP_BLOCK
    imp = jnp.where((j == cur) | (j == 0), float(B_REP + 1), imp)
    imp = jnp.where(j > cur, -1.0, imp)
    jf = j.astype(f32)
    sel = jnp.zeros((tq, nblk), f32)
    for _ in range(N_SEL):
        mx = jnp.max(imp, axis=1, keepdims=True)
        idx = jnp.min(jnp.where(imp == mx, jf, float(nblk)), axis=1, keepdims=True)
        hit = jf == idx
        sel = jnp.where(hit, 1.0, sel)
        imp = jnp.where(hit, NEG, imp)
    sel_ref[0, 0] = sel


def _cmp_topk(slopes, qt, kc, vc, gsmall, *, tq, qpos0, row0):
    nb, _, nqt, _, _ = qt.shape
    nblk = kc.shape[2]
    rb0 = row0 // tq
    return pl.pallas_call(
        functools.partial(_cmp_topk_kernel, tq=tq, nblk=nblk, qpos0=qpos0),
        grid=(nb, B_KV, nqt),
        in_specs=[
            pl.BlockSpec(memory_space=pltpu.SMEM),
            pl.BlockSpec((1, 1, 1, B_REP * tq, B_HD), lambda b, g, i: (b, g, i, 0, 0)),
            pl.BlockSpec((1, 1, nblk, B_HD), lambda b, g, i: (b, g, 0, 0)),
            pl.BlockSpec((1, 1, nblk, B_HD), lambda b, g, i: (b, g, 0, 0)),
            pl.BlockSpec((tq, LANES), lambda b, g, i: (rb0 + b * nqt + i, 0)),
        ],
        out_specs=[
            pl.BlockSpec((1, 1, 1, B_REP * tq, B_HD), lambda b, g, i: (b, g, i, 0, 0)),
            pl.BlockSpec((1, 1, tq, nblk), lambda b, g, i: (b, g, i, 0)),
        ],
        out_shape=[
            jax.ShapeDtypeStruct(qt.shape, f32),
            jax.ShapeDtypeStruct((nb, B_KV, nqt * tq, nblk), f32),
        ],
        compiler_params=_cparams("parallel", "parallel", "parallel"),
        name="cmp_topk",
    )(slopes, qt, kc, vc, gsmall)


def _attn_kernel(slope_ref, q_ref, k_ref, v_ref, g_ref, *rest, tq, tk, nkt, nblk, qpos0, kpos0, window,
                 gate_lane):
    if window is None:
        sel_ref, o_ref, m_s, l_s, acc_s = rest
    else:
        o_ref, m_s, l_s, acc_s = rest
    gi = pl.program_id(1)
    i = pl.program_id(2)
    qlo = qpos0 + i * tq
    qpos = qlo + lax.broadcasted_iota(jnp.int32, (tq, 1), 0)
    m_s[...] = jnp.full(m_s.shape, NEG, f32)
    l_s[...] = jnp.zeros(l_s.shape, f32)
    acc_s[...] = jnp.zeros(acc_s.shape, f32)
    kt_hi = jnp.minimum((qlo + tq - 1 - kpos0) // tk + 1, nkt)
    if window is None:
        kt_lo = 0
        selb = sel_ref[0, 0].astype(bf16)
    else:
        kt_lo = jnp.maximum(qlo - (window - 1) - kpos0, 0) // tk

    def body(kt, carry):
        k0 = pl.multiple_of(kt * tk, tk)
        kb = k_ref[0, 0, pl.ds(k0, tk), :]
        vb = v_ref[0, 0, pl.ds(k0, tk), :]
        kidx = k0 + lax.broadcasted_iota(jnp.int32, (1, tk), 1)
        dist = qpos - (kpos0 + kidx)
        mask = dist >= 0
        if window is None:
            blk = lax.broadcasted_iota(jnp.int32, (nblk, tk), 0)
            kblk = (k0 + lax.broadcasted_iota(jnp.int32, (nblk, tk), 1)) // CMP_BLOCK
            expand = jnp.where(blk == kblk, 1.0, 0.0).astype(bf16)
            mask = mask & (jnp.dot(selb, expand, preferred_element_type=f32) > 0.5)
        else:
            mask = mask & (dist < window)
        distf = dist.astype(f32)
        for r in range(B_REP):
            rows = slice(r * tq, (r + 1) * tq)
            s = lax.dot_general(q_ref[0, 0, 0, rows, :], kb, (((1,), (1,)), ((), ())),
                                preferred_element_type=f32) * (B_HD ** -0.5)
            s = jnp.where(mask, s - slope_ref[gi * B_REP + r] * distf, NEG)
            m_old = m_s[rows, :]
            m_new = jnp.maximum(m_old, jnp.max(s, axis=1, keepdims=True))
            alpha = jnp.exp(m_old - m_new)
            p = jnp.exp(s - m_new)
            l_s[rows, :] = alpha * l_s[rows, :] + jnp.sum(p, axis=1, keepdims=True)
            acc_s[rows, :] = alpha * acc_s[rows, :] + jnp.dot(p.astype(bf16), vb, preferred_element_type=f32)
            m_s[rows, :] = m_new
        return carry

    lax.fori_loop(kt_lo, kt_hi, body, 0)
    g = g_ref[...]
    for r in range(B_REP):
        rows = slice(r * tq, (r + 1) * tq)
        o_ref[0, 0, 0, rows, :] = acc_s[rows, :] / l_s[rows, :] * _gate_col(g, gate_lane + gi * B_REP + r)


def _attn(slopes, qt, kh, vh, gsmall, sel, *, tq, tk, qpos0, kpos0, window, gate_lane, row0):
    nb, _, nqt, _, _ = qt.shape
    tkk = kh.shape[2]
    nkt = tkk // tk
    nblk = None if sel is None else sel.shape[3]
    rb0 = row0 // tq
    in_specs = [
        pl.BlockSpec(memory_space=pltpu.SMEM),
        pl.BlockSpec((1, 1, 1, B_REP * tq, B_HD), lambda b, g, i: (b, g, i, 0, 0)),
        pl.BlockSpec((1, 1, tkk, B_HD), lambda b, g, i: (b, g, 0, 0)),
        pl.BlockSpec((1, 1, tkk, B_HD), lambda b, g, i: (b, g, 0, 0)),
        pl.BlockSpec((tq, LANES), lambda b, g, i: (rb0 + b * nqt + i, 0)),
    ]
    args = [slopes, qt, kh, vh, gsmall]
    if sel is not None:
        in_specs.append(pl.BlockSpec((1, 1, tq, nblk), lambda b, g, i: (b, g, i, 0)))
        args.append(sel)
    return pl.pallas_call(
        functools.partial(_attn_kernel, tq=tq, tk=tk, nkt=nkt, nblk=nblk, qpos0=qpos0, kpos0=kpos0,
                          window=window, gate_lane=gate_lane),
        grid=(nb, B_KV, nqt),
        in_specs=in_specs,
        out_specs=pl.BlockSpec((1, 1, 1, B_REP * tq, B_HD), lambda b, g, i: (b, g, i, 0, 0)),
        out_shape=jax.ShapeDtypeStruct(qt.shape, f32),
        scratch_shapes=[pltpu.VMEM((B_REP * tq, 1), f32), pltpu.VMEM((B_REP * tq, 1), f32),
                        pltpu.VMEM((B_REP * tq, B_HD), f32)],
        compiler_params=_cparams("parallel", "parallel", "parallel"),
        name="attn_sel" if window is None else "attn_win",
    )(*args)


def _roll_lanes(x, shift):
    return x if shift == 0 else pltpu.roll(x, shift, axis=1)


def _softmax_rows(s, mask):
    s = jnp.where(mask, s, NEG)
    e = jnp.exp(s - jnp.max(s, axis=1, keepdims=True))
    return jnp.where(mask, e, 0.0), jnp.sum(e, axis=1, keepdims=True)


def _decode_kernel(slope_ref, q_ref, g_ref, kvc_ref, pages, slc_new_ref, win_ref, win_new_ref, exp_ref,
                   oc_ref, os_ref, ow_ref, kt_s, vt_s, wkt_s, wvt_s, *, tq, nblk, qpos0, win_kpos0):
    nrow = B_HEADS * tq
    q = q_ref[...]
    gs = g_ref[...]
    lane_grp = lax.broadcasted_iota(jnp.int32, (tq, B_KV * B_HD), 1) // B_HD
    qm = []
    for g in range(B_KV):
        qg = q[:, g * B_KV * B_HD:(g + 1) * B_KV * B_HD]
        for r in range(B_REP):
            qm.append(jnp.where(lane_grp == g, _roll_lanes(qg, ((g - r) % B_REP) * B_HD), 0.0))
    qm = jnp.concatenate(qm, axis=0).astype(bf16)
    row = lax.broadcasted_iota(jnp.int32, (nrow, 1), 0)
    qpos = qpos0 + row % tq
    slope = functools.reduce(lambda acc, h: jnp.where(row // tq == h, slope_ref[h], acc), range(B_HEADS),
                             jnp.zeros((nrow, 1), f32))
    scale = B_HD ** -0.5
    nt = (((1,), (1,)), ((), ()))

    def emit(o_ref, o, branch):
        for g in range(B_KV):
            acc = jnp.zeros((tq, B_KV * B_HD), f32)
            for r in range(B_REP):
                h = g * B_REP + r
                gate = jax.nn.sigmoid(gs[:, G_GATE + branch * B_HEADS + h:G_GATE + branch * B_HEADS + h + 1])
                oh = jnp.where(lane_grp == g, o[h * tq:(h + 1) * tq, :] * gate, 0.0)
                acc = acc + _roll_lanes(oh, ((r - g) % B_REP) * B_HD)
            o_ref[:, g * B_KV * B_HD:(g + 1) * B_KV * B_HD] = acc

    kvc = kvc_ref[0]
    j = lax.broadcasted_iota(jnp.int32, (1, nblk), 1)
    dist = qpos - ((j + 1) * CMP_BLOCK - 1)
    s = lax.dot_general(qm, kvc[:, :B_KV * B_HD].astype(bf16), nt, preferred_element_type=f32) * scale
    e, l = _softmax_rows(s - slope * dist.astype(f32), dist >= 0)
    p = e / l
    emit(oc_ref, jnp.dot(p.astype(bf16), kvc[:, B_KV * B_HD:].astype(bf16), preferred_element_type=f32), 0)
    imp = jnp.concatenate(
        [functools.reduce(lambda a, b: a + b, [p[(g * B_REP + r) * tq:(g * B_REP + r + 1) * tq] for r in range(B_REP)])
         for g in range(B_KV)], axis=0)
    cur = (qpos0 + lax.broadcasted_iota(jnp.int32, (B_KV * tq, 1), 0) % tq) // CMP_BLOCK
    imp = jnp.where((j == cur) | (j == 0), float(B_REP + 1), imp)
    imp = jnp.where(j > cur, -1.0, imp)
    jf = j.astype(f32)
    sel = jnp.zeros(imp.shape, f32)
    for _ in range(N_SEL):
        mx = jnp.max(imp, axis=1, keepdims=True)
        idx = jnp.min(jnp.where(imp == mx, jf, float(nblk)), axis=1, keepdims=True)
        hit = jf == idx
        sel = jnp.where(hit, 1.0, sel)
        imp = jnp.where(hit, NEG, imp)
    sel_rows = jnp.concatenate([sel[g * tq:(g + 1) * tq] for g in range(B_KV) for _ in range(B_REP)], axis=0)

    def transposed_kv(parts, kt_s, vt_s):
        for u, part in enumerate(parts):
            w = part.shape[-1]
            kt_s[:, u * w:(u + 1) * w] = part[0, :GRP_LANES, :].astype(bf16)
            vt_s[:, u * w:(u + 1) * w] = part[0, GRP_LANES:, :].astype(bf16)

    def attend(kt_s, vt_s, mask):
        s = jnp.dot(qm, kt_s[...], preferred_element_type=f32) * scale
        e, l = _softmax_rows(s - slope * dist.astype(f32), mask)
        return lax.dot_general(e.astype(bf16), vt_s[...], nt, preferred_element_type=f32) / l

    transposed_kv(list(pages) + [slc_new_ref], kt_s, vt_s)
    dist = qpos - lax.broadcasted_iota(jnp.int32, (1, kt_s.shape[1]), 1)
    picked = jnp.dot(sel_rows.astype(bf16), exp_ref[...], preferred_element_type=f32) > 0.5
    emit(os_ref, attend(kt_s, vt_s, picked & (dist >= 0)), 1)

    wb = win_ref.shape[-1]
    wkt_s[:, :wb] = win_ref[0, :GRP_LANES, :].astype(bf16)
    wvt_s[:, :wb] = win_ref[0, GRP_LANES:, :].astype(bf16)
    wkt_s[:, wb:] = win_new_ref[0, :GRP_LANES, :].astype(bf16)
    wvt_s[:, wb:] = win_new_ref[0, GRP_LANES:, :].astype(bf16)
    dist = qpos - (win_kpos0 + lax.broadcasted_iota(jnp.int32, (1, wkt_s.shape[1]), 1))
    emit(ow_ref, attend(wkt_s, wvt_s, (dist >= 0) & (dist < WINDOW)), 2)


def _decode(page_table, bufs, slopes, proj, kvc, slc_cache_t, slc_new_t, win_t, win_new_t, expand, *,
            tq, row0, qpos0, win_kpos0):
    nb, n_pages = page_table.shape
    nblk = kvc.shape[1]
    rb0 = row0 // tq
    tk = (n_pages + 1) * PAGE_SIZE
    twin = win_t.shape[-1] + win_new_t.shape[-1]
    out = pl.BlockSpec((tq, B_HEADS * B_HD), lambda b, pt: (rb0 + b, 0))
    page = (1, KV_COLS, PAGE_SIZE)

    def body(pt_ref, oc_buf, os_buf, ow_buf, slope_ref, q_ref, g_ref, kvc_ref, *refs):
        _decode_kernel(slope_ref, q_ref, g_ref, kvc_ref, refs[:n_pages], *refs[n_pages:], tq=tq, nblk=nblk,
                       qpos0=qpos0, win_kpos0=win_kpos0)

    return pl.pallas_call(
        body,
        grid_spec=pltpu.PrefetchScalarGridSpec(
            num_scalar_prefetch=1,
            grid=(nb,),
            in_specs=[pl.BlockSpec(memory_space=pl.ANY)] * 3
            + [pl.BlockSpec(memory_space=pltpu.SMEM),
               pl.BlockSpec((tq, B_HEADS * B_HD), lambda b, pt: (rb0 + b, C_BQ // (B_HEADS * B_HD))),
               pl.BlockSpec((tq, LANES), lambda b, pt: (rb0 + b, C_SMALL // LANES)),
               pl.BlockSpec((1,) + kvc.shape[1:], lambda b, pt: (b, 0, 0))]
            + [pl.BlockSpec(page, functools.partial(lambda b, pt, u: (pt[b, u], 0, 0), u=u)) for u in range(n_pages)]
            + [pl.BlockSpec((1,) + slc_new_t.shape[1:], lambda b, pt: (b, 0, 0)),
               pl.BlockSpec((1,) + win_t.shape[1:], lambda b, pt: (b, 0, 0)),
               pl.BlockSpec((1,) + win_new_t.shape[1:], lambda b, pt: (b, 0, 0)),
               pl.BlockSpec(expand.shape, lambda b, pt: (0, 0))],
            out_specs=[out, out, out],
            scratch_shapes=[pltpu.VMEM((GRP_LANES, tk), bf16), pltpu.VMEM((GRP_LANES, tk), bf16),
                            pltpu.VMEM((GRP_LANES, twin), bf16), pltpu.VMEM((GRP_LANES, twin), bf16)],
        ),
        out_shape=[jax.ShapeDtypeStruct(b.shape, f32) for b in bufs],
        input_output_aliases={1: 0, 2: 1, 3: 2},
        compiler_params=_cparams("parallel"),
        name="nsa_decode",
    )(page_table, *bufs, slopes, proj, proj, kvc, *([slc_cache_t] * n_pages), slc_new_t, win_t, win_new_t, expand)


GRP_LANES = B_KV * B_HD
_NT = (((1,), (1,)), ((), ()))


def _masked_queries(q, tq):
    lane_grp = lax.broadcasted_iota(jnp.int32, (tq, GRP_LANES), 1) // B_HD
    rows = []
    for g in range(B_KV):
        qg = q[:, g * GRP_LANES:(g + 1) * GRP_LANES]
        for r in range(B_REP):
            rows.append(jnp.where(lane_grp == g, _roll_lanes(qg, ((g - r) % B_REP) * B_HD), 0.0))
    return jnp.concatenate(rows, axis=0).astype(bf16)


def _group_columns(slope_ref, g, tq, qlo):
    row = lax.broadcasted_iota(jnp.int32, (B_REP * tq, 1), 0)
    slope = functools.reduce(lambda acc, r: jnp.where(row // tq == r, slope_ref[g * B_REP + r], acc), range(B_REP),
                             jnp.zeros((B_REP * tq, 1), f32))
    return qlo + row % tq, slope


def _emit_group(o_ref, og, gs, branch, g, tq):
    lane_grp = lax.broadcasted_iota(jnp.int32, (tq, GRP_LANES), 1) // B_HD
    acc = jnp.zeros((tq, GRP_LANES), f32)
    for r in range(B_REP):
        c = G_GATE + branch * B_HEADS + g * B_REP + r
        oh = jnp.where(lane_grp == g, og[r * tq:(r + 1) * tq, :] * jax.nn.sigmoid(gs[:, c:c + 1]), 0.0)
        acc = acc + _roll_lanes(oh, ((r - g) % B_REP) * B_HD)
    o_ref[:, g * GRP_LANES:(g + 1) * GRP_LANES] = acc


def _cmp_nat_kernel(slope_ref, q_ref, g_ref, kvc_ref, o_ref, sel_ref, *, tq, nblk):
    qlo = pl.program_id(1) * tq
    qm = _masked_queries(q_ref[...], tq)
    gs = g_ref[...]
    kvc = kvc_ref[0]
    kc = kvc[:, :GRP_LANES].astype(bf16)
    vc = kvc[:, GRP_LANES:].astype(bf16)
    j = lax.broadcasted_iota(jnp.int32, (1, nblk), 1)
    jf = j.astype(f32)
    imps = []
    for g in range(B_KV):
        qpos, slope = _group_columns(slope_ref, g, tq, qlo)
        dist = qpos - ((j + 1) * CMP_BLOCK - 1)
        s = lax.dot_general(qm[g * B_REP * tq:(g + 1) * B_REP * tq], kc, _NT, preferred_element_type=f32) * (B_HD ** -0.5)
        e, l = _softmax_rows(s - slope * dist.astype(f32), dist >= 0)
        p = e / l
        _emit_group(o_ref, jnp.dot(p.astype(bf16), vc, preferred_element_type=f32), gs, 0, g, tq)
        imps.append(functools.reduce(lambda a, b: a + b, [p[r * tq:(r + 1) * tq] for r in range(B_REP)]))
    imp = jnp.concatenate(imps, axis=0)
    cur = (qlo + lax.broadcasted_iota(jnp.int32, (B_KV * tq, 1), 0) % tq) // CMP_BLOCK
    imp = jnp.where((j == cur) | (j == 0), float(B_REP + 1), imp)
    imp = jnp.where(j > cur, -1.0, imp)
    sel = jnp.zeros(imp.shape, f32)
    for _ in range(N_SEL):
        mx = jnp.max(imp, axis=1, keepdims=True)
        idx = jnp.min(jnp.where(imp == mx, jf, float(nblk)), axis=1, keepdims=True)
        hit = jf == idx
        sel = jnp.where(hit, 1.0, sel)
        imp = jnp.where(hit, NEG, imp)
    sel_ref[0, 0] = sel


def _cmp_nat(slopes, proj, kvc, *, nb, t, tq):
    nqt = t // tq
    nblk = kvc.shape[1]
    return pl.pallas_call(
        functools.partial(_cmp_nat_kernel, tq=tq, nblk=nblk),
        grid=(nb, nqt),
        in_specs=[pl.BlockSpec(memory_space=pltpu.SMEM),
                  pl.BlockSpec((tq, B_HEADS * B_HD), lambda b, i: (b * nqt + i, C_BQ // (B_HEADS * B_HD))),
                  pl.BlockSpec((tq, LANES), lambda b, i: (b * nqt + i, C_SMALL // LANES)),
                  pl.BlockSpec((1, nblk, KV_COLS), lambda b, i: (b, 0, 0))],
        out_specs=[pl.BlockSpec((tq, B_HEADS * B_HD), lambda b, i: (b * nqt + i, 0)),
                   pl.BlockSpec((1, 1, B_KV * tq, nblk), lambda b, i: (b, i, 0, 0))],
        out_shape=[jax.ShapeDtypeStruct((proj.shape[0], B_HEADS * B_HD), f32),
                   jax.ShapeDtypeStruct((nb, nqt, B_KV * tq, nblk), f32)],
        compiler_params=_cparams("parallel", "parallel"),
        name="nsa_cmp",
    )(slopes, proj, proj, kvc)


def _sel_nat_kernel(slope_ref, q_ref, g_ref, kv_ref, sel_ref, o_ref, qm_s, m_s, l_s, acc_s, *, tq, tk, nblk):
    qlo = pl.program_id(1) * tq
    qm_s[...] = _masked_queries(q_ref[...] * (B_HD ** -0.5), tq)
    m_s[...] = jnp.full(m_s.shape, NEG, f32)
    l_s[...] = jnp.zeros(l_s.shape, f32)
    acc_s[...] = jnp.zeros(acc_s.shape, f32)
    selb = sel_ref[0, 0].astype(bf16)
    qpos = qlo + lax.broadcasted_iota(jnp.int32, (tq, 1), 0)
    grows = B_REP * tq

    def body(kt, carry):
        k0 = pl.multiple_of(kt * tk, tk)
        kb = kv_ref[pl.ds(k0, tk), :GRP_LANES]
        vb = kv_ref[pl.ds(k0, tk), GRP_LANES:]
        dist = qpos - (k0 + lax.broadcasted_iota(jnp.int32, (1, tk), 1))
        distf = dist.astype(f32)
        blk = lax.broadcasted_iota(jnp.int32, (nblk, tk), 0)
        kblk = (k0 + lax.broadcasted_iota(jnp.int32, (nblk, tk), 1)) // CMP_BLOCK
        picked = jnp.dot(selb, jnp.where(blk == kblk, 1.0, 0.0).astype(bf16), preferred_element_type=f32)
        for g in range(B_KV):
            sg = lax.dot_general(qm_s[g * grows:(g + 1) * grows, :], kb, _NT, preferred_element_type=f32)
            amask = jnp.where((picked[g * tq:(g + 1) * tq] > 0.5) & (dist >= 0), 0.0, NEG)
            ps, alphas = [], []
            for r in range(B_REP):
                rows = slice((g * B_REP + r) * tq, (g * B_REP + r + 1) * tq)
                s = sg[r * tq:(r + 1) * tq] - slope_ref[g * B_REP + r] * distf + amask
                m_old = m_s[rows, :]
                m_new = jnp.maximum(m_old, jnp.max(s, axis=1, keepdims=True))
                alpha = jnp.exp(m_old - m_new)
                p = jnp.exp(s - m_new)
                l_s[rows, :] = alpha * l_s[rows, :] + jnp.sum(p, axis=1, keepdims=True)
                m_s[rows, :] = m_new
                ps.append(p.astype(bf16))
                alphas.append(alpha)
            grp = slice(g * grows, (g + 1) * grows)
            acc_s[grp, :] = (jnp.concatenate(alphas, axis=0) * acc_s[grp, :]
                             + jnp.dot(jnp.concatenate(ps, axis=0), vb, preferred_element_type=f32))
        return carry

    lax.fori_loop(0, (qlo + tq - 1) // tk + 1, body, 0)
    gs = g_ref[...]
    for g in range(B_KV):
        rows = slice(g * grows, (g + 1) * grows)
        _emit_group(o_ref, acc_s[rows, :] / l_s[rows, :], gs, 1, g, tq)


def _win_nat_kernel(slope_ref, q_ref, g_ref, kv_ref, o_ref, *, tq, t):
    qlo = pl.program_id(1) * tq
    span = WINDOW + tq
    k0 = pl.multiple_of(jnp.clip(qlo - WINDOW, 0, t - span), LANES)
    qm = _masked_queries(q_ref[...] * (B_HD ** -0.5), tq)
    gs = g_ref[...]
    kb = kv_ref[pl.ds(k0, span), :GRP_LANES]
    vb = kv_ref[pl.ds(k0, span), GRP_LANES:]
    dist = qlo + lax.broadcasted_iota(jnp.int32, (tq, 1), 0) - (k0 + lax.broadcasted_iota(jnp.int32, (1, span), 1))
    distf = dist.astype(f32)
    amask = jnp.where((dist >= 0) & (dist < WINDOW), 0.0, NEG)
    grows = B_REP * tq
    for g in range(B_KV):
        sg = lax.dot_general(qm[g * grows:(g + 1) * grows], kb, _NT, preferred_element_type=f32)
        es, ls = [], []
        for r in range(B_REP):
            s = sg[r * tq:(r + 1) * tq] - slope_ref[g * B_REP + r] * distf + amask
            e = jnp.exp(s - jnp.max(s, axis=1, keepdims=True))
            es.append(e.astype(bf16))
            ls.append(jnp.sum(e, axis=1, keepdims=True))
        og = jnp.dot(jnp.concatenate(es, axis=0), vb, preferred_element_type=f32) / jnp.concatenate(ls, axis=0)
        _emit_group(o_ref, og, gs, 2, g, tq)


def _attn_nat(slopes, proj, kvb, sel, *, nb, t, tq, tk, branch):
    nqt = t // tq
    in_specs = [pl.BlockSpec(memory_space=pltpu.SMEM),
                pl.BlockSpec((tq, B_HEADS * B_HD), lambda b, i: (b * nqt + i, C_BQ // (B_HEADS * B_HD))),
                pl.BlockSpec((tq, LANES), lambda b, i: (b * nqt + i, C_SMALL // LANES)),
                pl.BlockSpec((t, KV_COLS), lambda b, i: (b, branch))]
    args = [slopes, proj, proj, kvb]
    if sel is None:
        body, scratch, name = functools.partial(_win_nat_kernel, tq=tq, t=t), [], "nsa_win"
    else:
        nblk = sel.shape[3]
        in_specs.append(pl.BlockSpec((1, 1, B_KV * tq, nblk), lambda b, i: (b, i, 0, 0)))
        args.append(sel)
        body = functools.partial(_sel_nat_kernel, tq=tq, tk=tk, nblk=nblk)
        scratch = [pltpu.VMEM((B_HEADS * tq, GRP_LANES), bf16), pltpu.VMEM((B_HEADS * tq, 1), f32),
                   pltpu.VMEM((B_HEADS * tq, 1), f32), pltpu.VMEM((B_HEADS * tq, GRP_LANES), f32)]
        name = "nsa_sel"
    return pl.pallas_call(
        body,
        grid=(nb, nqt),
        in_specs=in_specs,
        out_specs=pl.BlockSpec((tq, B_HEADS * B_HD), lambda b, i: (b * nqt + i, 0)),
        out_shape=jax.ShapeDtypeStruct((proj.shape[0], B_HEADS * B_HD), f32),
        scratch_shapes=scratch,
        compiler_params=_cparams("parallel", "parallel"),
        name=name,
    )(*args)


PAD_PAGES = 4


def _gather_kernel(pt_ref, *refs):
    del pt_ref
    pages, tail_ref, o_ref = refs[:-2], refs[-2], refs[-1]
    for u, page in enumerate(pages):
        o_ref[0, u] = page[0]
    o_ref[0, len(pages)] = tail_ref[0]
    for u in range(len(pages) + 1, len(pages) + PAD_PAGES):
        o_ref[0, u] = jnp.zeros(o_ref.shape[2:], f32)


def _gather_pages(page_table, cache, tail):
    nb, n_pages = page_table.shape
    page = (1, PAGE_SIZE, KV_COLS)
    return pl.pallas_call(
        _gather_kernel,
        grid_spec=pltpu.PrefetchScalarGridSpec(
            num_scalar_prefetch=1,
            grid=(nb,),
            in_specs=[pl.BlockSpec(page, functools.partial(lambda b, pt, u: (pt[b, u], 0, 0), u=u))
                      for u in range(n_pages)] + [pl.BlockSpec(page, lambda b, pt: (b, 0, 0))],
            out_specs=pl.BlockSpec((1, n_pages + PAD_PAGES, PAGE_SIZE, KV_COLS), lambda b, pt: (b, 0, 0, 0)),
        ),
        out_shape=jax.ShapeDtypeStruct((nb, n_pages + PAD_PAGES, PAGE_SIZE, KV_COLS), f32),
        compiler_params=_cparams("parallel"),
        name="gather_pages",
    )(page_table, *([cache] * n_pages), tail)


def _layer_norm(z, g, b):
    mu = jnp.mean(z, axis=1, keepdims=True)
    var = jnp.mean(jnp.square(z - mu), axis=1, keepdims=True)
    return (z - mu) * lax.rsqrt(var + LN_EPS) * g + b


def _tail_kernel(x_ref, ya_ref, oc_ref, os_ref, ow_ref, wm_ref, wa_ref, wb_ref, wo_ref, g_ref, b_ref,
                 h_ref, hb_ref):
    x = x_ref[...]
    gates = jax.nn.sigmoid(jnp.dot(x.astype(bf16), wm_ref[...], preferred_element_type=f32))
    yb = oc_ref[...] + os_ref[...] + ow_ref[...]
    ma = jnp.dot(ya_ref[...].astype(bf16), wa_ref[...], preferred_element_type=f32)
    mb = jnp.dot(yb.astype(bf16), wb_ref[...], preferred_element_type=f32)
    merged = gates[:, :D_MODEL] * ma + gates[:, D_MODEL:] * mb
    z = DN_ALPHA * x + jnp.dot(merged.astype(bf16), wo_ref[...], preferred_element_type=f32)
    h = _layer_norm(z, g_ref[...], b_ref[...])
    h_ref[...] = h
    hb_ref[...] = h.astype(bf16)


def _tail(x, ya, oc, os_, ow, wm, wa, wb, wo, g, b, tm):
    n = x.shape[0]
    row = pl.BlockSpec((tm, D_MODEL), lambda i: (i, 0))
    return pl.pallas_call(
        _tail_kernel,
        grid=(n // tm,),
        in_specs=[row] * 5 + [_full(wm.shape), _full(wa.shape), _full(wb.shape), _full(wo.shape),
                              _full(g.shape), _full(b.shape)],
        out_specs=[row, row],
        out_shape=[jax.ShapeDtypeStruct((n, D_MODEL), f32), jax.ShapeDtypeStruct((n, D_MODEL), bf16)],
        compiler_params=_cparams("parallel"),
        name="tail",
    )(x, ya, oc, os_, ow, wm, wa, wb, wo, g, b)


def _top16(x):
    kk, tb = x.shape
    ji = lax.broadcasted_iota(jnp.int32, (kk, tb), 0).astype(f32)
    rank = jnp.full((kk, tb), float(P_TOPK), f32)
    vals = []
    for k in range(P_TOPK):
        mx = jnp.max(x, axis=0, keepdims=True)
        idx = jnp.min(jnp.where(x == mx, ji, float(kk)), axis=0, keepdims=True)
        hit = ji == idx
        rank = jnp.where(hit, float(k), rank)
        vals.append(mx)
        x = jnp.where(hit, NEG, x)
    return rank, vals


_CAND_ROWS8 = ((1, 8), (2, 5), (3, 4), (4, 3))


def _route_kernel(h_ref, wqt_ref, keys_ref, ta_ref, tb_ref):
    qpt = lax.dot_general(wqt_ref[...], h_ref[...], (((1,), (1,)), ((), ())), preferred_element_type=f32)
    tb = qpt.shape[1]
    sub16 = lax.broadcasted_iota(jnp.int32, (P_TOPK, tb), 0)
    sub8 = lax.broadcasted_iota(jnp.int32, (8, tb), 0)
    for p in range(P_HEADS):
        sc, rk, vl = [], [], []
        for c in range(2):
            qs = qpt[(2 * p + c) * P_DHALF:(2 * p + c + 1) * P_DHALF, :].astype(bf16)
            s = jnp.dot(keys_ref[p, c], qs, preferred_element_type=f32)
            r, v = _top16(s)
            sc.append(s)
            rk.append(r)
            vl.append(v)
        v0, v1 = vl
        col0 = functools.reduce(lambda acc, k: jnp.where(sub16 == k, v0[k], acc), range(P_TOPK), jnp.zeros((P_TOPK, tb), f32))
        col1 = functools.reduce(lambda acc, k: jnp.where(sub16 == k, v1[k], acc), range(P_TOPK), jnp.zeros((P_TOPK, tb), f32))
        segs = [v0[0] + col1]
        for k1, keep in _CAND_ROWS8:
            segs.append(jnp.where(sub8 < keep, v0[k1] + col1[0:8], NEG))
        first = jnp.where(sub8 < 2, v0[5], jnp.where(sub8 < 4, v0[6], v0[7]))
        second = jnp.where(sub8 % 2 == 0, v1[0], v1[1])
        segs.append(jnp.where(sub8 < 6, first + second, NEG))
        segs.append(col0[8:16] + v1[0])
        cand = jnp.concatenate(segs, axis=0)
        crank, cvals = _top16(cand)
        taken = jnp.where(crank < float(P_TOPK), 1.0, 0.0)
        z = functools.reduce(lambda acc, v: acc + jnp.exp(v - cvals[0]), cvals, jnp.zeros((1, tb), f32))
        cnt = [jnp.sum(taken[0:16], axis=0, keepdims=True)]
        for i in range(len(_CAND_ROWS8)):
            cnt.append(jnp.sum(taken[16 + 8 * i:24 + 8 * i], axis=0, keepdims=True))
        t5 = taken[48:56]
        for lo in (0, 2, 4):
            cnt.append(jnp.sum(jnp.where((sub8 >= lo) & (sub8 < lo + 2), t5, 0.0), axis=0, keepdims=True))
        for i in range(8):
            cnt.append(taken[56 + i:57 + i])
        n_a = functools.reduce(lambda acc, k: jnp.where(rk[0] == float(k), cnt[k], acc), range(P_TOPK),
                               jnp.zeros((P_NKEYS, tb), f32))
        ta_ref[p, 0] = n_a
        ta_ref[p, 1] = jnp.exp(sc[0] - v0[0])
        tb_ref[p, 0] = rk[1]
        tb_ref[p, 1] = jnp.exp(sc[1] - v1[0]) / z


def _route(hb, wqt, keys, tb):
    n = hb.shape[0]
    spec = pl.BlockSpec((P_HEADS, 2, P_NKEYS, tb), lambda i: (0, 0, 0, i))
    return pl.pallas_call(
        _route_kernel,
        grid=(n // tb,),
        in_specs=[pl.BlockSpec((tb, D_MODEL), lambda i: (i, 0)), _full(wqt.shape), _full(keys.shape)],
        out_specs=[spec, spec],
        out_shape=[jax.ShapeDtypeStruct((P_HEADS, 2, P_NKEYS, n), f32),
                   jax.ShapeDtypeStruct((P_HEADS, 2, P_NKEYS, n), f32)],
        compiler_params=_cparams("parallel"),
        name="peer_route",
    )(hb, wqt, keys)


GATE_ROWS = 64


def _experts_kernel(hb_ref, h_ref, ta_ref, tb_ref, u_ref, vt_ref, g_ref, b_ref, y_ref, acc_s, ht_s, pt_s, *, te):
    j = pl.program_id(1)

    @pl.when(j == 0)
    def _():
        acc_s[...] = jnp.zeros(acc_s.shape, f32)

    ht_s[...] = lax.dot_general(u_ref[...], hb_ref[...], _NT_DIMS, preferred_element_type=f32)
    for aa in range(te // P_NKEYS):
        a = j * (te // P_NKEYS) + aa
        n_rows = [ta_ref[p, 0, pl.ds(a, 1), :] for p in range(P_HEADS)]
        e0_rows = [ta_ref[p, 1, pl.ds(a, 1), :] for p in range(P_HEADS)]
        for lt in range(ht_s.shape[1] // LANES):
            ls = slice(lt * LANES, (lt + 1) * LANES)
            for half in range(2):
                bs_ = slice(half * GATE_ROWS, (half + 1) * GATE_ROWS)
                ex = slice(aa * P_NKEYS + half * GATE_ROWS, aa * P_NKEYS + (half + 1) * GATE_ROWS)
                w = jnp.zeros((GATE_ROWS, LANES), f32)
                for p in range(P_HEADS):
                    w = w + jnp.where(tb_ref[p, 0, bs_, ls] < n_rows[p][:, ls], e0_rows[p][:, ls] * tb_ref[p, 1, bs_, ls],
                                      0.0)
                hs = ht_s[ex, ls]
                act = 0.5 * hs * (1.0 + lax.erf(hs * (0.5 ** 0.5)))
                pt_s[ex, ls] = (w * act).astype(bf16)
    acc_s[...] += jnp.dot(vt_ref[...], pt_s[...], preferred_element_type=f32)

    @pl.when(j == pl.num_programs(1) - 1)
    def _():
        z = DN_ALPHA * h_ref[...] + acc_s[...].T
        y_ref[...] = _layer_norm(z, g_ref[...], b_ref[...])


def _experts(hb, h, ta, tbl, u, vt, g, b, tb, te):
    n = hb.shape[0]
    row = pl.BlockSpec((tb, D_MODEL), lambda i, j: (i, 0))
    tab = pl.BlockSpec((P_HEADS, 2, P_NKEYS, tb), lambda i, j: (0, 0, 0, i))
    return pl.pallas_call(
        functools.partial(_experts_kernel, te=te),
        grid=(n // tb, P_EXPERTS // te),
        in_specs=[row, row, tab, tab,
                  pl.BlockSpec((te, D_MODEL), lambda i, j: (j, 0)),
                  pl.BlockSpec((D_MODEL, te), lambda i, j: (0, j)),
                  pl.BlockSpec((1, D_MODEL), lambda i, j: (0, 0)),
                  pl.BlockSpec((1, D_MODEL), lambda i, j: (0, 0))],
        out_specs=row,
        out_shape=jax.ShapeDtypeStruct((n, D_MODEL), f32),
        scratch_shapes=[pltpu.VMEM((D_MODEL, tb), f32), pltpu.VMEM((te, tb), f32), pltpu.VMEM((te, tb), bf16)],
        compiler_params=_cparams("parallel", "arbitrary"),
        name="peer_experts",
    )(hb, h, ta, tbl, u, vt, g, b)


def _to_q_tiles(q2, nb, t, tq):
    q = q2.reshape(nb, t // tq, tq, B_KV, B_REP, B_HD).transpose(0, 3, 1, 4, 2, 5)
    return q.reshape(nb, B_KV, t // tq, B_REP * tq, B_HD).astype(bf16)


def _from_q_tiles(o, nb, t, tq):
    o = o.reshape(nb, B_KV, t // tq, B_REP, tq, B_HD).transpose(0, 2, 4, 1, 3, 5)
    return o.reshape(nb, t, B_HEADS * B_HD)


def _kv_heads(kv3):
    nb, tk, _ = kv3.shape
    kv = kv3.reshape(nb, tk, 2, B_KV, B_HD).transpose(2, 0, 3, 1, 4)
    return kv[0], kv[1]


def _nsa(slopes, proj, gsmall, pe2, w2, *, nb, t, tq, row0, qpos0, cmp_src, slc3, win3, win_kpos0, tk_sel, tk_win):
    qt = _to_q_tiles(proj[row0:row0 + nb * t, C_BQ:C_CMP], nb, t, tq)
    x2, cmp_rows, cmp_steps, cmp_colblk = cmp_src
    kvc = _compress(x2, pe2, w2, rows=cmp_rows, steps=cmp_steps, row0=0, colblk=cmp_colblk)
    kc, vc = _kv_heads(kvc.reshape(nb, -1, KV_COLS))
    o_cmp, sel = _cmp_topk(slopes, qt, kc, vc, gsmall, tq=tq, qpos0=qpos0, row0=row0)
    ks, vs = _kv_heads(slc3)
    o_slc = _attn(slopes, qt, ks.astype(bf16), vs.astype(bf16), gsmall, sel, tq=tq, tk=tk_sel, qpos0=qpos0,
                  kpos0=0, window=None, gate_lane=G_GATE + B_HEADS, row0=row0)
    kw, vw = _kv_heads(win3)
    o_win = _attn(slopes, qt, kw.astype(bf16), vw.astype(bf16), gsmall, None, tq=tq, tk=tk_win, qpos0=qpos0,
                  kpos0=win_kpos0, window=WINDOW, gate_lane=G_GATE + 2 * B_HEADS, row0=row0)
    return tuple(_from_q_tiles(o, nb, t, tq).reshape(nb * t, B_HEADS * B_HD) for o in (o_cmp, o_slc, o_win))


def kernel(x_prompt, x_sample, cache_cmp_kv, cache_slc_kv, cache_win_kv, state_C, state_n, state_m, page_table,
           w_in, b_in, norm_a_g, nsa_pe, nsa_w_cmp, w_br_a, w_br_b, w_merge, w_out, ln1_g, ln1_b,
           peer_wq, peer_keys, peer_u, peer_v, ln2_g, ln2_b):
    bp, tp, _ = x_prompt.shape
    bs, ts, _ = x_sample.shape
    tsp = 8
    n_p, n_s = bp * tp, bs * tsp
    past = page_table.shape[1] * PAGE_SIZE

    perm = np.concatenate([np.arange(0, 2048), np.arange(2056, 5640), np.arange(2048, 2056), np.arange(5640, 5688)])
    w_perm = jnp.pad(w_in[:, perm], ((0, 0), (0, C_END - perm.size)))
    b_perm = jnp.pad(b_in[perm], (0, C_END - perm.size))
    w_perm_b = w_perm.astype(bf16)
    slopes = jnp.asarray(2.0 ** (-8.0 * np.arange(1, B_HEADS + 1) / B_HEADS), f32)
    wc = nsa_w_cmp.reshape(2, CMP_BLOCK, 1, B_HD, 1, B_HD)
    eye2 = jnp.eye(2, dtype=f32).reshape(1, 1, 2, 1, 2, 1)
    w2 = (wc * eye2).reshape(2, CMP_BLOCK * LANES, LANES).astype(bf16)
    pe2 = jnp.tile(nsa_pe, (1, 1, 2)).reshape(2, 1, CMP_BLOCK * LANES)

    xs_pad = jnp.pad(x_sample, ((0, 0), (0, tsp - ts), (0, 0)))
    x_all = jnp.concatenate([x_prompt.reshape(n_p, D_MODEL), xs_pad.reshape(n_s, D_MODEL)], axis=0)
    xb = x_all.astype(bf16)
    proj, kvb = _proj(xb, w_perm_b, b_perm.reshape(1, C_END), 256)
    gt = _proj_t(w_perm_b[:, C_SMALL:].T, xb, b_perm[C_SMALL:].reshape(LANES, 1), 512)

    zc = jnp.zeros((bp, A_HEADS, A_DQK, A_DV), f32)
    zn = jnp.zeros((bp, A_HEADS, 1, A_DQK), f32)
    zm = jnp.zeros((bp, A_HEADS, 1, 1), f32)
    ng = norm_a_g.reshape(1, A_HEADS * A_DV)
    ya, p_c, p_n, p_m = _mlstm(proj, gt, ng, zc, zn, zm, row0=0, nb=bp, t=tp, L=256, valid=256)
    ya, s_c, s_n, s_m = _mlstm(proj, gt, ng, state_C, state_n.reshape(bs, A_HEADS, 1, A_DQK),
                               state_m.reshape(bs, A_HEADS, 1, 1), row0=n_p, nb=bs, t=tsp, L=tsp, valid=ts, y_buf=ya)

    wt_kv = w_perm_b[:, C_CMP:C_SMALL].T
    b_kv = b_perm[C_CMP:C_SMALL].reshape(C_SMALL - C_CMP, 1)
    kvt_p = _proj_kvt(wt_kv, xb, b_kv, row0=0, nb=bp, t=tp, tn=512)
    kvt_s = _proj_kvt(wt_kv, xb, b_kv, row0=n_p, nb=1, t=n_s, tn=512)
    kvt_s = [a.reshape(KV_COLS, bs, tsp).transpose(1, 0, 2) for a in kvt_s]
    to_rows = lambda a: a.reshape(a.shape[0], 2, B_KV, B_HD, a.shape[2]).transpose(0, 4, 1, 2, 3)
    new_lanes = lambda a: jnp.pad(a, ((0, 0), (0, 0), (0, LANES - tsp)))

    kvc_p = _compress(proj, pe2, w2, rows=n_p, steps=1, row0=0, colblk=C_CMP // KV_COLS)
    oc, sel_p = _cmp_nat(slopes, proj, kvc_p.reshape(bp, tp // CMP_BLOCK, KV_COLS), nb=bp, t=tp, tq=128)
    os_ = _attn_nat(slopes, proj, kvb, sel_p, nb=bp, t=tp, tq=128, tk=1024, branch=1)
    ow = _attn_nat(slopes, proj, kvb, None, nb=bp, t=tp, tq=128, tk=512, branch=2)

    tk_s = past + PAD_PAGES * PAGE_SIZE
    wb = cache_win_kv.shape[1]
    cache_t = lambda c: c.transpose(0, 2, 3, 4, 1).reshape(c.shape[0], KV_COLS, c.shape[1])
    kvc_s = _compress_pages(page_table, cache_t(cache_cmp_kv), pe2, w2)
    kvc_s = jnp.pad(kvc_s, ((0, 0), (0, tk_s // CMP_BLOCK - kvc_s.shape[1]), (0, 0)))
    win_t = cache_t(cache_win_kv)
    s_win_t = jnp.concatenate([win_t[:, :, ts:], kvt_s[2][:, :, :ts]], axis=2)
    tk_sel = past + PAGE_SIZE
    nblk_s = tk_s // CMP_BLOCK
    expand = jnp.asarray(np.arange(tk_sel)[None, :] // CMP_BLOCK == np.arange(nblk_s)[:, None], bf16)
    oc, os_, ow = _decode(page_table, (oc, os_, ow), slopes, proj, kvc_s.reshape(bs, nblk_s, KV_COLS),
                          cache_t(cache_slc_kv), new_lanes(kvt_s[1]), win_t, new_lanes(kvt_s[2]), expand,
                          tq=tsp, row0=n_p, qpos0=past, win_kpos0=past - wb)

    h1, h1b = _tail(x_all, ya, oc, os_, ow, w_merge.astype(bf16), w_br_a.astype(bf16), w_br_b.astype(bf16),
                    w_out.astype(bf16), ln1_g.reshape(1, D_MODEL), ln1_b.reshape(1, D_MODEL), 256)
    tab_a, tab_b = _route(h1b, peer_wq.T.astype(bf16), peer_keys.astype(bf16), 256)
    y = _experts(h1b, h1, tab_a, tab_b, peer_u.astype(bf16), peer_v.T.astype(bf16), ln2_g.reshape(1, D_MODEL),
                 ln2_b.reshape(1, D_MODEL), 512, 2048)

    y_prompt = y[:n_p].reshape(bp, tp, D_MODEL)
    y_sample = y[n_p:].reshape(bs, tsp, D_MODEL)[:, :ts]
    dt = x_prompt.dtype
    return (y_prompt, y_sample, to_rows(kvt_p[0]), to_rows(kvt_p[1]), to_rows(kvt_p[2][:, :, -min(WINDOW, tp):]),
            p_c.astype(dt), p_n.astype(dt), p_m.astype(dt),
            to_rows(kvt_s[0][:, :, :ts]), to_rows(kvt_s[1][:, :, :ts]), to_rows(s_win_t),
            s_c.astype(state_C.dtype), s_n.astype(state_C.dtype), s_m.astype(state_C.dtype))
```

```python
import functools

import jax
import jax.numpy as jnp
import numpy as np
from jax import lax
from jax.experimental import pallas as pl
from jax.experimental.pallas import tpu as pltpu

D_MODEL = 1024
A_HEADS, A_DQK, A_DV = 4, 128, 256
B_HEADS, B_KV, B_HD = 16, 4, 64
B_REP = B_HEADS // B_KV
CMP_BLOCK = 64
N_SEL = 16
WINDOW = 512
PAGE_SIZE = 128
P_HEADS, P_NKEYS, P_DHALF, P_TOPK = 8, 128, 128, 16
P_EXPERTS = P_NKEYS * P_NKEYS
DN_ALPHA = 2.0 ** 0.25
LN_EPS = 1e-5
NEG = -1e30

LANES = 128
KV_COLS = 2 * B_KV * B_HD
VMEM_LIMIT = 56 * 1024 * 1024

C_AQ, C_AK, C_AV, C_AO, C_BQ, C_CMP, C_SLC, C_WIN, C_SMALL, C_END = (
    0, 512, 1024, 2048, 3072, 4096, 4608, 5120, 5632, 5760)
G_I, G_F, G_GATE = 0, A_HEADS, 2 * A_HEADS

bf16 = jnp.bfloat16
f32 = jnp.float32
_NT_DIMS = (((1,), (1,)), ((), ()))


def _cparams(*sem, flags=None):
    return pltpu.CompilerParams(dimension_semantics=sem, vmem_limit_bytes=VMEM_LIMIT, flags=flags)


def _full(shape):
    nd = len(shape)
    return pl.BlockSpec(shape, lambda *_: (0,) * nd)


def _proj_kernel(x_ref, w_ref, b_ref, o_ref, kvb_ref):
    res = jnp.dot(x_ref[...], w_ref[...], preferred_element_type=f32) + b_ref[...]
    o_ref[...] = res
    kvb_ref[...] = res[:, C_CMP:C_SMALL].astype(bf16)


def _proj(xb, w, b, tm):
    n, k = xb.shape
    e = w.shape[1]
    return pl.pallas_call(
        _proj_kernel,
        grid=(n // tm,),
        in_specs=[pl.BlockSpec((tm, k), lambda i: (i, 0)), _full((k, e)), _full((1, e))],
        out_specs=[pl.BlockSpec((tm, e), lambda i: (i, 0)), pl.BlockSpec((tm, C_SMALL - C_CMP), lambda i: (i, 0))],
        out_shape=[jax.ShapeDtypeStruct((n, e), f32), jax.ShapeDtypeStruct((n, C_SMALL - C_CMP), bf16)],
        compiler_params=_cparams("parallel"),
        name="proj",
    )(xb, w, b)


def _proj_t_kernel(wt_ref, x_ref, b_ref, o_ref):
    o_ref[...] = lax.dot_general(wt_ref[...], x_ref[...], (((1,), (1,)), ((), ())),
                                 preferred_element_type=f32) + b_ref[...]


def _proj_t(wt, xb, bcol, tn):
    e, k = wt.shape
    n = xb.shape[0]
    return pl.pallas_call(
        _proj_t_kernel,
        grid=(n // tn,),
        in_specs=[_full((e, k)), pl.BlockSpec((tn, k), lambda i: (i, 0)), _full((e, 1))],
        out_specs=pl.BlockSpec((e, tn), lambda i: (0, i)),
        out_shape=jax.ShapeDtypeStruct((e, n), f32),
        compiler_params=_cparams("parallel"),
        name="proj_t",
    )(wt, xb, bcol)


def _proj_kvt_kernel(wt_ref, x_ref, b_ref, oc_ref, os_ref, ow_ref):
    res = lax.dot_general(wt_ref[...], x_ref[...], _NT_DIMS, preferred_element_type=f32) + b_ref[...]
    for i, o_ref in enumerate((oc_ref, os_ref, ow_ref)):
        o_ref[0] = res[i * KV_COLS:(i + 1) * KV_COLS]


def _proj_kvt(wt, xb, bcol, *, row0, nb, t, tn):
    e, k = wt.shape
    nt = t // tn
    rb0 = row0 // tn
    out = pl.BlockSpec((1, KV_COLS, tn), lambda b, i: (b, 0, i))
    return pl.pallas_call(
        _proj_kvt_kernel,
        grid=(nb, nt),
        in_specs=[_full((e, k)), pl.BlockSpec((tn, k), lambda b, i: (rb0 + b * nt + i, 0)), _full((e, 1))],
        out_specs=[out, out, out],
        out_shape=[jax.ShapeDtypeStruct((nb, KV_COLS, t), f32)] * 3,
        compiler_params=_cparams("parallel", "parallel"),
        name="proj_kvt",
    )(wt, xb, bcol)


def _mlstm_kernel(q_ref, k_ref, v_ref, ao_ref, g_ref, gt_ref, ng_ref, c0_ref, n0_ref, m0_ref,
                  y_ref, c_out, n_out, m_out, c_s, n_s, m_s, *, L, valid):
    c = pl.program_id(1)

    @pl.when(c == 0)
    def _():
        c_s[...] = c0_ref[0]
        n_s[...] = n0_ref[0]
        m_s[...] = m0_ref[0]

    g = g_ref[...]
    gt = gt_ref[0]
    t_col = lax.broadcasted_iota(jnp.int32, (L, 1), 0)
    s_row = lax.broadcasted_iota(jnp.int32, (1, L), 1)
    tt = lax.broadcasted_iota(jnp.int32, (L, L), 0)
    ss = lax.broadcasted_iota(jnp.int32, (L, L), 1)
    causal = ss <= tt
    for h in range(A_HEADS):
        q = q_ref[:, h * A_DQK:(h + 1) * A_DQK]
        k = k_ref[:, h * A_DQK:(h + 1) * A_DQK] * (A_DQK ** -0.5)
        v = v_ref[:, h * A_DV:(h + 1) * A_DV]
        i_col, f_col = g[:, G_I + h:G_I + h + 1], g[:, G_F + h:G_F + h + 1]
        i_row, f_row = gt[G_I + h:G_I + h + 1, :], gt[G_F + h:G_F + h + 1, :]
        lf_col = jax.nn.log_sigmoid(f_col)
        lf_row = jax.nn.log_sigmoid(f_row)
        if valid < L:
            lf_col = jnp.where(t_col < valid, lf_col, 0.0)
            lf_row = jnp.where(s_row < valid, lf_row, 0.0)
            i_col = jnp.where(t_col < valid, i_col, NEG)
            i_row = jnp.where(s_row < valid, i_row, NEG)
        b_col = jnp.sum(jnp.where(causal, lf_row, 0.0), axis=1, keepdims=True)
        b_row = jnp.sum(jnp.where(tt <= ss, lf_col, 0.0), axis=0, keepdims=True)
        m_prev = m_s[h]
        cmat = c_s[h]
        n_row = n_s[h]

        d_log = jnp.where(causal, b_col - b_row + i_row, NEG)
        inter = b_col + m_prev
        m_t = jnp.maximum(inter, jnp.max(d_log, axis=1, keepdims=True))
        qb = q.astype(bf16)
        qk = lax.dot_general(qb, k.astype(bf16), _NT_DIMS, preferred_element_type=f32)
        smat = qk * jnp.exp(d_log - m_t)
        w_inter = jnp.exp(inter - m_t)
        vb = v.astype(bf16)
        num = (w_inter * jnp.dot(qb, cmat.astype(bf16), preferred_element_type=f32)
               + jnp.dot(smat.astype(bf16), vb, preferred_element_type=f32))
        den = w_inter * jnp.sum(q * n_row, axis=1, keepdims=True) + jnp.sum(smat, axis=1, keepdims=True)
        hid = num / jnp.maximum(jnp.abs(den), jnp.exp(-m_t))
        mu = jnp.mean(hid, axis=1, keepdims=True)
        var = jnp.mean(jnp.square(hid - mu), axis=1, keepdims=True)
        hid = (hid - mu) * lax.rsqrt(var + LN_EPS) * ng_ref[:, h * A_DV:(h + 1) * A_DV]
        y_ref[:, h * A_DV:(h + 1) * A_DV] = hid * jax.nn.sigmoid(ao_ref[:, h * A_DV:(h + 1) * A_DV])

        b_end = b_col[L - 1:L, :]
        g_row = b_end - b_row + i_row
        m_new = jnp.maximum(b_end + m_prev, jnp.max(g_row, axis=1, keepdims=True))
        a = jnp.exp(b_end + m_prev - m_new)
        w_col = jnp.exp(b_end - b_col + i_col - m_new)
        kw = k * w_col
        c_s[h] = a * cmat + lax.dot_general(kw.astype(bf16), vb, (((0,), (0,)), ((), ())),
                                            preferred_element_type=f32)
        n_s[h] = a * n_row + jnp.sum(kw, axis=0, keepdims=True)
        m_s[h] = m_new

    @pl.when(c == pl.num_programs(1) - 1)
    def _():
        c_out[0] = c_s[...]
        n_out[0] = n_s[...]
        m_out[0] = m_s[...]


def _into(buf, kern):
    if buf is None:
        return kern, [], [], {}
    return (lambda buf_ref, *refs: kern(*refs)), [pl.BlockSpec(memory_space=pl.ANY)], [buf], {0: 0}


def _mlstm(proj, gt, norm_g, c0, n0, m0, *, row0, nb, t, L, valid, y_buf=None):
    nc = t // L
    rb0 = row0 // L
    gt = gt[:8, row0:row0 + nb * t].reshape(8, nb * nc, L).transpose(1, 0, 2)
    rows = lambda b, c: rb0 + b * nc + c
    st = lambda b, c: (b, 0, 0, 0)
    qk_w, v_w = A_HEADS * A_DQK, A_HEADS * A_DV
    kern, alias_specs, alias_args, aliases = _into(y_buf, functools.partial(_mlstm_kernel, L=L, valid=valid))
    y, c_f, n_f, m_f = pl.pallas_call(
        kern,
        grid=(nb, nc),
        input_output_aliases=aliases,
        in_specs=alias_specs + [
            pl.BlockSpec((L, qk_w), lambda b, c: (rows(b, c), C_AQ // qk_w)),
            pl.BlockSpec((L, qk_w), lambda b, c: (rows(b, c), C_AK // qk_w)),
            pl.BlockSpec((L, v_w), lambda b, c: (rows(b, c), C_AV // v_w)),
            pl.BlockSpec((L, v_w), lambda b, c: (rows(b, c), C_AO // v_w)),
            pl.BlockSpec((L, LANES), lambda b, c: (rows(b, c), C_SMALL // LANES)),
            pl.BlockSpec((1, 8, L), lambda b, c: (b * nc + c, 0, 0)),
            pl.BlockSpec((1, v_w), lambda b, c: (0, 0)),
            pl.BlockSpec((1, A_HEADS, A_DQK, A_DV), st),
            pl.BlockSpec((1, A_HEADS, 1, A_DQK), st),
            pl.BlockSpec((1, A_HEADS, 1, 1), st),
        ],
        out_specs=[
            pl.BlockSpec((L, v_w), lambda b, c: (rows(b, c), 0)),
            pl.BlockSpec((1, A_HEADS, A_DQK, A_DV), st),
            pl.BlockSpec((1, A_HEADS, 1, A_DQK), st),
            pl.BlockSpec((1, A_HEADS, 1, 1), st),
        ],
        out_shape=[
            jax.ShapeDtypeStruct((proj.shape[0], A_HEADS * A_DV), f32),
            jax.ShapeDtypeStruct((nb, A_HEADS, A_DQK, A_DV), f32),
            jax.ShapeDtypeStruct((nb, A_HEADS, 1, A_DQK), f32),
            jax.ShapeDtypeStruct((nb, A_HEADS, 1, 1), f32),
        ],
        scratch_shapes=[pltpu.VMEM((A_HEADS, A_DQK, A_DV), f32), pltpu.VMEM((A_HEADS, 1, A_DQK), f32),
                        pltpu.VMEM((A_HEADS, 1, 1), f32)],
        compiler_params=_cparams("parallel", "arbitrary"),
        name="mlstm",
    )(*alias_args, proj, proj, proj, proj, proj, gt, norm_g, c0, n0, m0)
    return y, c_f, n_f[:, :, 0], m_f[:, :, 0, 0]


def _compress_kernel(x_ref, pe_ref, w_ref, o_ref, xf_ref, *, nblk):
    for l in range(CMP_BLOCK):
        xf_ref[:, l * LANES:(l + 1) * LANES] = x_ref[pl.ds(l, nblk, stride=CMP_BLOCK), :]
    xf = (xf_ref[...] + pe_ref[0]).astype(bf16)
    o_ref[...] = jnp.dot(xf, w_ref[0], preferred_element_type=f32)


def _compress(x2, pe2, w2, *, rows, steps, row0, colblk):
    nblk = rows // CMP_BLOCK
    kflat = CMP_BLOCK * LANES
    rb0 = row0 // rows
    ngrp = KV_COLS // LANES
    return pl.pallas_call(
        functools.partial(_compress_kernel, nblk=nblk),
        grid=(steps, ngrp),
        in_specs=[pl.BlockSpec((rows, LANES), lambda s, p: (rb0 + s, colblk * ngrp + p)),
                  pl.BlockSpec((1, 1, kflat), lambda s, p: (p // 2, 0, 0)),
                  pl.BlockSpec((1, kflat, LANES), lambda s, p: (p // 2, 0, 0))],
        out_specs=pl.BlockSpec((nblk, LANES), lambda s, p: (s, p)),
        out_shape=jax.ShapeDtypeStruct((steps * nblk, KV_COLS), f32),
        scratch_shapes=[pltpu.VMEM((nblk, kflat), f32)],
        compiler_params=_cparams("parallel", "parallel"),
        name="compress",
    )(x2, pe2, w2)


SEQS_PER_STEP = 8


def _compress_pages_kernel(pt_ref, *refs):
    del pt_ref
    pages = refs[:SEQS_PER_STEP]
    pe_ref, w_ref, o_ref = refs[SEQS_PER_STEP:SEQS_PER_STEP + 3]
    tm = refs[SEQS_PER_STEP + 3:SEQS_PER_STEP + 3 + KV_COLS // LANES]
    xf_s = refs[-1]
    j = pl.program_id(1)
    blocks = SEQS_PER_STEP * PAGE_SIZE // CMP_BLOCK
    for u, page in enumerate(pages):
        rows = page[0].T
        for p, tm_p in enumerate(tm):
            tm_p[u * PAGE_SIZE:(u + 1) * PAGE_SIZE, :] = rows[:, p * LANES:(p + 1) * LANES]
    dst = pl.ds(pl.multiple_of(j * blocks, blocks), blocks)
    for p, tm_p in enumerate(tm):
        for l in range(CMP_BLOCK):
            piece = tm_p[pl.ds(l, blocks, stride=CMP_BLOCK), :] + pe_ref[p // 2, :, l * LANES:(l + 1) * LANES]
            xf_s[p, dst, l * LANES:(l + 1) * LANES] = piece.astype(bf16)

    @pl.when(j == pl.num_programs(1) - 1)
    def _():
        for p in range(len(tm)):
            o_ref[0, :, p * LANES:(p + 1) * LANES] = jnp.dot(xf_s[p], w_ref[p // 2], preferred_element_type=f32)


def _compress_pages(page_table, cache_t, pe2, w2):
    nb, n_pages = page_table.shape
    groups = nb // SEQS_PER_STEP
    kflat = CMP_BLOCK * LANES
    rows = SEQS_PER_STEP * n_pages * PAGE_SIZE // CMP_BLOCK
    ngrp = KV_COLS // LANES
    out = pl.pallas_call(
        _compress_pages_kernel,
        grid_spec=pltpu.PrefetchScalarGridSpec(
            num_scalar_prefetch=1,
            grid=(groups, n_pages),
            in_specs=[pl.BlockSpec((1, KV_COLS, PAGE_SIZE),
                                   functools.partial(lambda s, j, pt, u: (pt[s * SEQS_PER_STEP + u, j], 0, 0), u=u))
                      for u in range(SEQS_PER_STEP)]
            + [pl.BlockSpec((2, 1, kflat), lambda s, j, pt: (0, 0, 0)),
               pl.BlockSpec((2, kflat, LANES), lambda s, j, pt: (0, 0, 0))],
            out_specs=pl.BlockSpec((1, rows, KV_COLS), lambda s, j, pt: (s, 0, 0)),
            scratch_shapes=[pltpu.VMEM((SEQS_PER_STEP * PAGE_SIZE, LANES), f32) for _ in range(ngrp)]
            + [pltpu.VMEM((ngrp, rows, kflat), bf16)],
        ),
        out_shape=jax.ShapeDtypeStruct((groups, rows, KV_COLS), f32),
        compiler_params=_cparams("parallel", "arbitrary"),
        name="compress_pages",
    )(page_table, *([cache_t] * SEQS_PER_STEP), pe2, w2)
    out = out.reshape(groups, n_pages, SEQS_PER_STEP, PAGE_SIZE // CMP_BLOCK, KV_COLS).transpose(0, 2, 1, 3, 4)
    return out.reshape(nb, n_pages * PAGE_SIZE // CMP_BLOCK, KV_COLS)


def _gate_col(g, lane_idx):
    lane = lax.broadcasted_iota(jnp.int32, g.shape, 1)
    return jax.nn.sigmoid(jnp.sum(jnp.where(lane == lane_idx, g, 0.0), axis=1, keepdims=True))


def _cmp_topk_kernel(slope_ref, q_ref, kc_ref, vc_ref, g_ref, o_ref, sel_ref, *, tq, nblk, qpos0):
    gi = pl.program_id(1)
    i = pl.program_id(2)
    qpos = qpos0 + i * tq + lax.broadcasted_iota(jnp.int32, (tq, 1), 0)
    j = lax.broadcasted_iota(jnp.int32, (1, nblk), 1)
    dist = qpos - ((j + 1) * CMP_BLOCK - 1)
    valid = dist >= 0
    distf = dist.astype(f32)
    kc = kc_ref[0, 0].astype(bf16)
    vc = vc_ref[0, 0].astype(bf16)
    g = g_ref[...]
    imp = jnp.zeros((tq, nblk), f32)
    for r in range(B_REP):
        qr = q_ref[0, 0, 0, r * tq:(r + 1) * tq, :]
        s = lax.dot_general(qr, kc, (((1,), (1,)), ((), ())), preferred_element_type=f32) * (B_HD ** -0.5)
        s = s - slope_ref[gi * B_REP + r] * distf
        s = jnp.where(valid, s, NEG)
        e = jnp.exp(s - jnp.max(s, axis=1, keepdims=True))
        p = jnp.where(valid, e / jnp.sum(e, axis=1, keepdims=True), 0.0)
        imp = imp + p
        o = jnp.dot(p.astype(bf16), vc, preferred_element_type=f32)
        o_ref[0, 0, 0, r * tq:(r + 1) * tq, :] = o * _gate_col(g, G_GATE + gi * B_REP + r)
    cur = qpos // CMP_BLOCK
    imp = jnp.where((j == cur) | (j == 0), float(B_REP + 1), imp)
    imp = jnp.where(j > cur, -1.0, imp)
    jf = j.astype(f32)
    sel = jnp.zeros((tq, nblk), f32)
    for _ in range(N_SEL):
        mx = jnp.max(imp, axis=1, keepdims=True)
        idx = jnp.min(jnp.where(imp == mx, jf, float(nblk)), axis=1, keepdims=True)
        hit = jf == idx
        sel = jnp.where(hit, 1.0, sel)
        imp = jnp.where(hit, NEG, imp)
    sel_ref[0, 0] = sel


def _cmp_topk(slopes, qt, kc, vc, gsmall, *, tq, qpos0, row0):
    nb, _, nqt, _, _ = qt.shape
    nblk = kc.shape[2]
    rb0 = row0 // tq
    return pl.pallas_call(
        functools.partial(_cmp_topk_kernel, tq=tq, nblk=nblk, qpos0=qpos0),
        grid=(nb, B_KV, nqt),
        in_specs=[
            pl.BlockSpec(memory_space=pltpu.SMEM),
            pl.BlockSpec((1, 1, 1, B_REP * tq, B_HD), lambda b, g, i: (b, g, i, 0, 0)),
            pl.BlockSpec((1, 1, nblk, B_HD), lambda b, g, i: (b, g, 0, 0)),
            pl.BlockSpec((1, 1, nblk, B_HD), lambda b, g, i: (b, g, 0, 0)),
            pl.BlockSpec((tq, LANES), lambda b, g, i: (rb0 + b * nqt + i, 0)),
        ],
        out_specs=[
            pl.BlockSpec((1, 1, 1, B_REP * tq, B_HD), lambda b, g, i: (b, g, i, 0, 0)),
            pl.BlockSpec((1, 1, tq, nblk), lambda b, g, i: (b, g, i, 0)),
        ],
        out_shape=[
            jax.ShapeDtypeStruct(qt.shape, f32),
            jax.ShapeDtypeStruct((nb, B_KV, nqt * tq, nblk), f32),
        ],
        compiler_params=_cparams("parallel", "parallel", "parallel"),
        name="cmp_topk",
    )(slopes, qt, kc, vc, gsmall)


def _attn_kernel(slope_ref, q_ref, k_ref, v_ref, g_ref, *rest, tq, tk, nkt, nblk, qpos0, kpos0, window,
                 gate_lane):
    if window is None:
        sel_ref, o_ref, m_s, l_s, acc_s = rest
    else:
        o_ref, m_s, l_s, acc_s = rest
    gi = pl.program_id(1)
    i = pl.program_id(2)
    qlo = qpos0 + i * tq
    qpos = qlo + lax.broadcasted_iota(jnp.int32, (tq, 1), 0)
    m_s[...] = jnp.full(m_s.shape, NEG, f32)
    l_s[...] = jnp.zeros(l_s.shape, f32)
    acc_s[...] = jnp.zeros(acc_s.shape, f32)
    kt_hi = jnp.minimum((qlo + tq - 1 - kpos0) // tk + 1, nkt)
    if window is None:
        kt_lo = 0
        selb = sel_ref[0, 0].astype(bf16)
    else:
        kt_lo = jnp.maximum(qlo - (window - 1) - kpos0, 0) // tk

    def body(kt, carry):
        k0 = pl.multiple_of(kt * tk, tk)
        kb = k_ref[0, 0, pl.ds(k0, tk), :]
        vb = v_ref[0, 0, pl.ds(k0, tk), :]
        kidx = k0 + lax.broadcasted_iota(jnp.int32, (1, tk), 1)
        dist = qpos - (kpos0 + kidx)
        mask = dist >= 0
        if window is None:
            blk = lax.broadcasted_iota(jnp.int32, (nblk, tk), 0)
            kblk = (k0 + lax.broadcasted_iota(jnp.int32, (nblk, tk), 1)) // CMP_BLOCK
            expand = jnp.where(blk == kblk, 1.0, 0.0).astype(bf16)
            mask = mask & (jnp.dot(selb, expand, preferred_element_type=f32) > 0.5)
        else:
            mask = mask & (dist < window)
        distf = dist.astype(f32)
        for r in range(B_REP):
            rows = slice(r * tq, (r + 1) * tq)
            s = lax.dot_general(q_ref[0, 0, 0, rows, :], kb, (((1,), (1,)), ((), ())),
                                preferred_element_type=f32) * (B_HD ** -0.5)
            s = jnp.where(mask, s - slope_ref[gi * B_REP + r] * distf, NEG)
            m_old = m_s[rows, :]
            m_new = jnp.maximum(m_old, jnp.max(s, axis=1, keepdims=True))
            alpha = jnp.exp(m_old - m_new)
            p = jnp.exp(s - m_new)
            l_s[rows, :] = alpha * l_s[rows, :] + jnp.sum(p, axis=1, keepdims=True)
            acc_s[rows, :] = alpha * acc_s[rows, :] + jnp.dot(p.astype(bf16), vb, preferred_element_type=f32)
            m_s[rows, :] = m_new
        return carry

    lax.fori_loop(kt_lo, kt_hi, body, 0)
    g = g_ref[...]
    for r in range(B_REP):
        rows = slice(r * tq, (r + 1) * tq)
        o_ref[0, 0, 0, rows, :] = acc_s[rows, :] / l_s[rows, :] * _gate_col(g, gate_lane + gi * B_REP + r)


def _attn(slopes, qt, kh, vh, gsmall, sel, *, tq, tk, qpos0, kpos0, window, gate_lane, row0):
    nb, _, nqt, _, _ = qt.shape
    tkk = kh.shape[2]
    nkt = tkk // tk
    nblk = None if sel is None else sel.shape[3]
    rb0 = row0 // tq
    in_specs = [
        pl.BlockSpec(memory_space=pltpu.SMEM),
        pl.BlockSpec((1, 1, 1, B_REP * tq, B_HD), lambda b, g, i: (b, g, i, 0, 0)),
        pl.BlockSpec((1, 1, tkk, B_HD), lambda b, g, i: (b, g, 0, 0)),
        pl.BlockSpec((1, 1, tkk, B_HD), lambda b, g, i: (b, g, 0, 0)),
        pl.BlockSpec((tq, LANES), lambda b, g, i: (rb0 + b * nqt + i, 0)),
    ]
    args = [slopes, qt, kh, vh, gsmall]
    if sel is not None:
        in_specs.append(pl.BlockSpec((1, 1, tq, nblk), lambda b, g, i: (b, g, i, 0)))
        args.append(sel)
    return pl.pallas_call(
        functools.partial(_attn_kernel, tq=tq, tk=tk, nkt=nkt, nblk=nblk, qpos0=qpos0, kpos0=kpos0,
                          window=window, gate_lane=gate_lane),
        grid=(nb, B_KV, nqt),
        in_specs=in_specs,
        out_specs=pl.BlockSpec((1, 1, 1, B_REP * tq, B_HD), lambda b, g, i: (b, g, i, 0, 0)),
        out_shape=jax.ShapeDtypeStruct(qt.shape, f32),
        scratch_shapes=[pltpu.VMEM((B_REP * tq, 1), f32), pltpu.VMEM((B_REP * tq, 1), f32),
                        pltpu.VMEM((B_REP * tq, B_HD), f32)],
        compiler_params=_cparams("parallel", "parallel", "parallel"),
        name="attn_sel" if window is None else "attn_win",
    )(*args)


def _roll_lanes(x, shift):
    return x if shift == 0 else pltpu.roll(x, shift, axis=1)


def _softmax_rows(s, mask):
    s = jnp.where(mask, s, NEG)
    e = jnp.exp(s - jnp.max(s, axis=1, keepdims=True))
    return jnp.where(mask, e, 0.0), jnp.sum(e, axis=1, keepdims=True)


def _decode_kernel(slope_ref, q_ref, g_ref, kvc_ref, pages, slc_new_ref, win_ref, win_new_ref, exp_ref,
                   oc_ref, os_ref, ow_ref, kt_s, vt_s, wkt_s, wvt_s, *, tq, nblk, qpos0, win_kpos0):
    nrow = B_HEADS * tq
    q = q_ref[...]
    gs = g_ref[...]
    lane_grp = lax.broadcasted_iota(jnp.int32, (tq, B_KV * B_HD), 1) // B_HD
    qm = []
    for g in range(B_KV):
        qg = q[:, g * B_KV * B_HD:(g + 1) * B_KV * B_HD]
        for r in range(B_REP):
            qm.append(jnp.where(lane_grp == g, _roll_lanes(qg, ((g - r) % B_REP) * B_HD), 0.0))
    qm = jnp.concatenate(qm, axis=0).astype(bf16)
    row = lax.broadcasted_iota(jnp.int32, (nrow, 1), 0)
    qpos = qpos0 + row % tq
    slope = functools.reduce(lambda acc, h: jnp.where(row // tq == h, slope_ref[h], acc), range(B_HEADS),
                             jnp.zeros((nrow, 1), f32))
    scale = B_HD ** -0.5
    nt = (((1,), (1,)), ((), ()))

    def emit(o_ref, o, branch):
        for g in range(B_KV):
            acc = jnp.zeros((tq, B_KV * B_HD), f32)
            for r in range(B_REP):
                h = g * B_REP + r
                gate = jax.nn.sigmoid(gs[:, G_GATE + branch * B_HEADS + h:G_GATE + branch * B_HEADS + h + 1])
                oh = jnp.where(lane_grp == g, o[h * tq:(h + 1) * tq, :] * gate, 0.0)
                acc = acc + _roll_lanes(oh, ((r - g) % B_REP) * B_HD)
            o_ref[:, g * B_KV * B_HD:(g + 1) * B_KV * B_HD] = acc

    kvc = kvc_ref[0]
    j = lax.broadcasted_iota(jnp.int32, (1, nblk), 1)
    dist = qpos - ((j + 1) * CMP_BLOCK - 1)
    s = lax.dot_general(qm, kvc[:, :B_KV * B_HD].astype(bf16), nt, preferred_element_type=f32) * scale
    e, l = _softmax_rows(s - slope * dist.astype(f32), dist >= 0)
    p = e / l
    emit(oc_ref, jnp.dot(p.astype(bf16), kvc[:, B_KV * B_HD:].astype(bf16), preferred_element_type=f32), 0)
    imp = jnp.concatenate(
        [functools.reduce(lambda a, b: a + b, [p[(g * B_REP + r) * tq:(g * B_REP + r + 1) * tq] for r in range(B_REP)])
         for g in range(B_KV)], axis=0)
    cur = (qpos0 + lax.broadcasted_iota(jnp.int32, (B_KV * tq, 1), 0) % tq) // CMP_BLOCK
    imp = jnp.where((j == cur) | (j == 0), float(B_REP + 1), imp)
    imp = jnp.where(j > cur, -1.0, imp)
    rank = jnp.zeros(imp.shape, f32)
    for other in range(nblk):
        col = imp[:, other:other + 1]
        rank = rank + jnp.where((col > imp) | ((col == imp) & (j > other)), 1.0, 0.0)
    sel = jnp.where(rank < float(N_SEL), 1.0, 0.0)
    sel_rows = jnp.concatenate([sel[g * tq:(g + 1) * tq] for g in range(B_KV) for _ in range(B_REP)], axis=0)

    def transposed_kv(parts, kt_s, vt_s):
        for u, part in enumerate(parts):
            w = part.shape[-1]
            kt_s[:, u * w:(u + 1) * w] = part[0, :GRP_LANES, :].astype(bf16)
            vt_s[:, u * w:(u + 1) * w] = part[0, GRP_LANES:, :].astype(bf16)

    def attend(kt_s, vt_s, mask):
        s = jnp.dot(qm, kt_s[...], preferred_element_type=f32) * scale
        e, l = _softmax_rows(s - slope * dist.astype(f32), mask)
        return lax.dot_general(e.astype(bf16), vt_s[...], nt, preferred_element_type=f32) / l

    transposed_kv(list(pages) + [slc_new_ref], kt_s, vt_s)
    dist = qpos - lax.broadcasted_iota(jnp.int32, (1, kt_s.shape[1]), 1)
    picked = jnp.dot(sel_rows.astype(bf16), exp_ref[...], preferred_element_type=f32) > 0.5
    emit(os_ref, attend(kt_s, vt_s, picked & (dist >= 0)), 1)

    wb = win_ref.shape[-1]
    wkt_s[:, :wb] = win_ref[0, :GRP_LANES, :].astype(bf16)
    wvt_s[:, :wb] = win_ref[0, GRP_LANES:, :].astype(bf16)
    wkt_s[:, wb:] = win_new_ref[0, :GRP_LANES, :].astype(bf16)
    wvt_s[:, wb:] = win_new_ref[0, GRP_LANES:, :].astype(bf16)
    dist = qpos - (win_kpos0 + lax.broadcasted_iota(jnp.int32, (1, wkt_s.shape[1]), 1))
    emit(ow_ref, attend(wkt_s, wvt_s, (dist >= 0) & (dist < WINDOW)), 2)


def _decode(page_table, bufs, slopes, proj, kvc, slc_cache_t, slc_new_t, win_t, win_new_t, expand, *,
            tq, row0, qpos0, win_kpos0):
    nb, n_pages = page_table.shape
    nblk = kvc.shape[1]
    rb0 = row0 // tq
    tk = (n_pages + 1) * PAGE_SIZE
    twin = win_t.shape[-1] + win_new_t.shape[-1]
    out = pl.BlockSpec((tq, B_HEADS * B_HD), lambda b, pt: (rb0 + b, 0))
    page = (1, KV_COLS, PAGE_SIZE)

    def body(pt_ref, oc_buf, os_buf, ow_buf, slope_ref, q_ref, g_ref, kvc_ref, *refs):
        _decode_kernel(slope_ref, q_ref, g_ref, kvc_ref, refs[:n_pages], *refs[n_pages:], tq=tq, nblk=nblk,
                       qpos0=qpos0, win_kpos0=win_kpos0)

    return pl.pallas_call(
        body,
        grid_spec=pltpu.PrefetchScalarGridSpec(
            num_scalar_prefetch=1,
            grid=(nb,),
            in_specs=[pl.BlockSpec(memory_space=pl.ANY)] * 3
            + [pl.BlockSpec(memory_space=pltpu.SMEM),
               pl.BlockSpec((tq, B_HEADS * B_HD), lambda b, pt: (rb0 + b, C_BQ // (B_HEADS * B_HD))),
               pl.BlockSpec((tq, LANES), lambda b, pt: (rb0 + b, C_SMALL // LANES)),
               pl.BlockSpec((1,) + kvc.shape[1:], lambda b, pt: (b, 0, 0))]
            + [pl.BlockSpec(page, functools.partial(lambda b, pt, u: (pt[b, u], 0, 0), u=u)) for u in range(n_pages)]
            + [pl.BlockSpec((1,) + slc_new_t.shape[1:], lambda b, pt: (b, 0, 0)),
               pl.BlockSpec((1,) + win_t.shape[1:], lambda b, pt: (b, 0, 0)),
               pl.BlockSpec((1,) + win_new_t.shape[1:], lambda b, pt: (b, 0, 0)),
               pl.BlockSpec(expand.shape, lambda b, pt: (0, 0))],
            out_specs=[out, out, out],
            scratch_shapes=[pltpu.VMEM((GRP_LANES, tk), bf16), pltpu.VMEM((GRP_LANES, tk), bf16),
                            pltpu.VMEM((GRP_LANES, twin), bf16), pltpu.VMEM((GRP_LANES, twin), bf16)],
        ),
        out_shape=[jax.ShapeDtypeStruct(b.shape, f32) for b in bufs],
        input_output_aliases={1: 0, 2: 1, 3: 2},
        compiler_params=_cparams("parallel"),
        name="nsa_decode",
    )(page_table, *bufs, slopes, proj, proj, kvc, *([slc_cache_t] * n_pages), slc_new_t, win_t, win_new_t, expand)


GRP_LANES = B_KV * B_HD
_NT = (((1,), (1,)), ((), ()))


def _masked_queries(q, tq):
    lane_grp = lax.broadcasted_iota(jnp.int32, (tq, GRP_LANES), 1) // B_HD
    rows = []
    for g in range(B_KV):
        qg = q[:, g * GRP_LANES:(g + 1) * GRP_LANES]
        for r in range(B_REP):
            rows.append(jnp.where(lane_grp == g, _roll_lanes(qg, ((g - r) % B_REP) * B_HD), 0.0))
    return jnp.concatenate(rows, axis=0).astype(bf16)


def _group_columns(slope_ref, g, tq, qlo):
    row = lax.broadcasted_iota(jnp.int32, (B_REP * tq, 1), 0)
    slope = functools.reduce(lambda acc, r: jnp.where(row // tq == r, slope_ref[g * B_REP + r], acc), range(B_REP),
                             jnp.zeros((B_REP * tq, 1), f32))
    return qlo + row % tq, slope


def _emit_group(o_ref, og, gs, branch, g, tq):
    lane_grp = lax.broadcasted_iota(jnp.int32, (tq, GRP_LANES), 1) // B_HD
    acc = jnp.zeros((tq, GRP_LANES), f32)
    for r in range(B_REP):
        c = G_GATE + branch * B_HEADS + g * B_REP + r
        oh = jnp.where(lane_grp == g, og[r * tq:(r + 1) * tq, :] * jax.nn.sigmoid(gs[:, c:c + 1]), 0.0)
        acc = acc + _roll_lanes(oh, ((r - g) % B_REP) * B_HD)
    o_ref[:, g * GRP_LANES:(g + 1) * GRP_LANES] = acc


def _cmp_nat_kernel(slope_ref, q_ref, g_ref, kvc_ref, o_ref, sel_ref, *, tq, nblk):
    qlo = pl.program_id(1) * tq
    qm = _masked_queries(q_ref[...], tq)
    gs = g_ref[...]
    kvc = kvc_ref[0]
    kc = kvc[:, :GRP_LANES].astype(bf16)
    vc = kvc[:, GRP_LANES:].astype(bf16)
    j = lax.broadcasted_iota(jnp.int32, (1, nblk), 1)
    jf = j.astype(f32)
    imps = []
    for g in range(B_KV):
        qpos, slope = _group_columns(slope_ref, g, tq, qlo)
        dist = qpos - ((j + 1) * CMP_BLOCK - 1)
        s = lax.dot_general(qm[g * B_REP * tq:(g + 1) * B_REP * tq], kc, _NT, preferred_element_type=f32) * (B_HD ** -0.5)
        e, l = _softmax_rows(s - slope * dist.astype(f32), dist >= 0)
        p = e / l
        _emit_group(o_ref, jnp.dot(p.astype(bf16), vc, preferred_element_type=f32), gs, 0, g, tq)
        imps.append(functools.reduce(lambda a, b: a + b, [p[r * tq:(r + 1) * tq] for r in range(B_REP)]))
    imp = jnp.concatenate(imps, axis=0)
    cur = (qlo + lax.broadcasted_iota(jnp.int32, (B_KV * tq, 1), 0) % tq) // CMP_BLOCK
    imp = jnp.where((j == cur) | (j == 0), float(B_REP + 1), imp)
    imp = jnp.where(j > cur, -1.0, imp)
    sel = jnp.zeros(imp.shape, f32)
    for _ in range(N_SEL):
        mx = jnp.max(imp, axis=1, keepdims=True)
        idx = jnp.min(jnp.where(imp == mx, jf, float(nblk)), axis=1, keepdims=True)
        hit = jf == idx
        sel = jnp.where(hit, 1.0, sel)
        imp = jnp.where(hit, NEG, imp)
    for g in range(B_KV):
        sel_ref[0, g] = sel[g * tq:(g + 1) * tq]


def _cmp_nat(slopes, proj, kvc, *, nb, t, tq):
    nqt = t // tq
    nblk = kvc.shape[1]
    return pl.pallas_call(
        functools.partial(_cmp_nat_kernel, tq=tq, nblk=nblk),
        grid=(nb, nqt),
        in_specs=[pl.BlockSpec(memory_space=pltpu.SMEM),
                  pl.BlockSpec((tq, B_HEADS * B_HD), lambda b, i: (b * nqt + i, C_BQ // (B_HEADS * B_HD))),
                  pl.BlockSpec((tq, LANES), lambda b, i: (b * nqt + i, C_SMALL // LANES)),
                  pl.BlockSpec((1, nblk, KV_COLS), lambda b, i: (b, 0, 0))],
        out_specs=[pl.BlockSpec((tq, B_HEADS * B_HD), lambda b, i: (b * nqt + i, 0)),
                   pl.BlockSpec((1, B_KV, tq, nblk), lambda b, i: (b, 0, i, 0))],
        out_shape=[jax.ShapeDtypeStruct((proj.shape[0], B_HEADS * B_HD), f32),
                   jax.ShapeDtypeStruct((nb, B_KV, t, nblk), f32)],
        compiler_params=_cparams("parallel", "parallel"),
        name="nsa_cmp",
    )(slopes, proj, proj, kvc)


def _sel_nat_kernel(slope_ref, q_ref, g_ref, kv_ref, sel_ref, o_ref, qm_s, m_s, l_s, acc_s, *, tq, tk, nblk):
    qlo = pl.program_id(1) * tq
    qm_s[...] = _masked_queries(q_ref[...] * (B_HD ** -0.5), tq)
    m_s[...] = jnp.full(m_s.shape, NEG, f32)
    l_s[...] = jnp.zeros(l_s.shape, f32)
    acc_s[...] = jnp.zeros(acc_s.shape, f32)
    selb = jnp.concatenate([sel_ref[0, g] for g in range(B_KV)], axis=0).astype(bf16)
    qpos = qlo + lax.broadcasted_iota(jnp.int32, (tq, 1), 0)
    grows = B_REP * tq

    def body(kt, carry):
        k0 = pl.multiple_of(kt * tk, tk)
        kb = kv_ref[pl.ds(k0, tk), :GRP_LANES]
        vb = kv_ref[pl.ds(k0, tk), GRP_LANES:]
        dist = qpos - (k0 + lax.broadcasted_iota(jnp.int32, (1, tk), 1))
        distf = dist.astype(f32)
        blk = lax.broadcasted_iota(jnp.int32, (nblk, tk), 0)
        kblk = (k0 + lax.broadcasted_iota(jnp.int32, (nblk, tk), 1)) // CMP_BLOCK
        picked = jnp.dot(selb, jnp.where(blk == kblk, 1.0, 0.0).astype(bf16), preferred_element_type=f32)
        for g in range(B_KV):
            sg = lax.dot_general(qm_s[g * grows:(g + 1) * grows, :], kb, _NT, preferred_element_type=f32)
            amask = jnp.where((picked[g * tq:(g + 1) * tq] > 0.5) & (dist >= 0), 0.0, NEG)
            ps, alphas = [], []
            for r in range(B_REP):
                rows = slice((g * B_REP + r) * tq, (g * B_REP + r + 1) * tq)
                s = sg[r * tq:(r + 1) * tq] - slope_ref[g * B_REP + r] * distf + amask
                m_old = m_s[rows, :]
                m_new = jnp.maximum(m_old, jnp.max(s, axis=1, keepdims=True))
                alpha = jnp.exp(m_old - m_new)
                p = jnp.exp(s - m_new)
                l_s[rows, :] = alpha * l_s[rows, :] + jnp.sum(p, axis=1, keepdims=True)
                m_s[rows, :] = m_new
                ps.append(p.astype(bf16))
                alphas.append(alpha)
            grp = slice(g * grows, (g + 1) * grows)
            acc_s[grp, :] = (jnp.concatenate(alphas, axis=0) * acc_s[grp, :]
                             + jnp.dot(jnp.concatenate(ps, axis=0), vb, preferred_element_type=f32))
        return carry

    lax.fori_loop(0, (qlo + tq - 1) // tk + 1, body, 0)
    gs = g_ref[...]
    for g in range(B_KV):
        rows = slice(g * grows, (g + 1) * grows)
        _emit_group(o_ref, acc_s[rows, :] / l_s[rows, :], gs, 1, g, tq)


def _win_nat_kernel(slope_ref, q_ref, g_ref, kv_ref, o_ref, *, tq, t):
    qlo = pl.program_id(1) * tq
    span = WINDOW + tq
    k0 = pl.multiple_of(jnp.clip(qlo - WINDOW, 0, t - span), LANES)
    qm = _masked_queries(q_ref[...] * (B_HD ** -0.5), tq)
    gs = g_ref[...]
    kb = kv_ref[pl.ds(k0, span), :GRP_LANES]
    vb = kv_ref[pl.ds(k0, span), GRP_LANES:]
    dist = qlo + lax.broadcasted_iota(jnp.int32, (tq, 1), 0) - (k0 + lax.broadcasted_iota(jnp.int32, (1, span), 1))
    distf = dist.astype(f32)
    amask = jnp.where((dist >= 0) & (dist < WINDOW), 0.0, NEG)
    grows = B_REP * tq
    for g in range(B_KV):
        sg = lax.dot_general(qm[g * grows:(g + 1) * grows], kb, _NT, preferred_element_type=f32)
        es, ls = [], []
        for r in range(B_REP):
            s = sg[r * tq:(r + 1) * tq] - slope_ref[g * B_REP + r] * distf + amask
            e = jnp.exp(s - jnp.max(s, axis=1, keepdims=True))
            es.append(e.astype(bf16))
            ls.append(jnp.sum(e, axis=1, keepdims=True))
        og = jnp.dot(jnp.concatenate(es, axis=0), vb, preferred_element_type=f32) / jnp.concatenate(ls, axis=0)
        _emit_group(o_ref, og, gs, 2, g, tq)


def _attn_nat(slopes, proj, kvb, sel, *, nb, t, tq, tk, branch):
    nqt = t // tq
    in_specs = [pl.BlockSpec(memory_space=pltpu.SMEM),
                pl.BlockSpec((tq, B_HEADS * B_HD), lambda b, i: (b * nqt + i, C_BQ // (B_HEADS * B_HD))),
                pl.BlockSpec((tq, LANES), lambda b, i: (b * nqt + i, C_SMALL // LANES)),
                pl.BlockSpec((t, KV_COLS), lambda b, i: (b, branch))]
    args = [slopes, proj, proj, kvb]
    if sel is None:
        body, scratch, name = functools.partial(_win_nat_kernel, tq=tq, t=t), [], "nsa_win"
    else:
        nblk = sel.shape[3]
        in_specs.append(pl.BlockSpec((1, B_KV, tq, nblk), lambda b, i: (b, 0, i, 0)))
        args.append(sel)
        body = functools.partial(_sel_nat_kernel, tq=tq, tk=tk, nblk=nblk)
        scratch = [pltpu.VMEM((B_HEADS * tq, GRP_LANES), bf16), pltpu.VMEM((B_HEADS * tq, 1), f32),
                   pltpu.VMEM((B_HEADS * tq, 1), f32), pltpu.VMEM((B_HEADS * tq, GRP_LANES), f32)]
        name = "nsa_sel"
    return pl.pallas_call(
        body,
        grid=(nb, nqt),
        in_specs=in_specs,
        out_specs=pl.BlockSpec((tq, B_HEADS * B_HD), lambda b, i: (b * nqt + i, 0)),
        out_shape=jax.ShapeDtypeStruct((proj.shape[0], B_HEADS * B_HD), f32),
        scratch_shapes=scratch,
        compiler_params=_cparams("parallel", "parallel"),
        name=name,
    )(*args)


PAD_PAGES = 4


def _gather_kernel(pt_ref, *refs):
    del pt_ref
    pages, tail_ref, o_ref = refs[:-2], refs[-2], refs[-1]
    for u, page in enumerate(pages):
        o_ref[0, u] = page[0]
    o_ref[0, len(pages)] = tail_ref[0]
    for u in range(len(pages) + 1, len(pages) + PAD_PAGES):
        o_ref[0, u] = jnp.zeros(o_ref.shape[2:], f32)


def _gather_pages(page_table, cache, tail):
    nb, n_pages = page_table.shape
    page = (1, PAGE_SIZE, KV_COLS)
    return pl.pallas_call(
        _gather_kernel,
        grid_spec=pltpu.PrefetchScalarGridSpec(
            num_scalar_prefetch=1,
            grid=(nb,),
            in_specs=[pl.BlockSpec(page, functools.partial(lambda b, pt, u: (pt[b, u], 0, 0), u=u))
                      for u in range(n_pages)] + [pl.BlockSpec(page, lambda b, pt: (b, 0, 0))],
            out_specs=pl.BlockSpec((1, n_pages + PAD_PAGES, PAGE_SIZE, KV_COLS), lambda b, pt: (b, 0, 0, 0)),
        ),
        out_shape=jax.ShapeDtypeStruct((nb, n_pages + PAD_PAGES, PAGE_SIZE, KV_COLS), f32),
        compiler_params=_cparams("parallel"),
        name="gather_pages",
    )(page_table, *([cache] * n_pages), tail)


def _layer_norm(z, g, b):
    mu = jnp.mean(z, axis=1, keepdims=True)
    var = jnp.mean(jnp.square(z - mu), axis=1, keepdims=True)
    return (z - mu) * lax.rsqrt(var + LN_EPS) * g + b


def _tail_kernel(x_ref, ya_ref, oc_ref, os_ref, ow_ref, wm_ref, wa_ref, wb_ref, wo_ref, g_ref, b_ref,
                 h_ref, hb_ref):
    x = x_ref[...]
    gates = jax.nn.sigmoid(jnp.dot(x.astype(bf16), wm_ref[...], preferred_element_type=f32))
    yb = oc_ref[...] + os_ref[...] + ow_ref[...]
    ma = jnp.dot(ya_ref[...].astype(bf16), wa_ref[...], preferred_element_type=f32)
    mb = jnp.dot(yb.astype(bf16), wb_ref[...], preferred_element_type=f32)
    merged = gates[:, :D_MODEL] * ma + gates[:, D_MODEL:] * mb
    z = DN_ALPHA * x + jnp.dot(merged.astype(bf16), wo_ref[...], preferred_element_type=f32)
    h = _layer_norm(z, g_ref[...], b_ref[...])
    h_ref[...] = h
    hb_ref[...] = h.astype(bf16)


def _tail(x, ya, oc, os_, ow, wm, wa, wb, wo, g, b, tm):
    n = x.shape[0]
    row = pl.BlockSpec((tm, D_MODEL), lambda i: (i, 0))
    return pl.pallas_call(
        _tail_kernel,
        grid=(n // tm,),
        in_specs=[row] * 5 + [_full(wm.shape), _full(wa.shape), _full(wb.shape), _full(wo.shape),
                              _full(g.shape), _full(b.shape)],
        out_specs=[row, row],
        out_shape=[jax.ShapeDtypeStruct((n, D_MODEL), f32), jax.ShapeDtypeStruct((n, D_MODEL), bf16)],
        compiler_params=_cparams("parallel"),
        name="tail",
    )(x, ya, oc, os_, ow, wm, wa, wb, wo, g, b)


def _top16(x):
    kk, tb = x.shape
    ji = lax.broadcasted_iota(jnp.int32, (kk, tb), 0).astype(f32)
    rank = jnp.full((kk, tb), float(P_TOPK), f32)
    vals = []
    for k in range(P_TOPK):
        mx = jnp.max(x, axis=0, keepdims=True)
        idx = jnp.min(jnp.where(x == mx, ji, float(kk)), axis=0, keepdims=True)
        hit = ji == idx
        rank = jnp.where(hit, float(k), rank)
        vals.append(mx)
        x = jnp.where(hit, NEG, x)
    return rank, vals


_CAND_ROWS8 = ((1, 8), (2, 5), (3, 4), (4, 3))


def _route_kernel(h_ref, wqt_ref, keys_ref, ta_ref, tb_ref):
    qpt = lax.dot_general(wqt_ref[...], h_ref[...], (((1,), (1,)), ((), ())), preferred_element_type=f32)
    tb = qpt.shape[1]
    sub16 = lax.broadcasted_iota(jnp.int32, (P_TOPK, tb), 0)
    sub8 = lax.broadcasted_iota(jnp.int32, (8, tb), 0)
    for p in range(P_HEADS):
        sc, rk, vl = [], [], []
        for c in range(2):
            qs = qpt[(2 * p + c) * P_DHALF:(2 * p + c + 1) * P_DHALF, :].astype(bf16)
            s = jnp.dot(keys_ref[p, c], qs, preferred_element_type=f32)
            r, v = _top16(s)
            sc.append(s)
            rk.append(r)
            vl.append(v)
        v0, v1 = vl
        col0 = functools.reduce(lambda acc, k: jnp.where(sub16 == k, v0[k], acc), range(P_TOPK), jnp.zeros((P_TOPK, tb), f32))
        col1 = functools.reduce(lambda acc, k: jnp.where(sub16 == k, v1[k], acc), range(P_TOPK), jnp.zeros((P_TOPK, tb), f32))
        segs = [v0[0] + col1]
        for k1, keep in _CAND_ROWS8:
            segs.append(jnp.where(sub8 < keep, v0[k1] + col1[0:8], NEG))
        first = jnp.where(sub8 < 2, v0[5], jnp.where(sub8 < 4, v0[6], v0[7]))
        second = jnp.where(sub8 % 2 == 0, v1[0], v1[1])
        segs.append(jnp.where(sub8 < 6, first + second, NEG))
        segs.append(col0[8:16] + v1[0])
        cand = jnp.concatenate(segs, axis=0)
        crank, cvals = _top16(cand)
        taken = jnp.where(crank < float(P_TOPK), 1.0, 0.0)
        z = functools.reduce(lambda acc, v: acc + jnp.exp(v - cvals[0]), cvals, jnp.zeros((1, tb), f32))
        cnt = [jnp.sum(taken[0:16], axis=0, keepdims=True)]
        for i in range(len(_CAND_ROWS8)):
            cnt.append(jnp.sum(taken[16 + 8 * i:24 + 8 * i], axis=0, keepdims=True))
        t5 = taken[48:56]
        for lo in (0, 2, 4):
            cnt.append(jnp.sum(jnp.where((sub8 >= lo) & (sub8 < lo + 2), t5, 0.0), axis=0, keepdims=True))
        for i in range(8):
            cnt.append(taken[56 + i:57 + i])
        n_a = functools.reduce(lambda acc, k: jnp.where(rk[0] == float(k), cnt[k], acc), range(P_TOPK),
                               jnp.zeros((P_NKEYS, tb), f32))
        ta_ref[p, 0] = n_a
        ta_ref[p, 1] = jnp.exp(sc[0] - v0[0])
        tb_ref[p, 0] = rk[1]
        tb_ref[p, 1] = jnp.exp(sc[1] - v1[0]) / z


def _route(hb, wqt, keys, tb):
    n = hb.shape[0]
    spec = pl.BlockSpec((P_HEADS, 2, P_NKEYS, tb), lambda i: (0, 0, 0, i))
    return pl.pallas_call(
        _route_kernel,
        grid=(n // tb,),
        in_specs=[pl.BlockSpec((tb, D_MODEL), lambda i: (i, 0)), _full(wqt.shape), _full(keys.shape)],
        out_specs=[spec, spec],
        out_shape=[jax.ShapeDtypeStruct((P_HEADS, 2, P_NKEYS, n), f32),
                   jax.ShapeDtypeStruct((P_HEADS, 2, P_NKEYS, n), f32)],
        compiler_params=_cparams("parallel"),
        name="peer_route",
    )(hb, wqt, keys)


GATE_ROWS = 64


def _experts_kernel(hb_ref, h_ref, ta_ref, tb_ref, u_ref, vt_ref, g_ref, b_ref, y_ref, acc_s, ht_s, pt_s, *, te):
    j = pl.program_id(1)

    @pl.when(j == 0)
    def _():
        acc_s[...] = jnp.zeros(acc_s.shape, f32)

    ht_s[...] = lax.dot_general(u_ref[...], hb_ref[...], _NT_DIMS, preferred_element_type=f32)
    for aa in range(te // P_NKEYS):
        a = j * (te // P_NKEYS) + aa
        n_rows = [ta_ref[p, 0, pl.ds(a, 1), :] for p in range(P_HEADS)]
        e0_rows = [ta_ref[p, 1, pl.ds(a, 1), :] for p in range(P_HEADS)]
        for lt in range(ht_s.shape[1] // LANES):
            ls = slice(lt * LANES, (lt + 1) * LANES)
            for half in range(2):
                bs_ = slice(half * GATE_ROWS, (half + 1) * GATE_ROWS)
                ex = slice(aa * P_NKEYS + half * GATE_ROWS, aa * P_NKEYS + (half + 1) * GATE_ROWS)
                w = jnp.zeros((GATE_ROWS, LANES), f32)
                for p in range(P_HEADS):
                    w = w + jnp.where(tb_ref[p, 0, bs_, ls] < n_rows[p][:, ls], e0_rows[p][:, ls] * tb_ref[p, 1, bs_, ls],
                                      0.0)
                hs = ht_s[ex, ls]
                act = 0.5 * hs * (1.0 + lax.erf(hs * (0.5 ** 0.5)))
                pt_s[ex, ls] = (w * act).astype(bf16)
    acc_s[...] += jnp.dot(vt_ref[...], pt_s[...], preferred_element_type=f32)

    @pl.when(j == pl.num_programs(1) - 1)
    def _():
        z = DN_ALPHA * h_ref[...] + acc_s[...].T
        y_ref[...] = _layer_norm(z, g_ref[...], b_ref[...])


def _experts(hb, h, ta, tbl, u, vt, g, b, tb, te):
    n = hb.shape[0]
    row = pl.BlockSpec((tb, D_MODEL), lambda i, j: (i, 0))
    tab = pl.BlockSpec((P_HEADS, 2, P_NKEYS, tb), lambda i, j: (0, 0, 0, i))
    return pl.pallas_call(
        functools.partial(_experts_kernel, te=te),
        grid=(n // tb, P_EXPERTS // te),
        in_specs=[row, row, tab, tab,
                  pl.BlockSpec((te, D_MODEL), lambda i, j: (j, 0)),
                  pl.BlockSpec((D_MODEL, te), lambda i, j: (0, j)),
                  pl.BlockSpec((1, D_MODEL), lambda i, j: (0, 0)),
                  pl.BlockSpec((1, D_MODEL), lambda i, j: (0, 0))],
        out_specs=row,
        out_shape=jax.ShapeDtypeStruct((n, D_MODEL), f32),
        scratch_shapes=[pltpu.VMEM((D_MODEL, tb), f32), pltpu.VMEM((te, tb), f32), pltpu.VMEM((te, tb), bf16)],
        compiler_params=_cparams("parallel", "arbitrary"),
        name="peer_experts",
    )(hb, h, ta, tbl, u, vt, g, b)


def _to_q_tiles(q2, nb, t, tq):
    q = q2.reshape(nb, t // tq, tq, B_KV, B_REP, B_HD).transpose(0, 3, 1, 4, 2, 5)
    return q.reshape(nb, B_KV, t // tq, B_REP * tq, B_HD).astype(bf16)


def _from_q_tiles(o, nb, t, tq):
    o = o.reshape(nb, B_KV, t // tq, B_REP, tq, B_HD).transpose(0, 2, 4, 1, 3, 5)
    return o.reshape(nb, t, B_HEADS * B_HD)


def _kv_heads(kv3):
    nb, tk, _ = kv3.shape
    kv = kv3.reshape(nb, tk, 2, B_KV, B_HD).transpose(2, 0, 3, 1, 4)
    return kv[0], kv[1]


def _nsa(slopes, proj, gsmall, pe2, w2, *, nb, t, tq, row0, qpos0, cmp_src, slc3, win3, win_kpos0, tk_sel, tk_win):
    qt = _to_q_tiles(proj[row0:row0 + nb * t, C_BQ:C_CMP], nb, t, tq)
    x2, cmp_rows, cmp_steps, cmp_colblk = cmp_src
    kvc = _compress(x2, pe2, w2, rows=cmp_rows, steps=cmp_steps, row0=0, colblk=cmp_colblk)
    kc, vc = _kv_heads(kvc.reshape(nb, -1, KV_COLS))
    o_cmp, sel = _cmp_topk(slopes, qt, kc, vc, gsmall, tq=tq, qpos0=qpos0, row0=row0)
    ks, vs = _kv_heads(slc3)
    o_slc = _attn(slopes, qt, ks.astype(bf16), vs.astype(bf16), gsmall, sel, tq=tq, tk=tk_sel, qpos0=qpos0,
                  kpos0=0, window=None, gate_lane=G_GATE + B_HEADS, row0=row0)
    kw, vw = _kv_heads(win3)
    o_win = _attn(slopes, qt, kw.astype(bf16), vw.astype(bf16), gsmall, None, tq=tq, tk=tk_win, qpos0=qpos0,
                  kpos0=win_kpos0, window=WINDOW, gate_lane=G_GATE + 2 * B_HEADS, row0=row0)
    return tuple(_from_q_tiles(o, nb, t, tq).reshape(nb * t, B_HEADS * B_HD) for o in (o_cmp, o_slc, o_win))


def kernel(x_prompt, x_sample, cache_cmp_kv, cache_slc_kv, cache_win_kv, state_C, state_n, state_m, page_table,
           w_in, b_in, norm_a_g, nsa_pe, nsa_w_cmp, w_br_a, w_br_b, w_merge, w_out, ln1_g, ln1_b,
           peer_wq, peer_keys, peer_u, peer_v, ln2_g, ln2_b):
    bp, tp, _ = x_prompt.shape
    bs, ts, _ = x_sample.shape
    tsp = 8
    n_p, n_s = bp * tp, bs * tsp
    past = page_table.shape[1] * PAGE_SIZE

    perm = np.concatenate([np.arange(0, 2048), np.arange(2056, 5640), np.arange(2048, 2056), np.arange(5640, 5688)])
    w_perm = jnp.pad(w_in[:, perm], ((0, 0), (0, C_END - perm.size)))
    b_perm = jnp.pad(b_in[perm], (0, C_END - perm.size))
    w_perm_b = w_perm.astype(bf16)
    slopes = jnp.asarray(2.0 ** (-8.0 * np.arange(1, B_HEADS + 1) / B_HEADS), f32)
    wc = nsa_w_cmp.reshape(2, CMP_BLOCK, 1, B_HD, 1, B_HD)
    eye2 = jnp.eye(2, dtype=f32).reshape(1, 1, 2, 1, 2, 1)
    w2 = (wc * eye2).reshape(2, CMP_BLOCK * LANES, LANES).astype(bf16)
    pe2 = jnp.tile(nsa_pe, (1, 1, 2)).reshape(2, 1, CMP_BLOCK * LANES)

    xs_pad = jnp.pad(x_sample, ((0, 0), (0, tsp - ts), (0, 0)))
    x_all = jnp.concatenate([x_prompt.reshape(n_p, D_MODEL), xs_pad.reshape(n_s, D_MODEL)], axis=0)
    xb = x_all.astype(bf16)
    proj, kvb = _proj(xb, w_perm_b, b_perm.reshape(1, C_END), 256)
    gt = _proj_t(w_perm_b[:, C_SMALL:].T, xb, b_perm[C_SMALL:].reshape(LANES, 1), 512)

    zc = jnp.zeros((bp, A_HEADS, A_DQK, A_DV), f32)
    zn = jnp.zeros((bp, A_HEADS, 1, A_DQK), f32)
    zm = jnp.zeros((bp, A_HEADS, 1, 1), f32)
    ng = norm_a_g.reshape(1, A_HEADS * A_DV)
    ya, p_c, p_n, p_m = _mlstm(proj, gt, ng, zc, zn, zm, row0=0, nb=bp, t=tp, L=256, valid=256)
    ya, s_c, s_n, s_m = _mlstm(proj, gt, ng, state_C, state_n.reshape(bs, A_HEADS, 1, A_DQK),
                               state_m.reshape(bs, A_HEADS, 1, 1), row0=n_p, nb=bs, t=tsp, L=tsp, valid=ts, y_buf=ya)

    wt_kv = w_perm_b[:, C_CMP:C_SMALL].T
    b_kv = b_perm[C_CMP:C_SMALL].reshape(C_SMALL - C_CMP, 1)
    kvt_p = _proj_kvt(wt_kv, xb, b_kv, row0=0, nb=bp, t=tp, tn=512)
    kvt_s = _proj_kvt(wt_kv, xb, b_kv, row0=n_p, nb=1, t=n_s, tn=512)
    kvt_s = [a.reshape(KV_COLS, bs, tsp).transpose(1, 0, 2) for a in kvt_s]
    to_rows = lambda a: a.reshape(a.shape[0], 2, B_KV, B_HD, a.shape[2]).transpose(0, 4, 1, 2, 3)
    new_lanes = lambda a: jnp.pad(a, ((0, 0), (0, 0), (0, LANES - tsp)))

    kvc_p = _compress(proj, pe2, w2, rows=n_p, steps=1, row0=0, colblk=C_CMP // KV_COLS)
    oc, sel_p = _cmp_nat(slopes, proj, kvc_p.reshape(bp, tp // CMP_BLOCK, KV_COLS), nb=bp, t=tp, tq=256)
    os_ = _attn_nat(slopes, proj, kvb, sel_p, nb=bp, t=tp, tq=128, tk=1024, branch=1)
    ow = _attn_nat(slopes, proj, kvb, None, nb=bp, t=tp, tq=128, tk=512, branch=2)

    tk_s = past + PAD_PAGES * PAGE_SIZE
    wb = cache_win_kv.shape[1]
    cache_t = lambda c: c.transpose(0, 2, 3, 4, 1).reshape(c.shape[0], KV_COLS, c.shape[1])
    kvc_s = _compress_pages(page_table, cache_t(cache_cmp_kv), pe2, w2)
    kvc_s = jnp.pad(kvc_s, ((0, 0), (0, tk_s // CMP_BLOCK - kvc_s.shape[1]), (0, 0)))
    win_t = cache_t(cache_win_kv)
    s_win_t = jnp.concatenate([win_t[:, :, ts:], kvt_s[2][:, :, :ts]], axis=2)
    tk_sel = past + PAGE_SIZE
    nblk_s = tk_s // CMP_BLOCK
    expand = jnp.asarray(np.arange(tk_sel)[None, :] // CMP_BLOCK == np.arange(nblk_s)[:, None], bf16)
    oc, os_, ow = _decode(page_table, (oc, os_, ow), slopes, proj, kvc_s.reshape(bs, nblk_s, KV_COLS),
                          cache_t(cache_slc_kv), new_lanes(kvt_s[1]), win_t, new_lanes(kvt_s[2]), expand,
                          tq=tsp, row0=n_p, qpos0=past, win_kpos0=past - wb)

    h1, h1b = _tail(x_all, ya, oc, os_, ow, w_merge.astype(bf16), w_br_a.astype(bf16), w_br_b.astype(bf16),
                    w_out.astype(bf16), ln1_g.reshape(1, D_MODEL), ln1_b.reshape(1, D_MODEL), 256)
    tab_a, tab_b = _route(h1b, peer_wq.T.astype(bf16), peer_keys.astype(bf16), 256)
    y = _experts(h1b, h1, tab_a, tab_b, peer_u.astype(bf16), peer_v.T.astype(bf16), ln2_g.reshape(1, D_MODEL),
                 ln2_b.reshape(1, D_MODEL), 512, 2048)

    y_prompt = y[:n_p].reshape(bp, tp, D_MODEL)
    y_sample = y[n_p:].reshape(bs, tsp, D_MODEL)[:, :ts]
    dt = x_prompt.dtype
    return (y_prompt, y_sample, to_rows(kvt_p[0]), to_rows(kvt_p[1]), to_rows(kvt_p[2][:, :, -min(WINDOW, tp):]),
            p_c.astype(dt), p_n.astype(dt), p_m.astype(dt),
            to_rows(kvt_s[0][:, :, :ts]), to_rows(kvt_s[1][:, :, :ts]), to_rows(s_win_t),
            s_c.astype(state_C.dtype), s_n.astype(state_C.dtype), s_m.astype(state_C.dtype))
```

```python
import functools

import jax
import jax.numpy as jnp
import numpy as np
from jax import lax
from jax.experimental import pallas as pl
from jax.experimental.pallas import tpu as pltpu

D_MODEL = 1024
A_HEADS, A_DQK, A_DV = 4, 128, 256
B_HEADS, B_KV, B_HD = 16, 4, 64
B_REP = B_HEADS // B_KV
CMP_BLOCK = 64
N_SEL = 16
WINDOW = 512
PAGE_SIZE = 128
P_HEADS, P_NKEYS, P_DHALF, P_TOPK = 8, 128, 128, 16
P_EXPERTS = P_NKEYS * P_NKEYS
DN_ALPHA = 2.0 ** 0.25
LN_EPS = 1e-5
NEG = -1e30

LANES = 128
KV_COLS = 2 * B_KV * B_HD
VMEM_LIMIT = 56 * 1024 * 1024

C_AQ, C_AK, C_AV, C_AO, C_BQ, C_CMP, C_SLC, C_WIN, C_SMALL, C_END = (
    0, 512, 1024, 2048, 3072, 4096, 4608, 5120, 5632, 5760)
G_I, G_F, G_GATE = 0, A_HEADS, 2 * A_HEADS

bf16 = jnp.bfloat16
f32 = jnp.float32
_NT_DIMS = (((1,), (1,)), ((), ()))


def _cparams(*sem, flags=None):
    return pltpu.CompilerParams(dimension_semantics=sem, vmem_limit_bytes=VMEM_LIMIT, flags=flags)


def _full(shape):
    nd = len(shape)
    return pl.BlockSpec(shape, lambda *_: (0,) * nd)


def _proj_kernel(x_ref, w_ref, b_ref, o_ref, kvb_ref):
    res = jnp.dot(x_ref[...], w_ref[...], preferred_element_type=f32) + b_ref[...]
    o_ref[...] = res
    kvb_ref[...] = res[:, C_CMP:C_SMALL].astype(bf16)


def _proj(xb, w, b, tm):
    n, k = xb.shape
    e = w.shape[1]
    return pl.pallas_call(
        _proj_kernel,
        grid=(n // tm,),
        in_specs=[pl.BlockSpec((tm, k), lambda i: (i, 0)), _full((k, e)), _full((1, e))],
        out_specs=[pl.BlockSpec((tm, e), lambda i: (i, 0)), pl.BlockSpec((tm, C_SMALL - C_CMP), lambda i: (i, 0))],
        out_shape=[jax.ShapeDtypeStruct((n, e), f32), jax.ShapeDtypeStruct((n, C_SMALL - C_CMP), bf16)],
        compiler_params=_cparams("parallel"),
        name="proj",
    )(xb, w, b)


def _proj_t_kernel(wt_ref, x_ref, b_ref, o_ref):
    o_ref[...] = lax.dot_general(wt_ref[...], x_ref[...], (((1,), (1,)), ((), ())),
                                 preferred_element_type=f32) + b_ref[...]


def _proj_t(wt, xb, bcol, tn):
    e, k = wt.shape
    n = xb.shape[0]
    return pl.pallas_call(
        _proj_t_kernel,
        grid=(n // tn,),
        in_specs=[_full((e, k)), pl.BlockSpec((tn, k), lambda i: (i, 0)), _full((e, 1))],
        out_specs=pl.BlockSpec((e, tn), lambda i: (0, i)),
        out_shape=jax.ShapeDtypeStruct((e, n), f32),
        compiler_params=_cparams("parallel"),
        name="proj_t",
    )(wt, xb, bcol)


def _proj_kvt_kernel(wt_ref, x_ref, b_ref, oc_ref, os_ref, ow_ref):
    res = lax.dot_general(wt_ref[...], x_ref[...], _NT_DIMS, preferred_element_type=f32) + b_ref[...]
    for i, o_ref in enumerate((oc_ref, os_ref, ow_ref)):
        o_ref[0] = res[i * KV_COLS:(i + 1) * KV_COLS]


def _proj_kvt(wt, xb, bcol, *, row0, nb, t, tn):
    e, k = wt.shape
    nt = t // tn
    rb0 = row0 // tn
    out = pl.BlockSpec((1, KV_COLS, tn), lambda b, i: (b, 0, i))
    return pl.pallas_call(
        _proj_kvt_kernel,
        grid=(nb, nt),
        in_specs=[_full((e, k)), pl.BlockSpec((tn, k), lambda b, i: (rb0 + b * nt + i, 0)), _full((e, 1))],
        out_specs=[out, out, out],
        out_shape=[jax.ShapeDtypeStruct((nb, KV_COLS, t), f32)] * 3,
        compiler_params=_cparams("parallel", "parallel"),
        name="proj_kvt",
    )(wt, xb, bcol)


def _mlstm_kernel(q_ref, k_ref, v_ref, ao_ref, g_ref, gt_ref, ng_ref, c0_ref, n0_ref, m0_ref,
                  y_ref, c_out, n_out, m_out, c_s, n_s, m_s, *, L, valid):
    c = pl.program_id(1)

    @pl.when(c == 0)
    def _():
        c_s[...] = c0_ref[0]
        n_s[...] = n0_ref[0]
        m_s[...] = m0_ref[0]

    g = g_ref[...]
    gt = gt_ref[0]
    t_col = lax.broadcasted_iota(jnp.int32, (L, 1), 0)
    s_row = lax.broadcasted_iota(jnp.int32, (1, L), 1)
    tt = lax.broadcasted_iota(jnp.int32, (L, L), 0)
    ss = lax.broadcasted_iota(jnp.int32, (L, L), 1)
    causal = ss <= tt
    for h in range(A_HEADS):
        q = q_ref[:, h * A_DQK:(h + 1) * A_DQK]
        k = k_ref[:, h * A_DQK:(h + 1) * A_DQK] * (A_DQK ** -0.5)
        v = v_ref[:, h * A_DV:(h + 1) * A_DV]
        i_col, f_col = g[:, G_I + h:G_I + h + 1], g[:, G_F + h:G_F + h + 1]
        i_row, f_row = gt[G_I + h:G_I + h + 1, :], gt[G_F + h:G_F + h + 1, :]
        lf_col = jax.nn.log_sigmoid(f_col)
        lf_row = jax.nn.log_sigmoid(f_row)
        if valid < L:
            lf_col = jnp.where(t_col < valid, lf_col, 0.0)
            lf_row = jnp.where(s_row < valid, lf_row, 0.0)
            i_col = jnp.where(t_col < valid, i_col, NEG)
            i_row = jnp.where(s_row < valid, i_row, NEG)
        b_col = jnp.sum(jnp.where(causal, lf_row, 0.0), axis=1, keepdims=True)
        b_row = jnp.sum(jnp.where(tt <= ss, lf_col, 0.0), axis=0, keepdims=True)
        m_prev = m_s[h]
        cmat = c_s[h]
        n_row = n_s[h]

        d_log = jnp.where(causal, b_col - b_row + i_row, NEG)
        inter = b_col + m_prev
        m_t = jnp.maximum(inter, jnp.max(d_log, axis=1, keepdims=True))
        qb = q.astype(bf16)
        qk = lax.dot_general(qb, k.astype(bf16), _NT_DIMS, preferred_element_type=f32)
        smat = qk * jnp.exp(d_log - m_t)
        w_inter = jnp.exp(inter - m_t)
        vb = v.astype(bf16)
        num = (w_inter * jnp.dot(qb, cmat.astype(bf16), preferred_element_type=f32)
               + jnp.dot(smat.astype(bf16), vb, preferred_element_type=f32))
        den = w_inter * jnp.sum(q * n_row, axis=1, keepdims=True) + jnp.sum(smat, axis=1, keepdims=True)
        hid = num / jnp.maximum(jnp.abs(den), jnp.exp(-m_t))
        mu = jnp.mean(hid, axis=1, keepdims=True)
        var = jnp.mean(jnp.square(hid - mu), axis=1, keepdims=True)
        hid = (hid - mu) * lax.rsqrt(var + LN_EPS) * ng_ref[:, h * A_DV:(h + 1) * A_DV]
        y_ref[:, h * A_DV:(h + 1) * A_DV] = hid * jax.nn.sigmoid(ao_ref[:, h * A_DV:(h + 1) * A_DV])

        b_end = b_col[L - 1:L, :]
        g_row = b_end - b_row + i_row
        m_new = jnp.maximum(b_end + m_prev, jnp.max(g_row, axis=1, keepdims=True))
        a = jnp.exp(b_end + m_prev - m_new)
        w_col = jnp.exp(b_end - b_col + i_col - m_new)
        kw = k * w_col
        c_s[h] = a * cmat + lax.dot_general(kw.astype(bf16), vb, (((0,), (0,)), ((), ())),
                                            preferred_element_type=f32)
        n_s[h] = a * n_row + jnp.sum(kw, axis=0, keepdims=True)
        m_s[h] = m_new

    @pl.when(c == pl.num_programs(1) - 1)
    def _():
        c_out[0] = c_s[...]
        n_out[0] = n_s[...]
        m_out[0] = m_s[...]


def _into(buf, kern):
    if buf is None:
        return kern, [], [], {}
    return (lambda buf_ref, *refs: kern(*refs)), [pl.BlockSpec(memory_space=pl.ANY)], [buf], {0: 0}


def _mlstm(proj, gt, norm_g, c0, n0, m0, *, row0, nb, t, L, valid, y_buf=None):
    nc = t // L
    rb0 = row0 // L
    gt = gt[:8, row0:row0 + nb * t].reshape(8, nb * nc, L).transpose(1, 0, 2)
    rows = lambda b, c: rb0 + b * nc + c
    st = lambda b, c: (b, 0, 0, 0)
    qk_w, v_w = A_HEADS * A_DQK, A_HEADS * A_DV
    kern, alias_specs, alias_args, aliases = _into(y_buf, functools.partial(_mlstm_kernel, L=L, valid=valid))
    y, c_f, n_f, m_f = pl.pallas_call(
        kern,
        grid=(nb, nc),
        input_output_aliases=aliases,
        in_specs=alias_specs + [
            pl.BlockSpec((L, qk_w), lambda b, c: (rows(b, c), C_AQ // qk_w)),
            pl.BlockSpec((L, qk_w), lambda b, c: (rows(b, c), C_AK // qk_w)),
            pl.BlockSpec((L, v_w), lambda b, c: (rows(b, c), C_AV // v_w)),
            pl.BlockSpec((L, v_w), lambda b, c: (rows(b, c), C_AO // v_w)),
            pl.BlockSpec((L, LANES), lambda b, c: (rows(b, c), C_SMALL // LANES)),
            pl.BlockSpec((1, 8, L), lambda b, c: (b * nc + c, 0, 0)),
            pl.BlockSpec((1, v_w), lambda b, c: (0, 0)),
            pl.BlockSpec((1, A_HEADS, A_DQK, A_DV), st),
            pl.BlockSpec((1, A_HEADS, 1, A_DQK), st),
            pl.BlockSpec((1, A_HEADS, 1, 1), st),
        ],
        out_specs=[
            pl.BlockSpec((L, v_w), lambda b, c: (rows(b, c), 0)),
            pl.BlockSpec((1, A_HEADS, A_DQK, A_DV), st),
            pl.BlockSpec((1, A_HEADS, 1, A_DQK), st),
            pl.BlockSpec((1, A_HEADS, 1, 1), st),
        ],
        out_shape=[
            jax.ShapeDtypeStruct((proj.shape[0], A_HEADS * A_DV), f32),
            jax.ShapeDtypeStruct((nb, A_HEADS, A_DQK, A_DV), f32),
            jax.ShapeDtypeStruct((nb, A_HEADS, 1, A_DQK), f32),
            jax.ShapeDtypeStruct((nb, A_HEADS, 1, 1), f32),
        ],
        scratch_shapes=[pltpu.VMEM((A_HEADS, A_DQK, A_DV), f32), pltpu.VMEM((A_HEADS, 1, A_DQK), f32),
                        pltpu.VMEM((A_HEADS, 1, 1), f32)],
        compiler_params=_cparams("parallel", "arbitrary"),
        name="mlstm",
    )(*alias_args, proj, proj, proj, proj, proj, gt, norm_g, c0, n0, m0)
    return y, c_f, n_f[:, :, 0], m_f[:, :, 0, 0]


def _compress_kernel(x_ref, pe_ref, w_ref, o_ref, xf_ref, *, nblk):
    for l in range(CMP_BLOCK):
        xf_ref[:, l * LANES:(l + 1) * LANES] = x_ref[pl.ds(l, nblk, stride=CMP_BLOCK), :]
    xf = (xf_ref[...] + pe_ref[0]).astype(bf16)
    o_ref[...] = jnp.dot(xf, w_ref[0], preferred_element_type=f32)


def _compress(x2, pe2, w2, *, rows, steps, row0, colblk):
    nblk = rows // CMP_BLOCK
    kflat = CMP_BLOCK * LANES
    rb0 = row0 // rows
    ngrp = KV_COLS // LANES
    return pl.pallas_call(
        functools.partial(_compress_kernel, nblk=nblk),
        grid=(steps, ngrp),
        in_specs=[pl.BlockSpec((rows, LANES), lambda s, p: (rb0 + s, colblk * ngrp + p)),
                  pl.BlockSpec((1, 1, kflat), lambda s, p: (p // 2, 0, 0)),
                  pl.BlockSpec((1, kflat, LANES), lambda s, p: (p // 2, 0, 0))],
        out_specs=pl.BlockSpec((nblk, LANES), lambda s, p: (s, p)),
        out_shape=jax.ShapeDtypeStruct((steps * nblk, KV_COLS), f32),
        scratch_shapes=[pltpu.VMEM((nblk, kflat), f32)],
        compiler_params=_cparams("parallel", "parallel"),
        name="compress",
    )(x2, pe2, w2)


SEQS_PER_STEP = 8


def _compress_pages_kernel(pt_ref, *refs):
    del pt_ref
    pages = refs[:SEQS_PER_STEP]
    pe_ref, w_ref, o_ref = refs[SEQS_PER_STEP:SEQS_PER_STEP + 3]
    tm = refs[SEQS_PER_STEP + 3:SEQS_PER_STEP + 3 + KV_COLS // LANES]
    xf_s = refs[-1]
    j = pl.program_id(1)
    blocks = SEQS_PER_STEP * PAGE_SIZE // CMP_BLOCK
    for u, page in enumerate(pages):
        rows = page[0].T
        for p, tm_p in enumerate(tm):
            tm_p[u * PAGE_SIZE:(u + 1) * PAGE_SIZE, :] = rows[:, p * LANES:(p + 1) * LANES]
    dst = pl.ds(pl.multiple_of(j * blocks, blocks), blocks)
    for p, tm_p in enumerate(tm):
        for l in range(CMP_BLOCK):
            piece = tm_p[pl.ds(l, blocks, stride=CMP_BLOCK), :] + pe_ref[p // 2, :, l * LANES:(l + 1) * LANES]
            xf_s[p, dst, l * LANES:(l + 1) * LANES] = piece.astype(bf16)

    @pl.when(j == pl.num_programs(1) - 1)
    def _():
        for p in range(len(tm)):
            o_ref[0, :, p * LANES:(p + 1) * LANES] = jnp.dot(xf_s[p], w_ref[p // 2], preferred_element_type=f32)


def _compress_pages(page_table, cache_t, pe2, w2):
    nb, n_pages = page_table.shape
    groups = nb // SEQS_PER_STEP
    kflat = CMP_BLOCK * LANES
    rows = SEQS_PER_STEP * n_pages * PAGE_SIZE // CMP_BLOCK
    ngrp = KV_COLS // LANES
    out = pl.pallas_call(
        _compress_pages_kernel,
        grid_spec=pltpu.PrefetchScalarGridSpec(
            num_scalar_prefetch=1,
            grid=(groups, n_pages),
            in_specs=[pl.BlockSpec((1, KV_COLS, PAGE_SIZE),
                                   functools.partial(lambda s, j, pt, u: (pt[s * SEQS_PER_STEP + u, j], 0, 0), u=u))
                      for u in range(SEQS_PER_STEP)]
            + [pl.BlockSpec((2, 1, kflat), lambda s, j, pt: (0, 0, 0)),
               pl.BlockSpec((2, kflat, LANES), lambda s, j, pt: (0, 0, 0))],
            out_specs=pl.BlockSpec((1, rows, KV_COLS), lambda s, j, pt: (s, 0, 0)),
            scratch_shapes=[pltpu.VMEM((SEQS_PER_STEP * PAGE_SIZE, LANES), f32) for _ in range(ngrp)]
            + [pltpu.VMEM((ngrp, rows, kflat), bf16)],
        ),
        out_shape=jax.ShapeDtypeStruct((groups, rows, KV_COLS), f32),
        compiler_params=_cparams("parallel", "arbitrary"),
        name="compress_pages",
    )(page_table, *([cache_t] * SEQS_PER_STEP), pe2, w2)
    out = out.reshape(groups, n_pages, SEQS_PER_STEP, PAGE_SIZE // CMP_BLOCK, KV_COLS).transpose(0, 2, 1, 3, 4)
    return out.reshape(nb, n_pages * PAGE_SIZE // CMP_BLOCK, KV_COLS)


def _roll_lanes(x, shift):
    return x if shift == 0 else pltpu.roll(x, shift, axis=1)


def _softmax_rows(s, mask):
    s = jnp.where(mask, s, NEG)
    e = jnp.exp(s - jnp.max(s, axis=1, keepdims=True))
    return jnp.where(mask, e, 0.0), jnp.sum(e, axis=1, keepdims=True)


def _decode_kernel(slope_ref, q_ref, g_ref, kvc_ref, pages, slc_new_ref, win_ref, win_new_ref, exp_ref,
                   oc_ref, os_ref, ow_ref, kt_s, vt_s, wkt_s, wvt_s, *, tq, nblk, qpos0, win_kpos0):
    nrow = B_HEADS * tq
    gs = g_ref[...]
    qm = _masked_queries(q_ref[...], tq)
    row = lax.broadcasted_iota(jnp.int32, (nrow, 1), 0)
    qpos = qpos0 + row % tq
    slope = functools.reduce(lambda acc, h: jnp.where(row // tq == h, slope_ref[h], acc), range(B_HEADS),
                             jnp.zeros((nrow, 1), f32))
    scale = B_HD ** -0.5
    nt = _NT_DIMS

    def emit(o_ref, o, branch):
        for g in range(B_KV):
            _emit_group(o_ref, o[g * B_REP * tq:(g + 1) * B_REP * tq], gs, branch, g, tq)

    kvc = kvc_ref[0]
    j = lax.broadcasted_iota(jnp.int32, (1, nblk), 1)
    dist = qpos - ((j + 1) * CMP_BLOCK - 1)
    s = lax.dot_general(qm, kvc[:, :B_KV * B_HD].astype(bf16), nt, preferred_element_type=f32) * scale
    e, l = _softmax_rows(s - slope * dist.astype(f32), dist >= 0)
    p = e / l
    emit(oc_ref, jnp.dot(p.astype(bf16), kvc[:, B_KV * B_HD:].astype(bf16), preferred_element_type=f32), 0)
    imp = jnp.concatenate(
        [functools.reduce(lambda a, b: a + b, [p[(g * B_REP + r) * tq:(g * B_REP + r + 1) * tq] for r in range(B_REP)])
         for g in range(B_KV)], axis=0)
    cur = (qpos0 + lax.broadcasted_iota(jnp.int32, (B_KV * tq, 1), 0) % tq) // CMP_BLOCK
    imp = jnp.where((j == cur) | (j == 0), float(B_REP + 1), imp)
    imp = jnp.where(j > cur, -1.0, imp)
    rank = jnp.zeros(imp.shape, f32)
    for other in range(nblk):
        col = imp[:, other:other + 1]
        rank = rank + jnp.where((col > imp) | ((col == imp) & (j > other)), 1.0, 0.0)
    sel = jnp.where(rank < float(N_SEL), 1.0, 0.0)
    sel_rows = jnp.concatenate([sel[g * tq:(g + 1) * tq] for g in range(B_KV) for _ in range(B_REP)], axis=0)

    def transposed_kv(parts, kt_s, vt_s):
        for u, part in enumerate(parts):
            w = part.shape[-1]
            kt_s[:, u * w:(u + 1) * w] = part[0, :GRP_LANES, :].astype(bf16)
            vt_s[:, u * w:(u + 1) * w] = part[0, GRP_LANES:, :].astype(bf16)

    def attend(kt_s, vt_s, mask):
        s = jnp.dot(qm, kt_s[...], preferred_element_type=f32) * scale
        e, l = _softmax_rows(s - slope * dist.astype(f32), mask)
        return lax.dot_general(e.astype(bf16), vt_s[...], nt, preferred_element_type=f32) / l

    transposed_kv(list(pages) + [slc_new_ref], kt_s, vt_s)
    dist = qpos - lax.broadcasted_iota(jnp.int32, (1, kt_s.shape[1]), 1)
    picked = jnp.dot(sel_rows.astype(bf16), exp_ref[...], preferred_element_type=f32) > 0.5
    emit(os_ref, attend(kt_s, vt_s, picked & (dist >= 0)), 1)

    wb = win_ref.shape[-1]
    wkt_s[:, :wb] = win_ref[0, :GRP_LANES, :].astype(bf16)
    wvt_s[:, :wb] = win_ref[0, GRP_LANES:, :].astype(bf16)
    wkt_s[:, wb:] = win_new_ref[0, :GRP_LANES, :].astype(bf16)
    wvt_s[:, wb:] = win_new_ref[0, GRP_LANES:, :].astype(bf16)
    dist = qpos - (win_kpos0 + lax.broadcasted_iota(jnp.int32, (1, wkt_s.shape[1]), 1))
    emit(ow_ref, attend(wkt_s, wvt_s, (dist >= 0) & (dist < WINDOW)), 2)


def _decode(page_table, bufs, slopes, proj, kvc, slc_cache_t, slc_new_t, win_t, win_new_t, expand, *,
            tq, row0, qpos0, win_kpos0):
    nb, n_pages = page_table.shape
    nblk = kvc.shape[1]
    rb0 = row0 // tq
    tk = (n_pages + 1) * PAGE_SIZE
    twin = win_t.shape[-1] + win_new_t.shape[-1]
    out = pl.BlockSpec((tq, B_HEADS * B_HD), lambda b, pt: (rb0 + b, 0))
    page = (1, KV_COLS, PAGE_SIZE)

    def body(pt_ref, oc_buf, os_buf, ow_buf, slope_ref, q_ref, g_ref, kvc_ref, *refs):
        _decode_kernel(slope_ref, q_ref, g_ref, kvc_ref, refs[:n_pages], *refs[n_pages:], tq=tq, nblk=nblk,
                       qpos0=qpos0, win_kpos0=win_kpos0)

    return pl.pallas_call(
        body,
        grid_spec=pltpu.PrefetchScalarGridSpec(
            num_scalar_prefetch=1,
            grid=(nb,),
            in_specs=[pl.BlockSpec(memory_space=pl.ANY)] * 3
            + [pl.BlockSpec(memory_space=pltpu.SMEM),
               pl.BlockSpec((tq, B_HEADS * B_HD), lambda b, pt: (rb0 + b, C_BQ // (B_HEADS * B_HD))),
               pl.BlockSpec((tq, LANES), lambda b, pt: (rb0 + b, C_SMALL // LANES)),
               pl.BlockSpec((1,) + kvc.shape[1:], lambda b, pt: (b, 0, 0))]
            + [pl.BlockSpec(page, functools.partial(lambda b, pt, u: (pt[b, u], 0, 0), u=u)) for u in range(n_pages)]
            + [pl.BlockSpec((1,) + slc_new_t.shape[1:], lambda b, pt: (b, 0, 0)),
               pl.BlockSpec((1,) + win_t.shape[1:], lambda b, pt: (b, 0, 0)),
               pl.BlockSpec((1,) + win_new_t.shape[1:], lambda b, pt: (b, 0, 0)),
               pl.BlockSpec(expand.shape, lambda b, pt: (0, 0))],
            out_specs=[out, out, out],
            scratch_shapes=[pltpu.VMEM((GRP_LANES, tk), bf16), pltpu.VMEM((GRP_LANES, tk), bf16),
                            pltpu.VMEM((GRP_LANES, twin), bf16), pltpu.VMEM((GRP_LANES, twin), bf16)],
        ),
        out_shape=[jax.ShapeDtypeStruct(b.shape, f32) for b in bufs],
        input_output_aliases={1: 0, 2: 1, 3: 2},
        compiler_params=_cparams("parallel"),
        name="nsa_decode",
    )(page_table, *bufs, slopes, proj, proj, kvc, *([slc_cache_t] * n_pages), slc_new_t, win_t, win_new_t, expand)


GRP_LANES = B_KV * B_HD
_NT = _NT_DIMS


def _masked_queries(q, tq):
    lane_grp = lax.broadcasted_iota(jnp.int32, (tq, GRP_LANES), 1) // B_HD
    rows = []
    for g in range(B_KV):
        qg = q[:, g * GRP_LANES:(g + 1) * GRP_LANES]
        for r in range(B_REP):
            rows.append(jnp.where(lane_grp == g, _roll_lanes(qg, ((g - r) % B_REP) * B_HD), 0.0))
    return jnp.concatenate(rows, axis=0).astype(bf16)


def _group_columns(slope_ref, g, tq, qlo):
    row = lax.broadcasted_iota(jnp.int32, (B_REP * tq, 1), 0)
    slope = functools.reduce(lambda acc, r: jnp.where(row // tq == r, slope_ref[g * B_REP + r], acc), range(B_REP),
                             jnp.zeros((B_REP * tq, 1), f32))
    return qlo + row % tq, slope


def _emit_group(o_ref, og, gs, branch, g, tq):
    lane_grp = lax.broadcasted_iota(jnp.int32, (tq, GRP_LANES), 1) // B_HD
    acc = jnp.zeros((tq, GRP_LANES), f32)
    for r in range(B_REP):
        c = G_GATE + branch * B_HEADS + g * B_REP + r
        oh = jnp.where(lane_grp == g, og[r * tq:(r + 1) * tq, :] * jax.nn.sigmoid(gs[:, c:c + 1]), 0.0)
        acc = acc + _roll_lanes(oh, ((r - g) % B_REP) * B_HD)
    o_ref[:, g * GRP_LANES:(g + 1) * GRP_LANES] = acc


def _cmp_nat_kernel(slope_ref, q_ref, g_ref, kvc_ref, o_ref, sel_ref, *, tq, nblk):
    qlo = pl.program_id(1) * tq
    qm = _masked_queries(q_ref[...], tq)
    gs = g_ref[...]
    kvc = kvc_ref[0]
    kc = kvc[:, :GRP_LANES].astype(bf16)
    vc = kvc[:, GRP_LANES:].astype(bf16)
    j = lax.broadcasted_iota(jnp.int32, (1, nblk), 1)
    jf = j.astype(f32)
    imps = []
    for g in range(B_KV):
        qpos, slope = _group_columns(slope_ref, g, tq, qlo)
        dist = qpos - ((j + 1) * CMP_BLOCK - 1)
        s = lax.dot_general(qm[g * B_REP * tq:(g + 1) * B_REP * tq], kc, _NT, preferred_element_type=f32) * (B_HD ** -0.5)
        e, l = _softmax_rows(s - slope * dist.astype(f32), dist >= 0)
        p = e / l
        _emit_group(o_ref, jnp.dot(p.astype(bf16), vc, preferred_element_type=f32), gs, 0, g, tq)
        imps.append(functools.reduce(lambda a, b: a + b, [p[r * tq:(r + 1) * tq] for r in range(B_REP)]))
    imp = jnp.concatenate(imps, axis=0)
    cur = (qlo + lax.broadcasted_iota(jnp.int32, (B_KV * tq, 1), 0) % tq) // CMP_BLOCK
    imp = jnp.where((j == cur) | (j == 0), float(B_REP + 1), imp)
    imp = jnp.where(j > cur, -1.0, imp)
    sel = jnp.zeros(imp.shape, f32)
    for _ in range(N_SEL):
        mx = jnp.max(imp, axis=1, keepdims=True)
        idx = jnp.min(jnp.where(imp == mx, jf, float(nblk)), axis=1, keepdims=True)
        hit = jf == idx
        sel = jnp.where(hit, 1.0, sel)
        imp = jnp.where(hit, NEG, imp)
    for g in range(B_KV):
        sel_ref[0, g] = sel[g * tq:(g + 1) * tq]


def _cmp_nat(slopes, proj, kvc, *, nb, t, tq):
    nqt = t // tq
    nblk = kvc.shape[1]
    return pl.pallas_call(
        functools.partial(_cmp_nat_kernel, tq=tq, nblk=nblk),
        grid=(nb, nqt),
        in_specs=[pl.BlockSpec(memory_space=pltpu.SMEM),
                  pl.BlockSpec((tq, B_HEADS * B_HD), lambda b, i: (b * nqt + i, C_BQ // (B_HEADS * B_HD))),
                  pl.BlockSpec((tq, LANES), lambda b, i: (b * nqt + i, C_SMALL // LANES)),
                  pl.BlockSpec((1, nblk, KV_COLS), lambda b, i: (b, 0, 0))],
        out_specs=[pl.BlockSpec((tq, B_HEADS * B_HD), lambda b, i: (b * nqt + i, 0)),
                   pl.BlockSpec((1, B_KV, tq, nblk), lambda b, i: (b, 0, i, 0))],
        out_shape=[jax.ShapeDtypeStruct((proj.shape[0], B_HEADS * B_HD), f32),
                   jax.ShapeDtypeStruct((nb, B_KV, t, nblk), f32)],
        compiler_params=_cparams("parallel", "parallel"),
        name="nsa_cmp",
    )(slopes, proj, proj, kvc)


def _sel_nat_kernel(slope_ref, q_ref, g_ref, kv_ref, sel_ref, o_ref, qm_s, m_s, l_s, acc_s, *, tq, tk, nblk):
    qlo = pl.program_id(1) * tq
    qm_s[...] = _masked_queries(q_ref[...] * (B_HD ** -0.5), tq)
    m_s[...] = jnp.full(m_s.shape, NEG, f32)
    l_s[...] = jnp.zeros(l_s.shape, f32)
    acc_s[...] = jnp.zeros(acc_s.shape, f32)
    selb = jnp.concatenate([sel_ref[0, g] for g in range(B_KV)], axis=0).astype(bf16)
    qpos = qlo + lax.broadcasted_iota(jnp.int32, (tq, 1), 0)
    grows = B_REP * tq

    def body(kt, carry):
        k0 = pl.multiple_of(kt * tk, tk)
        kb = kv_ref[pl.ds(k0, tk), :GRP_LANES]
        vb = kv_ref[pl.ds(k0, tk), GRP_LANES:]
        dist = qpos - (k0 + lax.broadcasted_iota(jnp.int32, (1, tk), 1))
        distf = dist.astype(f32)
        blk = lax.broadcasted_iota(jnp.int32, (nblk, tk), 0)
        kblk = (k0 + lax.broadcasted_iota(jnp.int32, (nblk, tk), 1)) // CMP_BLOCK
        picked = jnp.dot(selb, jnp.where(blk == kblk, 1.0, 0.0).astype(bf16), preferred_element_type=f32)
        for g in range(B_KV):
            sg = lax.dot_general(qm_s[g * grows:(g + 1) * grows, :], kb, _NT, preferred_element_type=f32)
            amask = jnp.where((picked[g * tq:(g + 1) * tq] > 0.5) & (dist >= 0), 0.0, NEG)
            ps, alphas = [], []
            for r in range(B_REP):
                rows = slice((g * B_REP + r) * tq, (g * B_REP + r + 1) * tq)
                s = sg[r * tq:(r + 1) * tq] - slope_ref[g * B_REP + r] * distf + amask
                m_old = m_s[rows, :]
                m_new = jnp.maximum(m_old, jnp.max(s, axis=1, keepdims=True))
                alpha = jnp.exp(m_old - m_new)
                p = jnp.exp(s - m_new)
                l_s[rows, :] = alpha * l_s[rows, :] + jnp.sum(p, axis=1, keepdims=True)
                m_s[rows, :] = m_new
                ps.append(p.astype(bf16))
                alphas.append(alpha)
            grp = slice(g * grows, (g + 1) * grows)
            acc_s[grp, :] = (jnp.concatenate(alphas, axis=0) * acc_s[grp, :]
                             + jnp.dot(jnp.concatenate(ps, axis=0), vb, preferred_element_type=f32))
        return carry

    lax.fori_loop(0, (qlo + tq - 1) // tk + 1, body, 0)
    gs = g_ref[...]
    for g in range(B_KV):
        rows = slice(g * grows, (g + 1) * grows)
        _emit_group(o_ref, acc_s[rows, :] / l_s[rows, :], gs, 1, g, tq)


def _win_nat_kernel(slope_ref, q_ref, g_ref, kv_ref, o_ref, *, tq, t):
    qlo = pl.program_id(1) * tq
    span = WINDOW + tq
    k0 = pl.multiple_of(jnp.clip(qlo - WINDOW, 0, t - span), LANES)
    qm = _masked_queries(q_ref[...] * (B_HD ** -0.5), tq)
    gs = g_ref[...]
    kb = kv_ref[pl.ds(k0, span), :GRP_LANES]
    vb = kv_ref[pl.ds(k0, span), GRP_LANES:]
    dist = qlo + lax.broadcasted_iota(jnp.int32, (tq, 1), 0) - (k0 + lax.broadcasted_iota(jnp.int32, (1, span), 1))
    distf = dist.astype(f32)
    amask = jnp.where((dist >= 0) & (dist < WINDOW), 0.0, NEG)
    grows = B_REP * tq
    for g in range(B_KV):
        sg = lax.dot_general(qm[g * grows:(g + 1) * grows], kb, _NT, preferred_element_type=f32)
        es, ls = [], []
        for r in range(B_REP):
            s = sg[r * tq:(r + 1) * tq] - slope_ref[g * B_REP + r] * distf + amask
            e = jnp.exp(s - jnp.max(s, axis=1, keepdims=True))
            es.append(e.astype(bf16))
            ls.append(jnp.sum(e, axis=1, keepdims=True))
        og = jnp.dot(jnp.concatenate(es, axis=0), vb, preferred_element_type=f32) / jnp.concatenate(ls, axis=0)
        _emit_group(o_ref, og, gs, 2, g, tq)


def _attn_nat(slopes, proj, kvb, sel, *, nb, t, tq, tk, branch):
    nqt = t // tq
    in_specs = [pl.BlockSpec(memory_space=pltpu.SMEM),
                pl.BlockSpec((tq, B_HEADS * B_HD), lambda b, i: (b * nqt + i, C_BQ // (B_HEADS * B_HD))),
                pl.BlockSpec((tq, LANES), lambda b, i: (b * nqt + i, C_SMALL // LANES)),
                pl.BlockSpec((t, KV_COLS), lambda b, i: (b, branch))]
    args = [slopes, proj, proj, kvb]
    if sel is None:
        body, scratch, name = functools.partial(_win_nat_kernel, tq=tq, t=t), [], "nsa_win"
    else:
        nblk = sel.shape[3]
        in_specs.append(pl.BlockSpec((1, B_KV, tq, nblk), lambda b, i: (b, 0, i, 0)))
        args.append(sel)
        body = functools.partial(_sel_nat_kernel, tq=tq, tk=tk, nblk=nblk)
        scratch = [pltpu.VMEM((B_HEADS * tq, GRP_LANES), bf16), pltpu.VMEM((B_HEADS * tq, 1), f32),
                   pltpu.VMEM((B_HEADS * tq, 1), f32), pltpu.VMEM((B_HEADS * tq, GRP_LANES), f32)]
        name = "nsa_sel"
    return pl.pallas_call(
        body,
        grid=(nb, nqt),
        in_specs=in_specs,
        out_specs=pl.BlockSpec((tq, B_HEADS * B_HD), lambda b, i: (b * nqt + i, 0)),
        out_shape=jax.ShapeDtypeStruct((proj.shape[0], B_HEADS * B_HD), f32),
        scratch_shapes=scratch,
        compiler_params=_cparams("parallel", "parallel"),
        name=name,
    )(*args)


PAD_PAGES = 4


def _layer_norm(z, g, b):
    mu = jnp.mean(z, axis=1, keepdims=True)
    var = jnp.mean(jnp.square(z - mu), axis=1, keepdims=True)
    return (z - mu) * lax.rsqrt(var + LN_EPS) * g + b


def _tail_kernel(x_ref, ya_ref, oc_ref, os_ref, ow_ref, wm_ref, wa_ref, wb_ref, wo_ref, g_ref, b_ref,
                 h_ref, hb_ref):
    x = x_ref[...]
    gates = jax.nn.sigmoid(jnp.dot(x.astype(bf16), wm_ref[...], preferred_element_type=f32))
    yb = oc_ref[...] + os_ref[...] + ow_ref[...]
    ma = jnp.dot(ya_ref[...].astype(bf16), wa_ref[...], preferred_element_type=f32)
    mb = jnp.dot(yb.astype(bf16), wb_ref[...], preferred_element_type=f32)
    merged = gates[:, :D_MODEL] * ma + gates[:, D_MODEL:] * mb
    z = DN_ALPHA * x + jnp.dot(merged.astype(bf16), wo_ref[...], preferred_element_type=f32)
    h = _layer_norm(z, g_ref[...], b_ref[...])
    h_ref[...] = h
    hb_ref[...] = h.astype(bf16)


def _tail(x, ya, oc, os_, ow, wm, wa, wb, wo, g, b, tm):
    n = x.shape[0]
    row = pl.BlockSpec((tm, D_MODEL), lambda i: (i, 0))
    return pl.pallas_call(
        _tail_kernel,
        grid=(n // tm,),
        in_specs=[row] * 5 + [_full(wm.shape), _full(wa.shape), _full(wb.shape), _full(wo.shape),
                              _full(g.shape), _full(b.shape)],
        out_specs=[row, row],
        out_shape=[jax.ShapeDtypeStruct((n, D_MODEL), f32), jax.ShapeDtypeStruct((n, D_MODEL), bf16)],
        compiler_params=_cparams("parallel"),
        name="tail",
    )(x, ya, oc, os_, ow, wm, wa, wb, wo, g, b)


def _top16(x):
    kk, tb = x.shape
    ji = lax.broadcasted_iota(jnp.int32, (kk, tb), 0).astype(f32)
    rank = jnp.full((kk, tb), float(P_TOPK), f32)
    vals = []
    for k in range(P_TOPK):
        mx = jnp.max(x, axis=0, keepdims=True)
        idx = jnp.min(jnp.where(x == mx, ji, float(kk)), axis=0, keepdims=True)
        hit = ji == idx
        rank = jnp.where(hit, float(k), rank)
        vals.append(mx)
        x = jnp.where(hit, NEG, x)
    return rank, vals


_CAND_ROWS8 = ((1, 8), (2, 5), (3, 4), (4, 3))


def _route_kernel(h_ref, wqt_ref, keys_ref, ta_ref, tb_ref):
    qpt = lax.dot_general(wqt_ref[...], h_ref[...], (((1,), (1,)), ((), ())), preferred_element_type=f32)
    tb = qpt.shape[1]
    sub16 = lax.broadcasted_iota(jnp.int32, (P_TOPK, tb), 0)
    sub8 = lax.broadcasted_iota(jnp.int32, (8, tb), 0)
    for p in range(P_HEADS):
        sc, rk, vl = [], [], []
        for c in range(2):
            qs = qpt[(2 * p + c) * P_DHALF:(2 * p + c + 1) * P_DHALF, :].astype(bf16)
            s = jnp.dot(keys_ref[p, c], qs, preferred_element_type=f32)
            r, v = _top16(s)
            sc.append(s)
            rk.append(r)
            vl.append(v)
        v0, v1 = vl
        col0 = functools.reduce(lambda acc, k: jnp.where(sub16 == k, v0[k], acc), range(P_TOPK), jnp.zeros((P_TOPK, tb), f32))
        col1 = functools.reduce(lambda acc, k: jnp.where(sub16 == k, v1[k], acc), range(P_TOPK), jnp.zeros((P_TOPK, tb), f32))
        segs = [v0[0] + col1]
        for k1, keep in _CAND_ROWS8:
            segs.append(jnp.where(sub8 < keep, v0[k1] + col1[0:8], NEG))
        first = jnp.where(sub8 < 2, v0[5], jnp.where(sub8 < 4, v0[6], v0[7]))
        second = jnp.where(sub8 % 2 == 0, v1[0], v1[1])
        segs.append(jnp.where(sub8 < 6, first + second, NEG))
        segs.append(col0[8:16] + v1[0])
        cand = jnp.concatenate(segs, axis=0)
        crank, cvals = _top16(cand)
        taken = jnp.where(crank < float(P_TOPK), 1.0, 0.0)
        z = functools.reduce(lambda acc, v: acc + jnp.exp(v - cvals[0]), cvals, jnp.zeros((1, tb), f32))
        cnt = [jnp.sum(taken[0:16], axis=0, keepdims=True)]
        for i in range(len(_CAND_ROWS8)):
            cnt.append(jnp.sum(taken[16 + 8 * i:24 + 8 * i], axis=0, keepdims=True))
        t5 = taken[48:56]
        for lo in (0, 2, 4):
            cnt.append(jnp.sum(jnp.where((sub8 >= lo) & (sub8 < lo + 2), t5, 0.0), axis=0, keepdims=True))
        for i in range(8):
            cnt.append(taken[56 + i:57 + i])
        n_a = functools.reduce(lambda acc, k: jnp.where(rk[0] == float(k), cnt[k], acc), range(P_TOPK),
                               jnp.zeros((P_NKEYS, tb), f32))
        ta_ref[p, 0] = n_a
        ta_ref[p, 1] = jnp.exp(sc[0] - v0[0])
        tb_ref[p, 0] = rk[1]
        tb_ref[p, 1] = jnp.exp(sc[1] - v1[0]) / z


def _route(hb, wqt, keys, tb):
    n = hb.shape[0]
    spec = pl.BlockSpec((P_HEADS, 2, P_NKEYS, tb), lambda i: (0, 0, 0, i))
    return pl.pallas_call(
        _route_kernel,
        grid=(n // tb,),
        in_specs=[pl.BlockSpec((tb, D_MODEL), lambda i: (i, 0)), _full(wqt.shape), _full(keys.shape)],
        out_specs=[spec, spec],
        out_shape=[jax.ShapeDtypeStruct((P_HEADS, 2, P_NKEYS, n), f32),
                   jax.ShapeDtypeStruct((P_HEADS, 2, P_NKEYS, n), f32)],
        compiler_params=_cparams("parallel"),
        name="peer_route",
    )(hb, wqt, keys)


def _experts_kernel(hb_ref, h_ref, ta_ref, tb_ref, u_ref, vt_ref, g_ref, b_ref, y_ref, acc_s, ht_s, pt_s, *, te):
    j = pl.program_id(1)

    @pl.when(j == 0)
    def _():
        acc_s[...] = jnp.zeros(acc_s.shape, f32)

    ht_s[...] = lax.dot_general(u_ref[...], hb_ref[...], _NT_DIMS, preferred_element_type=f32)
    for aa in range(te // P_NKEYS):
        a = j * (te // P_NKEYS) + aa
        n_rows = [ta_ref[p, 0, pl.ds(a, 1), :] for p in range(P_HEADS)]
        e0_rows = [ta_ref[p, 1, pl.ds(a, 1), :] for p in range(P_HEADS)]
        for lt in range(ht_s.shape[1] // LANES):
            ls = slice(lt * LANES, (lt + 1) * LANES)
            ex = slice(aa * P_NKEYS, (aa + 1) * P_NKEYS)
            w = jnp.zeros((P_NKEYS, LANES), f32)
            for p in range(P_HEADS):
                w = w + jnp.where(tb_ref[p, 0, :, ls] < n_rows[p][:, ls], e0_rows[p][:, ls] * tb_ref[p, 1, :, ls], 0.0)
            hs = ht_s[ex, ls]
            act = 0.5 * hs * (1.0 + lax.erf(hs * (0.5 ** 0.5)))
            pt_s[ex, ls] = (w * act).astype(bf16)
    acc_s[...] += jnp.dot(vt_ref[...], pt_s[...], preferred_element_type=f32)

    @pl.when(j == pl.num_programs(1) - 1)
    def _():
        z = DN_ALPHA * h_ref[...] + acc_s[...].T
        y_ref[...] = _layer_norm(z, g_ref[...], b_ref[...])


def _experts(hb, h, ta, tbl, u, vt, g, b, tb, te):
    n = hb.shape[0]
    row = pl.BlockSpec((tb, D_MODEL), lambda i, j: (i, 0))
    tab = pl.BlockSpec((P_HEADS, 2, P_NKEYS, tb), lambda i, j: (0, 0, 0, i))
    return pl.pallas_call(
        functools.partial(_experts_kernel, te=te),
        grid=(n // tb, P_EXPERTS // te),
        in_specs=[row, row, tab, tab,
                  pl.BlockSpec((te, D_MODEL), lambda i, j: (j, 0)),
                  pl.BlockSpec((D_MODEL, te), lambda i, j: (0, j)),
                  pl.BlockSpec((1, D_MODEL), lambda i, j: (0, 0)),
                  pl.BlockSpec((1, D_MODEL), lambda i, j: (0, 0))],
        out_specs=row,
        out_shape=jax.ShapeDtypeStruct((n, D_MODEL), f32),
        scratch_shapes=[pltpu.VMEM((D_MODEL, tb), f32), pltpu.VMEM((te, tb), f32), pltpu.VMEM((te, tb), bf16)],
        compiler_params=_cparams("parallel", "arbitrary"),
        name="peer_experts",
    )(hb, h, ta, tbl, u, vt, g, b)


def kernel(x_prompt, x_sample, cache_cmp_kv, cache_slc_kv, cache_win_kv, state_C, state_n, state_m, page_table,
           w_in, b_in, norm_a_g, nsa_pe, nsa_w_cmp, w_br_a, w_br_b, w_merge, w_out, ln1_g, ln1_b,
           peer_wq, peer_keys, peer_u, peer_v, ln2_g, ln2_b):
    bp, tp, _ = x_prompt.shape
    bs, ts, _ = x_sample.shape
    tsp = 8
    n_p, n_s = bp * tp, bs * tsp
    past = page_table.shape[1] * PAGE_SIZE

    perm = np.concatenate([np.arange(0, 2048), np.arange(2056, 5640), np.arange(2048, 2056), np.arange(5640, 5688)])
    w_perm = jnp.pad(w_in[:, perm], ((0, 0), (0, C_END - perm.size)))
    b_perm = jnp.pad(b_in[perm], (0, C_END - perm.size))
    w_perm_b = w_perm.astype(bf16)
    slopes = jnp.asarray(2.0 ** (-8.0 * np.arange(1, B_HEADS + 1) / B_HEADS), f32)
    wc = nsa_w_cmp.reshape(2, CMP_BLOCK, 1, B_HD, 1, B_HD)
    eye2 = jnp.eye(2, dtype=f32).reshape(1, 1, 2, 1, 2, 1)
    w2 = (wc * eye2).reshape(2, CMP_BLOCK * LANES, LANES).astype(bf16)
    pe2 = jnp.tile(nsa_pe, (1, 1, 2)).reshape(2, 1, CMP_BLOCK * LANES)

    xs_pad = jnp.pad(x_sample, ((0, 0), (0, tsp - ts), (0, 0)))
    x_all = jnp.concatenate([x_prompt.reshape(n_p, D_MODEL), xs_pad.reshape(n_s, D_MODEL)], axis=0)
    xb = x_all.astype(bf16)
    proj, kvb = _proj(xb, w_perm_b, b_perm.reshape(1, C_END), 256)
    gt = _proj_t(w_perm_b[:, C_SMALL:].T, xb, b_perm[C_SMALL:].reshape(LANES, 1), 512)

    zc = jnp.zeros((bp, A_HEADS, A_DQK, A_DV), f32)
    zn = jnp.zeros((bp, A_HEADS, 1, A_DQK), f32)
    zm = jnp.zeros((bp, A_HEADS, 1, 1), f32)
    ng = norm_a_g.reshape(1, A_HEADS * A_DV)
    ya, p_c, p_n, p_m = _mlstm(proj, gt, ng, zc, zn, zm, row0=0, nb=bp, t=tp, L=256, valid=256)
    ya, s_c, s_n, s_m = _mlstm(proj, gt, ng, state_C, state_n.reshape(bs, A_HEADS, 1, A_DQK),
                               state_m.reshape(bs, A_HEADS, 1, 1), row0=n_p, nb=bs, t=tsp, L=tsp, valid=ts, y_buf=ya)

    wt_kv = w_perm_b[:, C_CMP:C_SMALL].T
    b_kv = b_perm[C_CMP:C_SMALL].reshape(C_SMALL - C_CMP, 1)
    kvt_p = _proj_kvt(wt_kv, xb, b_kv, row0=0, nb=bp, t=tp, tn=512)
    kvt_s = _proj_kvt(wt_kv, xb, b_kv, row0=n_p, nb=1, t=n_s, tn=512)
    kvt_s = [a.reshape(KV_COLS, bs, tsp).transpose(1, 0, 2) for a in kvt_s]
    to_rows = lambda a: a.reshape(a.shape[0], 2, B_KV, B_HD, a.shape[2]).transpose(0, 4, 1, 2, 3)
    new_lanes = lambda a: jnp.pad(a, ((0, 0), (0, 0), (0, LANES - tsp)))

    kvc_p = _compress(proj, pe2, w2, rows=n_p, steps=1, row0=0, colblk=C_CMP // KV_COLS)
    oc, sel_p = _cmp_nat(slopes, proj, kvc_p.reshape(bp, tp // CMP_BLOCK, KV_COLS), nb=bp, t=tp, tq=256)
    os_ = _attn_nat(slopes, proj, kvb, sel_p, nb=bp, t=tp, tq=128, tk=1024, branch=1)
    ow = _attn_nat(slopes, proj, kvb, None, nb=bp, t=tp, tq=128, tk=512, branch=2)

    tk_s = past + PAD_PAGES * PAGE_SIZE
    wb = cache_win_kv.shape[1]
    cache_t = lambda c: c.transpose(0, 2, 3, 4, 1).reshape(c.shape[0], KV_COLS, c.shape[1])
    kvc_s = _compress_pages(page_table, cache_t(cache_cmp_kv), pe2, w2)
    kvc_s = jnp.pad(kvc_s, ((0, 0), (0, tk_s // CMP_BLOCK - kvc_s.shape[1]), (0, 0)))
    win_t = cache_t(cache_win_kv)
    s_win_t = jnp.concatenate([win_t[:, :, ts:], kvt_s[2][:, :, :ts]], axis=2)
    tk_sel = past + PAGE_SIZE
    nblk_s = tk_s // CMP_BLOCK
    expand = jnp.asarray(np.arange(tk_sel)[None, :] // CMP_BLOCK == np.arange(nblk_s)[:, None], bf16)
    oc, os_, ow = _decode(page_table, (oc, os_, ow), slopes, proj, kvc_s.reshape(bs, nblk_s, KV_COLS),
                          cache_t(cache_slc_kv), new_lanes(kvt_s[1]), win_t, new_lanes(kvt_s[2]), expand,
                          tq=tsp, row0=n_p, qpos0=past, win_kpos0=past - wb)

    h1, h1b = _tail(x_all, ya, oc, os_, ow, w_merge.astype(bf16), w_br_a.astype(bf16), w_br_b.astype(bf16),
                    w_out.astype(bf16), ln1_g.reshape(1, D_MODEL), ln1_b.reshape(1, D_MODEL), 256)
    tab_a, tab_b = _route(h1b, peer_wq.T.astype(bf16), peer_keys.astype(bf16), 256)
    y = _experts(h1b, h1, tab_a, tab_b, peer_u.astype(bf16), peer_v.T.astype(bf16), ln2_g.reshape(1, D_MODEL),
                 ln2_b.reshape(1, D_MODEL), 512, 2048)

    y_prompt = y[:n_p].reshape(bp, tp, D_MODEL)
    y_sample = y[n_p:].reshape(bs, tsp, D_MODEL)[:, :ts]
    dt = x_prompt.dtype
    return (y_prompt, y_sample, to_rows(kvt_p[0]), to_rows(kvt_p[1]), to_rows(kvt_p[2][:, :, -min(WINDOW, tp):]),
            p_c.astype(dt), p_n.astype(dt), p_m.astype(dt),
            to_rows(kvt_s[0][:, :, :ts]), to_rows(kvt_s[1][:, :, :ts]), to_rows(s_win_t),
            s_c.astype(state_C.dtype), s_n.astype(state_C.dtype), s_m.astype(state_C.dtype))
```

```python
import functools

import jax
import jax.numpy as jnp
import numpy as np
from jax import lax
from jax.experimental import pallas as pl
from jax.experimental.pallas import tpu as pltpu

D_MODEL = 1024
A_HEADS, A_DQK, A_DV = 4, 128, 256
B_HEADS, B_KV, B_HD = 16, 4, 64
B_REP = B_HEADS // B_KV
CMP_BLOCK = 64
N_SEL = 16
WINDOW = 512
PAGE_SIZE = 128
P_HEADS, P_NKEYS, P_DHALF, P_TOPK = 8, 128, 128, 16
P_EXPERTS = P_NKEYS * P_NKEYS
DN_ALPHA = 2.0 ** 0.25
LN_EPS = 1e-5
NEG = -1e30

LANES = 128
KV_COLS = 2 * B_KV * B_HD
VMEM_LIMIT = 56 * 1024 * 1024

C_AQ, C_AK, C_AV, C_AO, C_BQ, C_CMP, C_SLC, C_WIN, C_SMALL, C_END = (
    0, 512, 1024, 2048, 3072, 4096, 4608, 5120, 5632, 5760)
G_I, G_F, G_GATE = 0, A_HEADS, 2 * A_HEADS

bf16 = jnp.bfloat16
f32 = jnp.float32
_NT_DIMS = (((1,), (1,)), ((), ()))


def _cparams(*sem, flags=None):
    return pltpu.CompilerParams(dimension_semantics=sem, vmem_limit_bytes=VMEM_LIMIT, flags=flags)


def _full(shape):
    nd = len(shape)
    return pl.BlockSpec(shape, lambda *_: (0,) * nd)


def _proj_kernel(x_ref, w_ref, b_ref, o_ref, kvb_ref):
    res = jnp.dot(x_ref[...], w_ref[...], preferred_element_type=f32) + b_ref[...]
    o_ref[...] = res
    kvb_ref[...] = res[:, C_CMP:C_SMALL].astype(bf16)


def _proj(xb, w, b, tm):
    n, k = xb.shape
    e = w.shape[1]
    return pl.pallas_call(
        _proj_kernel,
        grid=(n // tm,),
        in_specs=[pl.BlockSpec((tm, k), lambda i: (i, 0)), _full((k, e)), _full((1, e))],
        out_specs=[pl.BlockSpec((tm, e), lambda i: (i, 0)), pl.BlockSpec((tm, C_SMALL - C_CMP), lambda i: (i, 0))],
        out_shape=[jax.ShapeDtypeStruct((n, e), f32), jax.ShapeDtypeStruct((n, C_SMALL - C_CMP), bf16)],
        compiler_params=_cparams("parallel"),
        name="proj",
    )(xb, w, b)


def _proj_t_kernel(wt_ref, x_ref, b_ref, o_ref):
    o_ref[...] = lax.dot_general(wt_ref[...], x_ref[...], (((1,), (1,)), ((), ())),
                                 preferred_element_type=f32) + b_ref[...]


def _proj_t(wt, xb, bcol, tn):
    e, k = wt.shape
    n = xb.shape[0]
    return pl.pallas_call(
        _proj_t_kernel,
        grid=(n // tn,),
        in_specs=[_full((e, k)), pl.BlockSpec((tn, k), lambda i: (i, 0)), _full((e, 1))],
        out_specs=pl.BlockSpec((e, tn), lambda i: (0, i)),
        out_shape=jax.ShapeDtypeStruct((e, n), f32),
        compiler_params=_cparams("parallel"),
        name="proj_t",
    )(wt, xb, bcol)


def _proj_kvt_kernel(wt_ref, x_ref, b_ref, oc_ref, os_ref, ow_ref):
    res = lax.dot_general(wt_ref[...], x_ref[...], _NT_DIMS, preferred_element_type=f32) + b_ref[...]
    for i, o_ref in enumerate((oc_ref, os_ref, ow_ref)):
        o_ref[0] = res[i * KV_COLS:(i + 1) * KV_COLS]


def _proj_kvt(wt, xb, bcol, *, row0, nb, t, tn):
    e, k = wt.shape
    nt = t // tn
    rb0 = row0 // tn
    out = pl.BlockSpec((1, KV_COLS, tn), lambda b, i: (b, 0, i))
    return pl.pallas_call(
        _proj_kvt_kernel,
        grid=(nb, nt),
        in_specs=[_full((e, k)), pl.BlockSpec((tn, k), lambda b, i: (rb0 + b * nt + i, 0)), _full((e, 1))],
        out_specs=[out, out, out],
        out_shape=[jax.ShapeDtypeStruct((nb, KV_COLS, t), f32)] * 3,
        compiler_params=_cparams("parallel", "parallel"),
        name="proj_kvt",
    )(wt, xb, bcol)


def _mlstm_kernel(q_ref, k_ref, v_ref, ao_ref, g_ref, gt_ref, ng_ref, c0_ref, n0_ref, m0_ref,
                  y_ref, c_out, n_out, m_out, c_s, n_s, m_s, *, L, valid):
    c = pl.program_id(1)

    @pl.when(c == 0)
    def _():
        c_s[...] = c0_ref[0]
        n_s[...] = n0_ref[0]
        m_s[...] = m0_ref[0]

    g = g_ref[...]
    gt = gt_ref[0]
    t_col = lax.broadcasted_iota(jnp.int32, (L, 1), 0)
    s_row = lax.broadcasted_iota(jnp.int32, (1, L), 1)
    tt = lax.broadcasted_iota(jnp.int32, (L, L), 0)
    ss = lax.broadcasted_iota(jnp.int32, (L, L), 1)
    causal = ss <= tt
    for h in range(A_HEADS):
        q = q_ref[:, h * A_DQK:(h + 1) * A_DQK]
        k = k_ref[:, h * A_DQK:(h + 1) * A_DQK] * (A_DQK ** -0.5)
        v = v_ref[:, h * A_DV:(h + 1) * A_DV]
        i_col, f_col = g[:, G_I + h:G_I + h + 1], g[:, G_F + h:G_F + h + 1]
        i_row, f_row = gt[G_I + h:G_I + h + 1, :], gt[G_F + h:G_F + h + 1, :]
        lf_col = jax.nn.log_sigmoid(f_col)
        lf_row = jax.nn.log_sigmoid(f_row)
        if valid < L:
            lf_col = jnp.where(t_col < valid, lf_col, 0.0)
            lf_row = jnp.where(s_row < valid, lf_row, 0.0)
            i_col = jnp.where(t_col < valid, i_col, NEG)
            i_row = jnp.where(s_row < valid, i_row, NEG)
        b_col = jnp.sum(jnp.where(causal, lf_row, 0.0), axis=1, keepdims=True)
        b_row = jnp.sum(jnp.where(tt <= ss, lf_col, 0.0), axis=0, keepdims=True)
        m_prev = m_s[h]
        cmat = c_s[h]
        n_row = n_s[h]

        d_log = jnp.where(causal, b_col - b_row + i_row, NEG)
        inter = b_col + m_prev
        m_t = jnp.maximum(inter, jnp.max(d_log, axis=1, keepdims=True))
        qb = q.astype(bf16)
        qk = lax.dot_general(qb, k.astype(bf16), _NT_DIMS, preferred_element_type=f32)
        smat = qk * jnp.exp(d_log - m_t)
        w_inter = jnp.exp(inter - m_t)
        vb = v.astype(bf16)
        num = (w_inter * jnp.dot(qb, cmat.astype(bf16), preferred_element_type=f32)
               + jnp.dot(smat.astype(bf16), vb, preferred_element_type=f32))
        den = w_inter * jnp.sum(q * n_row, axis=1, keepdims=True) + jnp.sum(smat, axis=1, keepdims=True)
        hid = num / jnp.maximum(jnp.abs(den), jnp.exp(-m_t))
        mu = jnp.mean(hid, axis=1, keepdims=True)
        var = jnp.mean(jnp.square(hid - mu), axis=1, keepdims=True)
        hid = (hid - mu) * lax.rsqrt(var + LN_EPS) * ng_ref[:, h * A_DV:(h + 1) * A_DV]
        y_ref[:, h * A_DV:(h + 1) * A_DV] = hid * jax.nn.sigmoid(ao_ref[:, h * A_DV:(h + 1) * A_DV])

        b_end = b_col[L - 1:L, :]
        g_row = b_end - b_row + i_row
        m_new = jnp.maximum(b_end + m_prev, jnp.max(g_row, axis=1, keepdims=True))
        a = jnp.exp(b_end + m_prev - m_new)
        w_col = jnp.exp(b_end - b_col + i_col - m_new)
        kw = k * w_col
        c_s[h] = a * cmat + lax.dot_general(kw.astype(bf16), vb, (((0,), (0,)), ((), ())),
                                            preferred_element_type=f32)
        n_s[h] = a * n_row + jnp.sum(kw, axis=0, keepdims=True)
        m_s[h] = m_new

    @pl.when(c == pl.num_programs(1) - 1)
    def _():
        c_out[0] = c_s[...]
        n_out[0] = n_s[...]
        m_out[0] = m_s[...]


def _into(buf, kern):
    if buf is None:
        return kern, [], [], {}
    return (lambda buf_ref, *refs: kern(*refs)), [pl.BlockSpec(memory_space=pl.ANY)], [buf], {0: 0}


def _mlstm(proj, gt, norm_g, c0, n0, m0, *, row0, nb, t, L, valid, y_buf=None):
    nc = t // L
    rb0 = row0 // L
    gt = gt[:8, row0:row0 + nb * t].reshape(8, nb * nc, L).transpose(1, 0, 2)
    rows = lambda b, c: rb0 + b * nc + c
    st = lambda b, c: (b, 0, 0, 0)
    qk_w, v_w = A_HEADS * A_DQK, A_HEADS * A_DV
    kern, alias_specs, alias_args, aliases = _into(y_buf, functools.partial(_mlstm_kernel, L=L, valid=valid))
    y, c_f, n_f, m_f = pl.pallas_call(
        kern,
        grid=(nb, nc),
        input_output_aliases=aliases,
        in_specs=alias_specs + [
            pl.BlockSpec((L, qk_w), lambda b, c: (rows(b, c), C_AQ // qk_w)),
            pl.BlockSpec((L, qk_w), lambda b, c: (rows(b, c), C_AK // qk_w)),
            pl.BlockSpec((L, v_w), lambda b, c: (rows(b, c), C_AV // v_w)),
            pl.BlockSpec((L, v_w), lambda b, c: (rows(b, c), C_AO // v_w)),
            pl.BlockSpec((L, LANES), lambda b, c: (rows(b, c), C_SMALL // LANES)),
            pl.BlockSpec((1, 8, L), lambda b, c: (b * nc + c, 0, 0)),
            pl.BlockSpec((1, v_w), lambda b, c: (0, 0)),
            pl.BlockSpec((1, A_HEADS, A_DQK, A_DV), st),
            pl.BlockSpec((1, A_HEADS, 1, A_DQK), st),
            pl.BlockSpec((1, A_HEADS, 1, 1), st),
        ],
        out_specs=[
            pl.BlockSpec((L, v_w), lambda b, c: (rows(b, c), 0)),
            pl.BlockSpec((1, A_HEADS, A_DQK, A_DV), st),
            pl.BlockSpec((1, A_HEADS, 1, A_DQK), st),
            pl.BlockSpec((1, A_HEADS, 1, 1), st),
        ],
        out_shape=[
            jax.ShapeDtypeStruct((proj.shape[0], A_HEADS * A_DV), f32),
            jax.ShapeDtypeStruct((nb, A_HEADS, A_DQK, A_DV), f32),
            jax.ShapeDtypeStruct((nb, A_HEADS, 1, A_DQK), f32),
            jax.ShapeDtypeStruct((nb, A_HEADS, 1, 1), f32),
        ],
        scratch_shapes=[pltpu.VMEM((A_HEADS, A_DQK, A_DV), f32), pltpu.VMEM((A_HEADS, 1, A_DQK), f32),
                        pltpu.VMEM((A_HEADS, 1, 1), f32)],
        compiler_params=_cparams("parallel", "arbitrary"),
        name="mlstm",
    )(*alias_args, proj, proj, proj, proj, proj, gt, norm_g, c0, n0, m0)
    return y, c_f, n_f[:, :, 0], m_f[:, :, 0, 0]


def _compress_kernel(x_ref, pe_ref, w_ref, o_ref, xf_ref, *, nblk):
    for l in range(CMP_BLOCK):
        xf_ref[:, l * LANES:(l + 1) * LANES] = x_ref[pl.ds(l, nblk, stride=CMP_BLOCK), :]
    xf = (xf_ref[...] + pe_ref[0]).astype(bf16)
    o_ref[...] = jnp.dot(xf, w_ref[0], preferred_element_type=f32)


def _compress(x2, pe2, w2, *, rows, steps, row0, colblk):
    nblk = rows // CMP_BLOCK
    kflat = CMP_BLOCK * LANES
    rb0 = row0 // rows
    ngrp = KV_COLS // LANES
    return pl.pallas_call(
        functools.partial(_compress_kernel, nblk=nblk),
        grid=(steps, ngrp),
        in_specs=[pl.BlockSpec((rows, LANES), lambda s, p: (rb0 + s, colblk * ngrp + p)),
                  pl.BlockSpec((1, 1, kflat), lambda s, p: (p // 2, 0, 0)),
                  pl.BlockSpec((1, kflat, LANES), lambda s, p: (p // 2, 0, 0))],
        out_specs=pl.BlockSpec((nblk, LANES), lambda s, p: (s, p)),
        out_shape=jax.ShapeDtypeStruct((steps * nblk, KV_COLS), f32),
        scratch_shapes=[pltpu.VMEM((nblk, kflat), f32)],
        compiler_params=_cparams("parallel", "parallel"),
        name="compress",
    )(x2, pe2, w2)


SEQS_PER_STEP = 8


def _compress_pages_kernel(pt_ref, *refs):
    del pt_ref
    pages = refs[:SEQS_PER_STEP]
    pe_ref, w_ref, o_ref = refs[SEQS_PER_STEP:SEQS_PER_STEP + 3]
    tm = refs[SEQS_PER_STEP + 3:SEQS_PER_STEP + 3 + KV_COLS // LANES]
    xf_s = refs[-1]
    j = pl.program_id(1)
    blocks = SEQS_PER_STEP * PAGE_SIZE // CMP_BLOCK
    for u, page in enumerate(pages):
        rows = page[0].T
        for p, tm_p in enumerate(tm):
            tm_p[u * PAGE_SIZE:(u + 1) * PAGE_SIZE, :] = rows[:, p * LANES:(p + 1) * LANES]
    dst = pl.ds(pl.multiple_of(j * blocks, blocks), blocks)
    for p, tm_p in enumerate(tm):
        for l in range(CMP_BLOCK):
            piece = tm_p[pl.ds(l, blocks, stride=CMP_BLOCK), :] + pe_ref[p // 2, :, l * LANES:(l + 1) * LANES]
            xf_s[p, dst, l * LANES:(l + 1) * LANES] = piece.astype(bf16)

    @pl.when(j == pl.num_programs(1) - 1)
    def _():
        for p in range(len(tm)):
            o_ref[0, :, p * LANES:(p + 1) * LANES] = jnp.dot(xf_s[p], w_ref[p // 2], preferred_element_type=f32)


def _compress_pages(page_table, cache_t, pe2, w2):
    nb, n_pages = page_table.shape
    groups = nb // SEQS_PER_STEP
    kflat = CMP_BLOCK * LANES
    rows = SEQS_PER_STEP * n_pages * PAGE_SIZE // CMP_BLOCK
    ngrp = KV_COLS // LANES
    out = pl.pallas_call(
        _compress_pages_kernel,
        grid_spec=pltpu.PrefetchScalarGridSpec(
            num_scalar_prefetch=1,
            grid=(groups, n_pages),
            in_specs=[pl.BlockSpec((1, KV_COLS, PAGE_SIZE),
                                   functools.partial(lambda s, j, pt, u: (pt[s * SEQS_PER_STEP + u, j], 0, 0), u=u))
                      for u in range(SEQS_PER_STEP)]
            + [pl.BlockSpec((2, 1, kflat), lambda s, j, pt: (0, 0, 0)),
               pl.BlockSpec((2, kflat, LANES), lambda s, j, pt: (0, 0, 0))],
            out_specs=pl.BlockSpec((1, rows, KV_COLS), lambda s, j, pt: (s, 0, 0)),
            scratch_shapes=[pltpu.VMEM((SEQS_PER_STEP * PAGE_SIZE, LANES), f32) for _ in range(ngrp)]
            + [pltpu.VMEM((ngrp, rows, kflat), bf16)],
        ),
        out_shape=jax.ShapeDtypeStruct((groups, rows, KV_COLS), f32),
        compiler_params=_cparams("parallel", "arbitrary"),
        name="compress_pages",
    )(page_table, *([cache_t] * SEQS_PER_STEP), pe2, w2)
    out = out.reshape(groups, n_pages, SEQS_PER_STEP, PAGE_SIZE // CMP_BLOCK, KV_COLS).transpose(0, 2, 1, 3, 4)
    return out.reshape(nb, n_pages * PAGE_SIZE // CMP_BLOCK, KV_COLS)


def _roll_lanes(x, shift):
    return x if shift == 0 else pltpu.roll(x, shift, axis=1)


def _softmax_rows(s, mask):
    s = jnp.where(mask, s, NEG)
    e = jnp.exp(s - jnp.max(s, axis=1, keepdims=True))
    return jnp.where(mask, e, 0.0), jnp.sum(e, axis=1, keepdims=True)


def _decode_kernel(slope_ref, q_ref, g_ref, kvc_ref, pages, slc_new_ref, win_ref, win_new_ref, exp_ref,
                   oc_ref, os_ref, ow_ref, kt_s, vt_s, wkt_s, wvt_s, *, tq, nblk, qpos0, win_kpos0):
    nrow = B_HEADS * tq
    gs = g_ref[...]
    qm = _masked_queries(q_ref[...], tq)
    row = lax.broadcasted_iota(jnp.int32, (nrow, 1), 0)
    qpos = qpos0 + row % tq
    slope = functools.reduce(lambda acc, h: jnp.where(row // tq == h, slope_ref[h], acc), range(B_HEADS),
                             jnp.zeros((nrow, 1), f32))
    scale = B_HD ** -0.5
    nt = _NT_DIMS

    def emit(o_ref, o, branch):
        for g in range(B_KV):
            _emit_group(o_ref, o[g * B_REP * tq:(g + 1) * B_REP * tq], gs, branch, g, tq)

    kvc = kvc_ref[0]
    j = lax.broadcasted_iota(jnp.int32, (1, nblk), 1)
    dist = qpos - ((j + 1) * CMP_BLOCK - 1)
    s = lax.dot_general(qm, kvc[:, :B_KV * B_HD].astype(bf16), nt, preferred_element_type=f32) * scale
    e, l = _softmax_rows(s - slope * dist.astype(f32), dist >= 0)
    p = e / l
    emit(oc_ref, jnp.dot(p.astype(bf16), kvc[:, B_KV * B_HD:].astype(bf16), preferred_element_type=f32), 0)
    imp = jnp.concatenate(
        [functools.reduce(lambda a, b: a + b, [p[(g * B_REP + r) * tq:(g * B_REP + r + 1) * tq] for r in range(B_REP)])
         for g in range(B_KV)], axis=0)
    cur = (qpos0 + lax.broadcasted_iota(jnp.int32, (B_KV * tq, 1), 0) % tq) // CMP_BLOCK
    imp = jnp.where((j == cur) | (j == 0), float(B_REP + 1), imp)
    imp = jnp.where(j > cur, -1.0, imp)
    rank = jnp.zeros(imp.shape, f32)
    for other in range(nblk):
        col = imp[:, other:other + 1]
        rank = rank + jnp.where((col > imp) | ((col == imp) & (j > other)), 1.0, 0.0)
    sel = jnp.where(rank < float(N_SEL), 1.0, 0.0)
    sel_rows = jnp.concatenate([sel[g * tq:(g + 1) * tq] for g in range(B_KV) for _ in range(B_REP)], axis=0)

    def transposed_kv(parts, kt_s, vt_s):
        for u, part in enumerate(parts):
            w = part.shape[-1]
            kt_s[:, u * w:(u + 1) * w] = part[0, :GRP_LANES, :].astype(bf16)
            vt_s[:, u * w:(u + 1) * w] = part[0, GRP_LANES:, :].astype(bf16)

    def attend(kt_s, vt_s, mask):
        s = jnp.dot(qm, kt_s[...], preferred_element_type=f32) * scale
        e, l = _softmax_rows(s - slope * dist.astype(f32), mask)
        return lax.dot_general(e.astype(bf16), vt_s[...], nt, preferred_element_type=f32) / l

    transposed_kv(list(pages) + [slc_new_ref], kt_s, vt_s)
    dist = qpos - lax.broadcasted_iota(jnp.int32, (1, kt_s.shape[1]), 1)
    picked = jnp.dot(sel_rows.astype(bf16), exp_ref[...], preferred_element_type=f32) > 0.5
    emit(os_ref, attend(kt_s, vt_s, picked & (dist >= 0)), 1)

    wb = win_ref.shape[-1]
    wkt_s[:, :wb] = win_ref[0, :GRP_LANES, :].astype(bf16)
    wvt_s[:, :wb] = win_ref[0, GRP_LANES:, :].astype(bf16)
    wkt_s[:, wb:] = win_new_ref[0, :GRP_LANES, :].astype(bf16)
    wvt_s[:, wb:] = win_new_ref[0, GRP_LANES:, :].astype(bf16)
    dist = qpos - (win_kpos0 + lax.broadcasted_iota(jnp.int32, (1, wkt_s.shape[1]), 1))
    emit(ow_ref, attend(wkt_s, wvt_s, (dist >= 0) & (dist < WINDOW)), 2)


def _decode(page_table, bufs, slopes, proj, kvc, slc_cache_t, slc_new_t, win_t, win_new_t, expand, *,
            tq, row0, qpos0, win_kpos0):
    nb, n_pages = page_table.shape
    nblk = kvc.shape[1]
    rb0 = row0 // tq
    tk = (n_pages + 1) * PAGE_SIZE
    twin = win_t.shape[-1] + win_new_t.shape[-1]
    out = pl.BlockSpec((tq, B_HEADS * B_HD), lambda b, pt: (rb0 + b, 0))
    page = (1, KV_COLS, PAGE_SIZE)

    def body(pt_ref, oc_buf, os_buf, ow_buf, slope_ref, q_ref, g_ref, kvc_ref, *refs):
        _decode_kernel(slope_ref, q_ref, g_ref, kvc_ref, refs[:n_pages], *refs[n_pages:], tq=tq, nblk=nblk,
                       qpos0=qpos0, win_kpos0=win_kpos0)

    return pl.pallas_call(
        body,
        grid_spec=pltpu.PrefetchScalarGridSpec(
            num_scalar_prefetch=1,
            grid=(nb,),
            in_specs=[pl.BlockSpec(memory_space=pl.ANY)] * 3
            + [pl.BlockSpec(memory_space=pltpu.SMEM),
               pl.BlockSpec((tq, B_HEADS * B_HD), lambda b, pt: (rb0 + b, C_BQ // (B_HEADS * B_HD))),
               pl.BlockSpec((tq, LANES), lambda b, pt: (rb0 + b, C_SMALL // LANES)),
               pl.BlockSpec((1,) + kvc.shape[1:], lambda b, pt: (b, 0, 0))]
            + [pl.BlockSpec(page, functools.partial(lambda b, pt, u: (pt[b, u], 0, 0), u=u)) for u in range(n_pages)]
            + [pl.BlockSpec((1,) + slc_new_t.shape[1:], lambda b, pt: (b, 0, 0)),
               pl.BlockSpec((1,) + win_t.shape[1:], lambda b, pt: (b, 0, 0)),
               pl.BlockSpec((1,) + win_new_t.shape[1:], lambda b, pt: (b, 0, 0)),
               pl.BlockSpec(expand.shape, lambda b, pt: (0, 0))],
            out_specs=[out, out, out],
            scratch_shapes=[pltpu.VMEM((GRP_LANES, tk), bf16), pltpu.VMEM((GRP_LANES, tk), bf16),
                            pltpu.VMEM((GRP_LANES, twin), bf16), pltpu.VMEM((GRP_LANES, twin), bf16)],
        ),
        out_shape=[jax.ShapeDtypeStruct(b.shape, f32) for b in bufs],
        input_output_aliases={1: 0, 2: 1, 3: 2},
        compiler_params=_cparams("parallel"),
        name="nsa_decode",
    )(page_table, *bufs, slopes, proj, proj, kvc, *([slc_cache_t] * n_pages), slc_new_t, win_t, win_new_t, expand)


GRP_LANES = B_KV * B_HD
_NT = _NT_DIMS


def _masked_queries(q, tq):
    lane_grp = lax.broadcasted_iota(jnp.int32, (tq, GRP_LANES), 1) // B_HD
    rows = []
    for g in range(B_KV):
        qg = q[:, g * GRP_LANES:(g + 1) * GRP_LANES]
        for r in range(B_REP):
            rows.append(jnp.where(lane_grp == g, _roll_lanes(qg, ((g - r) % B_REP) * B_HD), 0.0))
    return jnp.concatenate(rows, axis=0).astype(bf16)


def _group_columns(slope_ref, g, tq, qlo):
    row = lax.broadcasted_iota(jnp.int32, (B_REP * tq, 1), 0)
    slope = functools.reduce(lambda acc, r: jnp.where(row // tq == r, slope_ref[g * B_REP + r], acc), range(B_REP),
                             jnp.zeros((B_REP * tq, 1), f32))
    return qlo + row % tq, slope


def _emit_group(o_ref, og, gs, branch, g, tq):
    lane_grp = lax.broadcasted_iota(jnp.int32, (tq, GRP_LANES), 1) // B_HD
    acc = jnp.zeros((tq, GRP_LANES), f32)
    for r in range(B_REP):
        c = G_GATE + branch * B_HEADS + g * B_REP + r
        oh = jnp.where(lane_grp == g, og[r * tq:(r + 1) * tq, :] * jax.nn.sigmoid(gs[:, c:c + 1]), 0.0)
        acc = acc + _roll_lanes(oh, ((r - g) % B_REP) * B_HD)
    o_ref[:, g * GRP_LANES:(g + 1) * GRP_LANES] = acc


def _cmp_nat_kernel(slope_ref, q_ref, g_ref, kvc_ref, o_ref, sel_ref, *, tq, nblk):
    qlo = pl.program_id(1) * tq
    qm = _masked_queries(q_ref[...], tq)
    gs = g_ref[...]
    kvc = kvc_ref[0]
    kc = kvc[:, :GRP_LANES].astype(bf16)
    vc = kvc[:, GRP_LANES:].astype(bf16)
    j = lax.broadcasted_iota(jnp.int32, (1, nblk), 1)
    jf = j.astype(f32)
    imps = []
    for g in range(B_KV):
        qpos, slope = _group_columns(slope_ref, g, tq, qlo)
        dist = qpos - ((j + 1) * CMP_BLOCK - 1)
        s = lax.dot_general(qm[g * B_REP * tq:(g + 1) * B_REP * tq], kc, _NT, preferred_element_type=f32) * (B_HD ** -0.5)
        e, l = _softmax_rows(s - slope * dist.astype(f32), dist >= 0)
        p = e / l
        _emit_group(o_ref, jnp.dot(p.astype(bf16), vc, preferred_element_type=f32), gs, 0, g, tq)
        imps.append(functools.reduce(lambda a, b: a + b, [p[r * tq:(r + 1) * tq] for r in range(B_REP)]))
    imp = jnp.concatenate(imps, axis=0)
    cur = (qlo + lax.broadcasted_iota(jnp.int32, (B_KV * tq, 1), 0) % tq) // CMP_BLOCK
    imp = jnp.where((j == cur) | (j == 0), float(B_REP + 1), imp)
    imp = jnp.where(j > cur, -1.0, imp)
    sel = jnp.zeros(imp.shape, f32)
    for _ in range(N_SEL):
        mx = jnp.max(imp, axis=1, keepdims=True)
        idx = jnp.min(jnp.where(imp == mx, jf, float(nblk)), axis=1, keepdims=True)
        hit = jf == idx
        sel = jnp.where(hit, 1.0, sel)
        imp = jnp.where(hit, NEG, imp)
    for g in range(B_KV):
        sel_ref[0, g] = sel[g * tq:(g + 1) * tq]


def _cmp_nat(slopes, proj, kvc, *, nb, t, tq):
    nqt = t // tq
    nblk = kvc.shape[1]
    return pl.pallas_call(
        functools.partial(_cmp_nat_kernel, tq=tq, nblk=nblk),
        grid=(nb, nqt),
        in_specs=[pl.BlockSpec(memory_space=pltpu.SMEM),
                  pl.BlockSpec((tq, B_HEADS * B_HD), lambda b, i: (b * nqt + i, C_BQ // (B_HEADS * B_HD))),
                  pl.BlockSpec((tq, LANES), lambda b, i: (b * nqt + i, C_SMALL // LANES)),
                  pl.BlockSpec((1, nblk, KV_COLS), lambda b, i: (b, 0, 0))],
        out_specs=[pl.BlockSpec((tq, B_HEADS * B_HD), lambda b, i: (b * nqt + i, 0)),
                   pl.BlockSpec((1, B_KV, tq, nblk), lambda b, i: (b, 0, i, 0))],
        out_shape=[jax.ShapeDtypeStruct((proj.shape[0], B_HEADS * B_HD), f32),
                   jax.ShapeDtypeStruct((nb, B_KV, t, nblk), f32)],
        compiler_params=_cparams("parallel", "parallel"),
        name="nsa_cmp",
    )(slopes, proj, proj, kvc)


def _sel_nat_kernel(slope_ref, q_ref, g_ref, kv_ref, sel_ref, o_ref, qm_s, m_s, l_s, acc_s, *, tq, tk, nblk):
    qlo = pl.program_id(1) * tq
    qm_s[...] = _masked_queries(q_ref[...] * (B_HD ** -0.5), tq)
    m_s[...] = jnp.full(m_s.shape, NEG, f32)
    l_s[...] = jnp.zeros(l_s.shape, f32)
    acc_s[...] = jnp.zeros(acc_s.shape, f32)
    selb = jnp.concatenate([sel_ref[0, g] for g in range(B_KV)], axis=0).astype(bf16)
    qpos = qlo + lax.broadcasted_iota(jnp.int32, (tq, 1), 0)
    grows = B_REP * tq

    def body(kt, carry):
        k0 = pl.multiple_of(kt * tk, tk)
        kb = kv_ref[pl.ds(k0, tk), :GRP_LANES]
        vb = kv_ref[pl.ds(k0, tk), GRP_LANES:]
        dist = qpos - (k0 + lax.broadcasted_iota(jnp.int32, (1, tk), 1))
        distf = dist.astype(f32)
        blk = lax.broadcasted_iota(jnp.int32, (nblk, tk), 0)
        kblk = (k0 + lax.broadcasted_iota(jnp.int32, (nblk, tk), 1)) // CMP_BLOCK
        picked = jnp.dot(selb, jnp.where(blk == kblk, 1.0, 0.0).astype(bf16), preferred_element_type=f32)
        for g in range(B_KV):
            sg = lax.dot_general(qm_s[g * grows:(g + 1) * grows, :], kb, _NT, preferred_element_type=f32)
            amask = jnp.where((picked[g * tq:(g + 1) * tq] > 0.5) & (dist >= 0), 0.0, NEG)
            ps, alphas = [], []
            for r in range(B_REP):
                rows = slice((g * B_REP + r) * tq, (g * B_REP + r + 1) * tq)
                s = sg[r * tq:(r + 1) * tq] - slope_ref[g * B_REP + r] * distf + amask
                m_old = m_s[rows, :]
                m_new = jnp.maximum(m_old, jnp.max(s, axis=1, keepdims=True))
                alpha = jnp.exp(m_old - m_new)
                p = jnp.exp(s - m_new)
                l_s[rows, :] = alpha * l_s[rows, :] + jnp.sum(p, axis=1, keepdims=True)
                m_s[rows, :] = m_new
                ps.append(p.astype(bf16))
                alphas.append(alpha)
            grp = slice(g * grows, (g + 1) * grows)
            acc_s[grp, :] = (jnp.concatenate(alphas, axis=0) * acc_s[grp, :]
                             + jnp.dot(jnp.concatenate(ps, axis=0), vb, preferred_element_type=f32))
        return carry

    lax.fori_loop(0, (qlo + tq - 1) // tk + 1, body, 0)
    gs = g_ref[...]
    for g in range(B_KV):
        rows = slice(g * grows, (g + 1) * grows)
        _emit_group(o_ref, acc_s[rows, :] / l_s[rows, :], gs, 1, g, tq)


def _win_nat_kernel(slope_ref, q_ref, g_ref, kv_ref, o_ref, *, tq, t):
    qlo = pl.program_id(1) * tq
    span = WINDOW + tq
    k0 = pl.multiple_of(jnp.clip(qlo - WINDOW, 0, t - span), LANES)
    qm = _masked_queries(q_ref[...] * (B_HD ** -0.5), tq)
    gs = g_ref[...]
    kb = kv_ref[pl.ds(k0, span), :GRP_LANES]
    vb = kv_ref[pl.ds(k0, span), GRP_LANES:]
    dist = qlo + lax.broadcasted_iota(jnp.int32, (tq, 1), 0) - (k0 + lax.broadcasted_iota(jnp.int32, (1, span), 1))
    distf = dist.astype(f32)
    amask = jnp.where((dist >= 0) & (dist < WINDOW), 0.0, NEG)
    grows = B_REP * tq
    for g in range(B_KV):
        sg = lax.dot_general(qm[g * grows:(g + 1) * grows], kb, _NT, preferred_element_type=f32)
        es, ls = [], []
        for r in range(B_REP):
            s = sg[r * tq:(r + 1) * tq] - slope_ref[g * B_REP + r] * distf + amask
            e = jnp.exp(s - jnp.max(s, axis=1, keepdims=True))
            es.append(e.astype(bf16))
            ls.append(jnp.sum(e, axis=1, keepdims=True))
        og = jnp.dot(jnp.concatenate(es, axis=0), vb, preferred_element_type=f32) / jnp.concatenate(ls, axis=0)
        _emit_group(o_ref, og, gs, 2, g, tq)


def _attn_nat(slopes, proj, kvb, sel, *, nb, t, tq, tk, branch):
    nqt = t // tq
    in_specs = [pl.BlockSpec(memory_space=pltpu.SMEM),
                pl.BlockSpec((tq, B_HEADS * B_HD), lambda b, i: (b * nqt + i, C_BQ // (B_HEADS * B_HD))),
                pl.BlockSpec((tq, LANES), lambda b, i: (b * nqt + i, C_SMALL // LANES)),
                pl.BlockSpec((t, KV_COLS), lambda b, i: (b, branch))]
    args = [slopes, proj, proj, kvb]
    if sel is None:
        body, scratch, name = functools.partial(_win_nat_kernel, tq=tq, t=t), [], "nsa_win"
    else:
        nblk = sel.shape[3]
        in_specs.append(pl.BlockSpec((1, B_KV, tq, nblk), lambda b, i: (b, 0, i, 0)))
        args.append(sel)
        body = functools.partial(_sel_nat_kernel, tq=tq, tk=tk, nblk=nblk)
        scratch = [pltpu.VMEM((B_HEADS * tq, GRP_LANES), bf16), pltpu.VMEM((B_HEADS * tq, 1), f32),
                   pltpu.VMEM((B_HEADS * tq, 1), f32), pltpu.VMEM((B_HEADS * tq, GRP_LANES), f32)]
        name = "nsa_sel"
    return pl.pallas_call(
        body,
        grid=(nb, nqt),
        in_specs=in_specs,
        out_specs=pl.BlockSpec((tq, B_HEADS * B_HD), lambda b, i: (b * nqt + i, 0)),
        out_shape=jax.ShapeDtypeStruct((proj.shape[0], B_HEADS * B_HD), f32),
        scratch_shapes=scratch,
        compiler_params=_cparams("parallel", "parallel"),
        name=name,
    )(*args)


PAD_PAGES = 4


def _layer_norm(z, g, b):
    mu = jnp.mean(z, axis=1, keepdims=True)
    var = jnp.mean(jnp.square(z - mu), axis=1, keepdims=True)
    return (z - mu) * lax.rsqrt(var + LN_EPS) * g + b


def _tail_kernel(x_ref, ya_ref, oc_ref, os_ref, ow_ref, wm_ref, wa_ref, wb_ref, wo_ref, g_ref, b_ref,
                 h_ref, hb_ref):
    x = x_ref[...]
    gates = jax.nn.sigmoid(jnp.dot(x.astype(bf16), wm_ref[...], preferred_element_type=f32))
    yb = oc_ref[...] + os_ref[...] + ow_ref[...]
    ma = jnp.dot(ya_ref[...].astype(bf16), wa_ref[...], preferred_element_type=f32)
    mb = jnp.dot(yb.astype(bf16), wb_ref[...], preferred_element_type=f32)
    merged = gates[:, :D_MODEL] * ma + gates[:, D_MODEL:] * mb
    z = DN_ALPHA * x + jnp.dot(merged.astype(bf16), wo_ref[...], preferred_element_type=f32)
    h = _layer_norm(z, g_ref[...], b_ref[...])
    h_ref[...] = h
    hb_ref[...] = h.astype(bf16)


def _tail(x, ya, oc, os_, ow, wm, wa, wb, wo, g, b, tm):
    n = x.shape[0]
    row = pl.BlockSpec((tm, D_MODEL), lambda i: (i, 0))
    return pl.pallas_call(
        _tail_kernel,
        grid=(n // tm,),
        in_specs=[row] * 5 + [_full(wm.shape), _full(wa.shape), _full(wb.shape), _full(wo.shape),
                              _full(g.shape), _full(b.shape)],
        out_specs=[row, row],
        out_shape=[jax.ShapeDtypeStruct((n, D_MODEL), f32), jax.ShapeDtypeStruct((n, D_MODEL), bf16)],
        compiler_params=_cparams("parallel"),
        name="tail",
    )(x, ya, oc, os_, ow, wm, wa, wb, wo, g, b)


def _top16(x):
    kk, tb = x.shape
    ji = lax.broadcasted_iota(jnp.int32, (kk, tb), 0).astype(f32)
    rank = jnp.full((kk, tb), float(P_TOPK), f32)
    vals = []
    for k in range(P_TOPK):
        mx = jnp.max(x, axis=0, keepdims=True)
        idx = jnp.min(jnp.where(x == mx, ji, float(kk)), axis=0, keepdims=True)
        hit = ji == idx
        rank = jnp.where(hit, float(k), rank)
        vals.append(mx)
        x = jnp.where(hit, NEG, x)
    return rank, vals


_CAND_ROWS8 = ((1, 8), (2, 5), (3, 4), (4, 3))


def _route_kernel(h_ref, wqt_ref, keys_ref, ta_ref, tb_ref):
    qpt = lax.dot_general(wqt_ref[...], h_ref[...], (((1,), (1,)), ((), ())), preferred_element_type=f32)
    tb = qpt.shape[1]
    sub16 = lax.broadcasted_iota(jnp.int32, (P_TOPK, tb), 0)
    sub8 = lax.broadcasted_iota(jnp.int32, (8, tb), 0)
    for p in range(P_HEADS):
        sc, rk, vl = [], [], []
        for c in range(2):
            qs = qpt[(2 * p + c) * P_DHALF:(2 * p + c + 1) * P_DHALF, :].astype(bf16)
            s = jnp.dot(keys_ref[p, c], qs, preferred_element_type=f32)
            r, v = _top16(s)
            sc.append(s)
            rk.append(r)
            vl.append(v)
        v0, v1 = vl
        col0 = functools.reduce(lambda acc, k: jnp.where(sub16 == k, v0[k], acc), range(P_TOPK), jnp.zeros((P_TOPK, tb), f32))
        col1 = functools.reduce(lambda acc, k: jnp.where(sub16 == k, v1[k], acc), range(P_TOPK), jnp.zeros((P_TOPK, tb), f32))
        segs = [v0[0] + col1]
        for k1, keep in _CAND_ROWS8:
            segs.append(jnp.where(sub8 < keep, v0[k1] + col1[0:8], NEG))
        first = jnp.where(sub8 < 2, v0[5], jnp.where(sub8 < 4, v0[6], v0[7]))
        second = jnp.where(sub8 % 2 == 0, v1[0], v1[1])
        segs.append(jnp.where(sub8 < 6, first + second, NEG))
        segs.append(col0[8:16] + v1[0])
        cand = jnp.concatenate(segs, axis=0)
        crank, cvals = _top16(cand)
        taken = jnp.where(crank < float(P_TOPK), 1.0, 0.0)
        z = functools.reduce(lambda acc, v: acc + jnp.exp(v - cvals[0]), cvals, jnp.zeros((1, tb), f32))
        cnt = [jnp.sum(taken[0:16], axis=0, keepdims=True)]
        for i in range(len(_CAND_ROWS8)):
            cnt.append(jnp.sum(taken[16 + 8 * i:24 + 8 * i], axis=0, keepdims=True))
        t5 = taken[48:56]
        for lo in (0, 2, 4):
            cnt.append(jnp.sum(jnp.where((sub8 >= lo) & (sub8 < lo + 2), t5, 0.0), axis=0, keepdims=True))
        for i in range(8):
            cnt.append(taken[56 + i:57 + i])
        n_a = functools.reduce(lambda acc, k: jnp.where(rk[0] == float(k), cnt[k], acc), range(P_TOPK),
                               jnp.zeros((P_NKEYS, tb), f32))
        ta_ref[p, 0] = n_a
        ta_ref[p, 1] = jnp.exp(sc[0] - v0[0])
        tb_ref[p, 0] = rk[1]
        tb_ref[p, 1] = jnp.exp(sc[1] - v1[0]) / z


def _route(hb, wqt, keys, tb):
    n = hb.shape[0]
    spec = pl.BlockSpec((P_HEADS, 2, P_NKEYS, tb), lambda i: (0, 0, 0, i))
    return pl.pallas_call(
        _route_kernel,
        grid=(n // tb,),
        in_specs=[pl.BlockSpec((tb, D_MODEL), lambda i: (i, 0)), _full(wqt.shape), _full(keys.shape)],
        out_specs=[spec, spec],
        out_shape=[jax.ShapeDtypeStruct((P_HEADS, 2, P_NKEYS, n), f32),
                   jax.ShapeDtypeStruct((P_HEADS, 2, P_NKEYS, n), f32)],
        compiler_params=_cparams("parallel"),
        name="peer_route",
    )(hb, wqt, keys)


def _experts_kernel(hb_ref, h_ref, ta_ref, tb_ref, u_ref, vt_ref, g_ref, b_ref, y_ref, acc_s, ht_s, pt_s, *, te):
    j = pl.program_id(1)

    @pl.when(j == 0)
    def _():
        acc_s[...] = jnp.zeros(acc_s.shape, f32)

    ht_s[...] = lax.dot_general(u_ref[...], hb_ref[...], _NT_DIMS, preferred_element_type=f32)
    for aa in range(te // P_NKEYS):
        a = j * (te // P_NKEYS) + aa
        n_rows = [ta_ref[p, 0, pl.ds(a, 1), :] for p in range(P_HEADS)]
        e0_rows = [ta_ref[p, 1, pl.ds(a, 1), :] for p in range(P_HEADS)]
        for lt in range(ht_s.shape[1] // LANES):
            ls = slice(lt * LANES, (lt + 1) * LANES)
            ex = slice(aa * P_NKEYS, (aa + 1) * P_NKEYS)
            w = jnp.zeros((P_NKEYS, LANES), f32)
            for p in range(P_HEADS):
                w = w + jnp.where(tb_ref[p, 0, :, ls] < n_rows[p][:, ls], e0_rows[p][:, ls] * tb_ref[p, 1, :, ls], 0.0)
            hs = ht_s[ex, ls]
            act = 0.5 * hs * (1.0 + lax.erf(hs * (0.5 ** 0.5)))
            pt_s[ex, ls] = (w * act).astype(bf16)
    acc_s[...] += jnp.dot(vt_ref[...], pt_s[...], preferred_element_type=f32)

    @pl.when(j == pl.num_programs(1) - 1)
    def _():
        z = DN_ALPHA * h_ref[...] + acc_s[...].T
        y_ref[...] = _layer_norm(z, g_ref[...], b_ref[...])


def _experts(hb, h, ta, tbl, u, vt, g, b, tb, te):
    n = hb.shape[0]
    row = pl.BlockSpec((tb, D_MODEL), lambda i, j: (i, 0))
    tab = pl.BlockSpec((P_HEADS, 2, P_NKEYS, tb), lambda i, j: (0, 0, 0, i))
    return pl.pallas_call(
        functools.partial(_experts_kernel, te=te),
        grid=(n // tb, P_EXPERTS // te),
        in_specs=[row, row, tab, tab,
                  pl.BlockSpec((te, D_MODEL), lambda i, j: (j, 0)),
                  pl.BlockSpec((D_MODEL, te), lambda i, j: (0, j)),
                  pl.BlockSpec((1, D_MODEL), lambda i, j: (0, 0)),
                  pl.BlockSpec((1, D_MODEL), lambda i, j: (0, 0))],
        out_specs=row,
        out_shape=jax.ShapeDtypeStruct((n, D_MODEL), f32),
        scratch_shapes=[pltpu.VMEM((D_MODEL, tb), f32), pltpu.VMEM((te, tb), f32), pltpu.VMEM((te, tb), bf16)],
        compiler_params=_cparams("parallel", "arbitrary"),
        name="peer_experts",
    )(hb, h, ta, tbl, u, vt, g, b)


def kernel(x_prompt, x_sample, cache_cmp_kv, cache_slc_kv, cache_win_kv, state_C, state_n, state_m, page_table,
           w_in, b_in, norm_a_g, nsa_pe, nsa_w_cmp, w_br_a, w_br_b, w_merge, w_out, ln1_g, ln1_b,
           peer_wq, peer_keys, peer_u, peer_v, ln2_g, ln2_b):
    bp, tp, _ = x_prompt.shape
    bs, ts, _ = x_sample.shape
    tsp = 8
    n_p, n_s = bp * tp, bs * tsp
    past = page_table.shape[1] * PAGE_SIZE

    perm = np.concatenate([np.arange(0, 2048), np.arange(2056, 5640), np.arange(2048, 2056), np.arange(5640, 5688)])
    w_perm = jnp.pad(w_in[:, perm], ((0, 0), (0, C_END - perm.size)))
    b_perm = jnp.pad(b_in[perm], (0, C_END - perm.size))
    w_perm_b = w_perm.astype(bf16)
    slopes = jnp.asarray(2.0 ** (-8.0 * np.arange(1, B_HEADS + 1) / B_HEADS), f32)
    wc = nsa_w_cmp.reshape(2, CMP_BLOCK, 1, B_HD, 1, B_HD)
    eye2 = jnp.eye(2, dtype=f32).reshape(1, 1, 2, 1, 2, 1)
    w2 = (wc * eye2).reshape(2, CMP_BLOCK * LANES, LANES).astype(bf16)
    pe2 = jnp.tile(nsa_pe, (1, 1, 2)).reshape(2, 1, CMP_BLOCK * LANES)

    xs_pad = jnp.pad(x_sample, ((0, 0), (0, tsp - ts), (0, 0)))
    x_all = jnp.concatenate([x_prompt.reshape(n_p, D_MODEL), xs_pad.reshape(n_s, D_MODEL)], axis=0)
    xb = x_all.astype(bf16)
    proj, kvb = _proj(xb, w_perm_b, b_perm.reshape(1, C_END), 256)
    gt = _proj_t(w_perm_b[:, C_SMALL:].T, xb, b_perm[C_SMALL:].reshape(LANES, 1), 512)

    zc = jnp.zeros((bp, A_HEADS, A_DQK, A_DV), f32)
    zn = jnp.zeros((bp, A_HEADS, 1, A_DQK), f32)
    zm = jnp.zeros((bp, A_HEADS, 1, 1), f32)
    ng = norm_a_g.reshape(1, A_HEADS * A_DV)
    ya, p_c, p_n, p_m = _mlstm(proj, gt, ng, zc, zn, zm, row0=0, nb=bp, t=tp, L=512, valid=512)
    ya, s_c, s_n, s_m = _mlstm(proj, gt, ng, state_C, state_n.reshape(bs, A_HEADS, 1, A_DQK),
                               state_m.reshape(bs, A_HEADS, 1, 1), row0=n_p, nb=bs, t=tsp, L=tsp, valid=ts, y_buf=ya)

    wt_kv = w_perm_b[:, C_CMP:C_SMALL].T
    b_kv = b_perm[C_CMP:C_SMALL].reshape(C_SMALL - C_CMP, 1)
    kvt_p = _proj_kvt(wt_kv, xb, b_kv, row0=0, nb=bp, t=tp, tn=512)
    kvt_s = _proj_kvt(wt_kv, xb, b_kv, row0=n_p, nb=1, t=n_s, tn=512)
    kvt_s = [a.reshape(KV_COLS, bs, tsp).transpose(1, 0, 2) for a in kvt_s]
    to_rows = lambda a: a.reshape(a.shape[0], 2, B_KV, B_HD, a.shape[2]).transpose(0, 4, 1, 2, 3)
    new_lanes = lambda a: jnp.pad(a, ((0, 0), (0, 0), (0, LANES - tsp)))

    kvc_p = _compress(proj, pe2, w2, rows=n_p, steps=1, row0=0, colblk=C_CMP // KV_COLS)
    oc, sel_p = _cmp_nat(slopes, proj, kvc_p.reshape(bp, tp // CMP_BLOCK, KV_COLS), nb=bp, t=tp, tq=512)
    os_ = _attn_nat(slopes, proj, kvb, sel_p, nb=bp, t=tp, tq=128, tk=1024, branch=1)
    ow = _attn_nat(slopes, proj, kvb, None, nb=bp, t=tp, tq=128, tk=512, branch=2)

    tk_s = past + PAD_PAGES * PAGE_SIZE
    wb = cache_win_kv.shape[1]
    cache_t = lambda c: c.transpose(0, 2, 3, 4, 1).reshape(c.shape[0], KV_COLS, c.shape[1])
    kvc_s = _compress_pages(page_table, cache_t(cache_cmp_kv), pe2, w2)
    kvc_s = jnp.pad(kvc_s, ((0, 0), (0, tk_s // CMP_BLOCK - kvc_s.shape[1]), (0, 0)))
    win_t = cache_t(cache_win_kv)
    s_win_t = jnp.concatenate([win_t[:, :, ts:], kvt_s[2][:, :, :ts]], axis=2)
    tk_sel = past + PAGE_SIZE
    nblk_s = tk_s // CMP_BLOCK
    expand = jnp.asarray(np.arange(tk_sel)[None, :] // CMP_BLOCK == np.arange(nblk_s)[:, None], bf16)
    oc, os_, ow = _decode(page_table, (oc, os_, ow), slopes, proj, kvc_s.reshape(bs, nblk_s, KV_COLS),
                          cache_t(cache_slc_kv), new_lanes(kvt_s[1]), win_t, new_lanes(kvt_s[2]), expand,
                          tq=tsp, row0=n_p, qpos0=past, win_kpos0=past - wb)

    h1, h1b = _tail(x_all, ya, oc, os_, ow, w_merge.astype(bf16), w_br_a.astype(bf16), w_br_b.astype(bf16),
                    w_out.astype(bf16), ln1_g.reshape(1, D_MODEL), ln1_b.reshape(1, D_MODEL), 256)
    tab_a, tab_b = _route(h1b, peer_wq.T.astype(bf16), peer_keys.astype(bf16), 256)
    y = _experts(h1b, h1, tab_a, tab_b, peer_u.astype(bf16), peer_v.T.astype(bf16), ln2_g.reshape(1, D_MODEL),
                 ln2_b.reshape(1, D_MODEL), 512, 2048)

    y_prompt = y[:n_p].reshape(bp, tp, D_MODEL)
    y_sample = y[n_p:].reshape(bs, tsp, D_MODEL)[:, :ts]
    dt = x_prompt.dtype
    return (y_prompt, y_sample, to_rows(kvt_p[0]), to_rows(kvt_p[1]), to_rows(kvt_p[2][:, :, -min(WINDOW, tp):]),
            p_c.astype(dt), p_n.astype(dt), p_m.astype(dt),
            to_rows(kvt_s[0][:, :, :ts]), to_rows(kvt_s[1][:, :, :ts]), to_rows(s_win_t),
            s_c.astype(state_C.dtype), s_n.astype(state_C.dtype), s_m.astype(state_C.dtype))
```
